```python
import math
import jax
import jax.numpy as jnp
from jax import lax
import numpy as np

D_MODEL = 2048
BATCH = 16
SEQ = 2048
DEPTH = 1
DEC_BATCH = 2
DEC_SEQ = 8192
PAST_LEN = 128

A_HEADS = 8
A_DK = 128
A_DV = 128
B_HEADS = 4
B_DK = 128
B_DV = 256
GLA_LOWRANK = 16
GLA_GATE_NORM = 16.0
CONV_K = 5
CHUNK = 64
N_EXPERTS = 16
EC_CAPACITY = 2
EXPERT_FF = D_MODEL // 2
EPS = 1e-6

A_QK = A_HEADS * A_DK
A_V = A_HEADS * A_DV
B_QK = B_HEADS * B_DK
B_V = B_HEADS * B_DV
MIX_WIDTH = A_V + B_V
CONV_CH = 2 * A_QK + A_V
IN_SPLITS = (A_QK, A_QK, A_V, A_V, A_HEADS, A_HEADS, A_HEADS, A_HEADS,
             B_QK, B_QK, B_V, B_V, GLA_LOWRANK, GLA_LOWRANK)
IN_WIDTH = 2 * A_QK + 2 * A_V + 4 * A_HEADS + 2 * B_QK + 2 * B_V + 2 * GLA_LOWRANK

kernel_name = "hybrid_bidir_gdn_gla_ec_moe"


def _rmsnorm(x, w):
    xf = x.astype(jnp.float32)
    y = xf * lax.rsqrt(jnp.mean(xf * xf, axis=-1, keepdims=True) + EPS)
    return (y * w.astype(jnp.float32)).astype(x.dtype)


def _l2norm(x):
    return x * lax.rsqrt(jnp.sum(x * x, axis=-1, keepdims=True) + EPS)


def _gated_head_norm(o, gate, w):
    y = o * lax.rsqrt(jnp.mean(o * o, axis=-1, keepdims=True) + EPS) * w.astype(jnp.float32)
    g = gate.astype(jnp.float32).reshape(o.shape)
    return (y * jax.nn.silu(g)).reshape(o.shape[0], o.shape[1], -1)


def _split_cols(p):
    outs = []
    off = 0
    for width in IN_SPLITS:
        outs.append(p[..., off:off + width])
        off += width
    return outs


def _flip(t):
    return jnp.flip(t, axis=1)


def _centred_conv(x, w):
    return lax.conv_general_dilated(
        x, w[:, None, :].astype(x.dtype), window_strides=(1,),
        padding=[(CONV_K // 2, CONV_K // 2)],
        dimension_numbers=("NWC", "WIO", "NWC"),
        feature_group_count=x.shape[-1])


def _chunk_masks():
    idx = jnp.arange(CHUNK)
    return idx[:, None] >= idx[None, :], idx[:, None] > idx[None, :]


def _to_chunks(t):
    bsz, seq, nh, d = t.shape
    return t.reshape(bsz, seq // CHUNK, CHUNK, nh, d).transpose(1, 0, 3, 2, 4)


def _from_chunks(o):
    n, bsz, nh, c, d = o.shape
    return o.transpose(1, 0, 3, 2, 4).reshape(bsz, n * c, nh, d)


def _delta_rule_chunked(q, k, v, beta, g):
    bsz, seq, nh, dk = q.shape
    dv = v.shape[-1]
    qc, kc, vc = _to_chunks(q), _to_chunks(k), _to_chunks(v)
    bc = _to_chunks(beta[..., None])[..., 0]
    gc = jnp.cumsum(_to_chunks(g[..., None])[..., 0], axis=-1)
    incl, strict = _chunk_masks()
    decay = jnp.exp(jnp.where(incl, gc[..., :, None] - gc[..., None, :], -jnp.inf))
    kb = kc * bc[..., None]
    a_strict = jnp.where(strict, jnp.einsum('nbhid,nbhjd->nbhij', kb, kc) * decay, 0.0)
    rhs = jnp.concatenate([vc * bc[..., None], kb * jnp.exp(gc)[..., None]], axis=-1)
    sol = lax.linalg.triangular_solve(a_strict, rhs, left_side=True, lower=True,
                                      unit_diagonal=True)
    u, w = sol[..., :dv], sol[..., dv:]
    attn = jnp.einsum('nbhid,nbhjd->nbhij', qc, kc) * decay
    q_dec = qc * jnp.exp(gc)[..., None]
    k_dec = kc * jnp.exp(gc[..., -1:] - gc)[..., None]
    g_last = jnp.exp(gc[..., -1])

    def step(state, xs):
        u_n, w_n, attn_n, qd_n, kd_n, gl_n = xs
        v_new = u_n - jnp.einsum('bhcd,bhde->bhce', w_n, state)
        o_n = (jnp.einsum('bhcd,bhde->bhce', qd_n, state)
               + jnp.einsum('bhij,bhje->bhie', attn_n, v_new))
        state = state * gl_n[..., None, None] + jnp.einsum('bhcd,bhce->bhde', kd_n, v_new)
        return state, o_n

    state0 = jnp.zeros((bsz, nh, dk, dv), jnp.float32)
    _, o = lax.scan(step, state0, (u, w, attn, q_dec, k_dec, g_last))
    return _from_chunks(o)


def _gla_chunked(q, k, v, log_a):
    bsz, seq, nh, dk = q.shape
    dv = v.shape[-1]
    qc, kc, vc = _to_chunks(q), _to_chunks(k), _to_chunks(v)
    bcum = jnp.cumsum(_to_chunks(log_a), axis=-2)
    incl, _ = _chunk_masks()

    def step(state, xs):
        q_n, k_n, v_n, b_n = xs
        rel = jnp.exp(jnp.where(incl[:, :, None],
                                b_n[..., :, None, :] - b_n[..., None, :, :], -jnp.inf))
        attn = jnp.einsum('bhijd,bhjd->bhij', q_n[..., :, None, :] * rel, k_n)
        b_last = b_n[..., -1:, :]
        o_n = (jnp.einsum('bhcd,bhde->bhce', q_n * jnp.exp(b_n), state)
               + jnp.einsum('bhij,bhje->bhie', attn, v_n))
        state = (jnp.exp(b_last)[..., 0, :, None] * state
                 + jnp.einsum('bhcd,bhce->bhde', k_n * jnp.exp(b_last - b_n), v_n))
        return state, o_n

    state0 = jnp.zeros((bsz, nh, dk, dv), jnp.float32)
    _, o = lax.scan(step, state0, (qc, kc, vc, bcum))
    return _from_chunks(o)


def _gated_deltanet(q_in, k_in, v_in, gate, beta_f, beta_b, a_f, a_b, conv_w,
                    a_log_f, a_log_b, dt_bias_f, dt_bias_b, norm_w):
    bsz, seq = q_in.shape[:2]
    qkv = jax.nn.silu(_centred_conv(jnp.concatenate([q_in, k_in, v_in], axis=-1),
                                    conv_w).astype(jnp.float32))
    q = qkv[..., :A_QK].reshape(bsz, seq, A_HEADS, A_DK)
    k = qkv[..., A_QK:2 * A_QK].reshape(bsz, seq, A_HEADS, A_DK)
    v = qkv[..., 2 * A_QK:].reshape(bsz, seq, A_HEADS, A_DV)
    q = _l2norm(q) * (A_DK ** -0.5)
    k = _l2norm(k)

    def log_decay(a, a_log, dt_bias):
        return -jnp.exp(a_log.astype(jnp.float32)) * jax.nn.softplus(
            a.astype(jnp.float32) + dt_bias.astype(jnp.float32))

    o_fwd = _delta_rule_chunked(q, k, v, jax.nn.sigmoid(beta_f.astype(jnp.float32)),
                                log_decay(a_f, a_log_f, dt_bias_f))
    o_bwd = _flip(_delta_rule_chunked(
        _flip(q), _flip(k), _flip(v), _flip(jax.nn.sigmoid(beta_b.astype(jnp.float32))),
        _flip(log_decay(a_b, a_log_b, dt_bias_b))))
    return _gated_head_norm(o_fwd + o_bwd, gate, norm_w)


def _gla(q_in, k_in, v_in, gate, lr_f, lr_b, gla_w_f, gla_b_f, gla_w_b, gla_b_b, norm_w):
    bsz, seq = q_in.shape[:2]
    q = q_in.astype(jnp.float32).reshape(bsz, seq, B_HEADS, B_DK) * (B_DK ** -0.5)
    k = k_in.astype(jnp.float32).reshape(bsz, seq, B_HEADS, B_DK)
    v = v_in.astype(jnp.float32).reshape(bsz, seq, B_HEADS, B_DV)

    def log_gate(lr, w, b):
        z = jnp.dot(lr.astype(jnp.float32), w.astype(jnp.float32)) + b.astype(jnp.float32)
        return (jax.nn.log_sigmoid(z) / GLA_GATE_NORM).reshape(bsz, seq, B_HEADS, B_DK)

    o_fwd = _gla_chunked(q, k, v, log_gate(lr_f, gla_w_f, gla_b_f))
    o_bwd = _flip(_gla_chunked(_flip(q), _flip(k), _flip(v),
                               _flip(log_gate(lr_b, gla_w_b, gla_b_b))))
    return _gated_head_norm(o_fwd + o_bwd, gate, norm_w)


def _expert_choice_moe(h, w_router, w1, w3, w2):
    bsz, seq, d = h.shape
    n_tok = bsz * seq
    cap = EC_CAPACITY * n_tok // N_EXPERTS
    x = h.reshape(n_tok, d)
    probs = jax.nn.softmax(jnp.dot(x.astype(jnp.float32), w_router.astype(jnp.float32)),
                           axis=-1)
    gate, tok = lax.top_k(probs.T, cap)
    xs = x[tok]
    hid = (jax.nn.silu(jnp.einsum('ecd,edf->ecf', xs, w1))
           * jnp.einsum('ecd,edf->ecf', xs, w3))
    ye = jnp.einsum('ecf,efd->ecd', hid, w2) * gate[..., None].astype(h.dtype)
    out = jnp.zeros((n_tok, d), h.dtype).at[tok.reshape(-1)].add(ye.reshape(-1, d))
    return out.reshape(bsz, seq, d)


def _layer(x, ln1, w_in, conv_w, a_log_f, a_log_b, dt_bias_f, dt_bias_b, norm_a,
           gla_w_f, gla_b_f, gla_w_b, gla_b_b, norm_b, w_out, ln2, w_router, w1, w3, w2):
    hn = _rmsnorm(x, ln1)
    (qa, ka, va, ga, beta_f, beta_b, a_f, a_b,
     qb, kb, vb, gb, lr_f, lr_b) = _split_cols(jnp.dot(hn, w_in))
    o_a = _gated_deltanet(qa, ka, va, ga, beta_f, beta_b, a_f, a_b, conv_w,
                          a_log_f, a_log_b, dt_bias_f, dt_bias_b, norm_a)
    o_b = _gla(qb, kb, vb, gb, lr_f, lr_b, gla_w_f, gla_b_f, gla_w_b, gla_b_b, norm_b)
    mix = jnp.concatenate([o_a, o_b], axis=-1).astype(x.dtype)
    h = x + jnp.dot(mix, w_out)
    return h + _expert_choice_moe(_rmsnorm(h, ln2), w_router, w1, w3, w2)


def _trunk(x, layer_params, ln_f):
    for l in range(DEPTH):
        x = _layer(x, *[p[l] for p in layer_params])
    return _rmsnorm(x, ln_f)


def setup_inputs(seed: int = 0) -> dict:
    key = jax.random.key(seed)
    ks = jax.random.split(key, 24)
    f32 = jnp.float32

    def nrm(k, shape, scale):
        return jax.random.normal(k, shape, f32) * scale

    def gain(k, shape):
        return 1.0 + 0.01 * jax.random.normal(k, shape, f32)

    def dt_bias(k):
        dt = jnp.exp(jax.random.uniform(k, (DEPTH, A_HEADS), f32,
                                        minval=math.log(1e-3), maxval=math.log(1e-1)))
        return dt + jnp.log(-jnp.expm1(-dt))

    return {
        "x_prompt": jax.random.normal(ks[0], (BATCH, SEQ, D_MODEL), f32),
        "x_sample": jax.random.normal(ks[1], (DEC_BATCH, DEC_SEQ, D_MODEL), f32),
        "ln1": gain(ks[2], (DEPTH, D_MODEL)),
        "w_in": nrm(ks[3], (DEPTH, D_MODEL, IN_WIDTH), D_MODEL ** -0.5),
        "conv_w": nrm(ks[4], (DEPTH, CONV_K, CONV_CH), CONV_K ** -0.5),
        "a_log_f": jnp.log(jax.random.uniform(ks[5], (DEPTH, A_HEADS), f32, 1.0, 16.0)),
        "a_log_b": jnp.log(jax.random.uniform(ks[6], (DEPTH, A_HEADS), f32, 1.0, 16.0)),
        "dt_bias_f": dt_bias(ks[7]),
        "dt_bias_b": dt_bias(ks[8]),
        "norm_a": gain(ks[9], (DEPTH, A_DV)),
        "gla_w_f": nrm(ks[10], (DEPTH, GLA_LOWRANK, B_QK), GLA_LOWRANK ** -0.5),
        "gla_b_f": nrm(ks[11], (DEPTH, B_QK), 0.1),
        "gla_w_b": nrm(ks[12], (DEPTH, GLA_LOWRANK, B_QK), GLA_LOWRANK ** -0.5),
        "gla_b_b": nrm(ks[13], (DEPTH, B_QK), 0.1),
        "norm_b": gain(ks[14], (DEPTH, B_DV)),
        "w_out": nrm(ks[15], (DEPTH, MIX_WIDTH, D_MODEL), MIX_WIDTH ** -0.5),
        "ln2": gain(ks[16], (DEPTH, D_MODEL)),
        "w_router": nrm(ks[17], (DEPTH, D_MODEL, N_EXPERTS), D_MODEL ** -0.5),
        "w1": nrm(ks[18], (DEPTH, N_EXPERTS, D_MODEL, EXPERT_FF), D_MODEL ** -0.5),
        "w3": nrm(ks[19], (DEPTH, N_EXPERTS, D_MODEL, EXPERT_FF), D_MODEL ** -0.5),
        "w2": nrm(ks[20], (DEPTH, N_EXPERTS, EXPERT_FF, D_MODEL), EXPERT_FF ** -0.5),
        "ln_f": gain(ks[21], (D_MODEL,)),
    }


def reference(x_prompt, x_sample, ln1, w_in, conv_w, a_log_f, a_log_b, dt_bias_f, dt_bias_b,
              norm_a, gla_w_f, gla_b_f, gla_w_b, gla_b_b, norm_b, w_out, ln2, w_router,
              w1, w3, w2, ln_f):
    layer_params = (ln1, w_in, conv_w, a_log_f, a_log_b, dt_bias_f, dt_bias_b, norm_a,
                    gla_w_f, gla_b_f, gla_w_b, gla_b_b, norm_b, w_out, ln2, w_router,
                    w1, w3, w2)
    y_prompt = _trunk(x_prompt, layer_params, ln_f)
    y_sample = _trunk(x_sample, layer_params, ln_f)
    return (y_prompt, y_sample)
```

```python
import functools

import jax
import jax.numpy as jnp
import numpy as np
from jax import lax
from jax.experimental import pallas as pl
from jax.experimental.pallas import tpu as pltpu

F32 = jnp.float32
BF16 = jnp.bfloat16

D_MODEL = 2048
A_HEADS, A_DK, A_DV = 8, 128, 128
B_HEADS, B_DK, B_DV = 4, 128, 256
GLA_LOWRANK = 16
GLA_GATE_NORM = 16.0
CONV_K = 5
CHUNK = 64
N_EXPERTS = 16
EC_CAPACITY = 2
EXPERT_FF = D_MODEL // 2
EPS = 1e-6

A_QK = A_HEADS * A_DK
A_V = A_HEADS * A_DV
B_QK = B_HEADS * B_DK
B_V = B_HEADS * B_DV
MAIN_WIDTH = 2 * A_QK + 2 * A_V + 2 * B_QK + 2 * B_V
SMALL_WIDTH = 128
LR_OFF = 4 * A_HEADS
V7X_VMEM_LIMIT = 56 * 1024 * 1024
CONV_HALO = 16
SCAN_ROWS = 256
CPS = SCAN_ROWS // CHUNK
GLA_LEVELS = 6
GDN_HEADS_PER_PASS = 4


def _cparams(sem):
    return pltpu.CompilerParams(dimension_semantics=sem, vmem_limit_bytes=V7X_VMEM_LIMIT)


def _dot(a, b):
    return jnp.dot(a.astype(BF16), b.astype(BF16), preferred_element_type=F32)


def _dot_nt(a, b):
    return lax.dot_general(a.astype(BF16), b.astype(BF16), (((1,), (1,)), ((), ())),
                           preferred_element_type=F32)


def _dot_tn(a, b):
    return lax.dot_general(a.astype(BF16), b.astype(BF16), (((0,), (0,)), ((), ())),
                           preferred_element_type=F32)


def _silu(x):
    return x * (1.0 / (1.0 + jnp.exp(-x)))


def _in_proj_kernel(x_ref, ln_ref, wm_ref, ws_ref, sp_ref, main_ref, small_ref, hn_ref):
    j = pl.program_id(1)

    @pl.when(j == 0)
    def _():
        xf = x_ref[...]
        y = xf * lax.rsqrt(jnp.mean(xf * xf, axis=-1, keepdims=True) + EPS) * ln_ref[...]
        hn = y.astype(BF16)
        hn_ref[...] = hn
        s = jnp.dot(hn, ws_ref[...], preferred_element_type=F32)
        lane = lax.broadcasted_iota(jnp.int32, s.shape, 1)
        neg_a = -jnp.exp(sp_ref[0:1, :])
        z = s + sp_ref[1:2, :]
        softplus = jnp.maximum(z, 0.0) + jnp.log(1.0 + jnp.exp(-jnp.abs(z)))
        sig = 1.0 / (1.0 + jnp.exp(-s))
        small_ref[...] = jnp.where(lane < 2 * A_HEADS, sig,
                                   jnp.where(lane < 4 * A_HEADS, neg_a * softplus, s))

    main_ref[...] = jnp.dot(hn_ref[...], wm_ref[...], preferred_element_type=F32).astype(BF16)


def _in_proj(x2d, ln1, w_main, w_small, small_params, tm, tn):
    t = x2d.shape[0]
    return pl.pallas_call(
        _in_proj_kernel,
        grid=(t // tm, MAIN_WIDTH // tn),
        in_specs=[
            pl.BlockSpec((tm, D_MODEL), lambda i, j: (i, 0)),
            pl.BlockSpec((1, D_MODEL), lambda i, j: (0, 0)),
            pl.BlockSpec((D_MODEL, tn), lambda i, j: (0, j)),
            pl.BlockSpec((D_MODEL, SMALL_WIDTH), lambda i, j: (0, 0)),
            pl.BlockSpec((8, SMALL_WIDTH), lambda i, j: (0, 0)),
        ],
        out_specs=[
            pl.BlockSpec((tm, tn), lambda i, j: (i, j)),
            pl.BlockSpec((tm, SMALL_WIDTH), lambda i, j: (i, 0)),
        ],
        out_shape=[
            jax.ShapeDtypeStruct((t, MAIN_WIDTH), BF16),
            jax.ShapeDtypeStruct((t, SMALL_WIDTH), F32),
        ],
        scratch_shapes=[pltpu.VMEM((tm, D_MODEL), BF16)],
        compiler_params=_cparams(("arbitrary", "arbitrary")),
        name="in_proj",
    )(x2d, ln1, w_main, w_small, small_params)


def _chunk_iotas():
    ii = lax.broadcasted_iota(jnp.int32, (CHUNK, CHUNK), 0)
    jj = lax.broadcasted_iota(jnp.int32, (CHUNK, CHUNK), 1)
    return ii, jj


def _row_to_col(row, eye):
    return jnp.sum(jnp.where(eye, row, 0.0), axis=1, keepdims=True)


def _col_to_row(col, eye):
    return jnp.sum(jnp.where(eye, col, 0.0), axis=0, keepdims=True)


def _l2norm(x):
    return x * lax.rsqrt(jnp.sum(x * x, axis=-1, keepdims=True) + EPS)


def _delta_chunks(qs, ks, vs, beta_rows, g_rows, states, revs):
    ii, jj = _chunk_iotas()
    eye = ii == jj
    nc = range(len(qs))
    incl = [(ii <= jj) if r else (ii >= jj) for r in revs]
    strict = [(ii < jj) if r else (ii > jj) for r in revs]
    gc_col = [jnp.sum(jnp.where(incl[c], g_rows[c], 0.0), axis=1, keepdims=True) for c in nc]
    gc_row = [_col_to_row(gc_col[c], eye) for c in nc]
    beta_col = [_row_to_col(beta_rows[c], eye) for c in nc]
    g_tot = [jnp.sum(g_rows[c], axis=1, keepdims=True) for c in nc]
    decay = [jnp.where(incl[c], jnp.exp(jnp.where(incl[c], gc_col[c] - gc_row[c], 0.0)), 0.0)
             for c in nc]
    kb = [ks[c] * beta_col[c] for c in nc]
    kk = [_dot_nt(kb[c], ks[c]) for c in nc]
    qk = [_dot_nt(qs[c], ks[c]) for c in nc]
    p = [jnp.where(strict[c], -kk[c] * decay[c], 0.0) for c in nc]
    toff = p
    for _ in range(5):
        p = [_dot(p[c], p[c]) for c in nc]
        tp = [_dot(toff[c], p[c]) for c in nc]
        toff = [toff[c] + p[c] + tp[c] for c in nc]
    e_gc = [jnp.exp(gc_col[c]) for c in nc]
    rhs = [jnp.concatenate([vs[c] * beta_col[c], kb[c] * e_gc[c]], axis=1) for c in nc]
    sol = [rhs[c] + _dot(toff[c], rhs[c]) for c in nc]
    attn = [jnp.where(incl[c], qk[c] * decay[c], 0.0) for c in nc]
    ws = [_dot(sol[c][:, A_DV:], states[c]) for c in nc]
    qs_state = [_dot(qs[c] * e_gc[c], states[c]) for c in nc]
    v_new = [sol[c][:, :A_DV] - ws[c] for c in nc]
    av = [_dot(attn[c], v_new[c]) for c in nc]
    kv = [_dot_tn(ks[c] * jnp.exp(g_tot[c] - gc_col[c]), v_new[c]) for c in nc]
    outs = [qs_state[c] + av[c] for c in nc]
    new_states = [states[c] * jnp.exp(g_tot[c]) + kv[c] for c in nc]
    return outs, new_states


def _gdn_kernel(blk_f, prv_f, nxt_f, blk_b, prv_b, nxt_b, cw_ref, sc_f, sc_b,
                of_ref, ob_ref, pad_ref, st_ref):
    n = pl.program_id(1)
    nb = pl.num_programs(1)

    @pl.when(n == 0)
    def _():
        st_ref[...] = jnp.zeros_like(st_ref)

    at_start = (n == 0, n == nb - 1)
    at_end = (n == nb - 1, n == 0)
    for d, (blk, prv, nxt) in enumerate(((blk_f, prv_f, nxt_f), (blk_b, prv_b, nxt_b))):
        pad_ref[d, 0:CONV_HALO, :] = jnp.where(at_start[d], jnp.zeros_like(prv[0]), prv[0])
        pad_ref[d, CONV_HALO:CONV_HALO + SCAN_ROWS, :] = blk[0]
        pad_ref[d, CONV_HALO + SCAN_ROWS:, :] = jnp.where(at_end[d], jnp.zeros_like(nxt[0]), nxt[0])

    lo = CONV_HALO - CONV_K // 2

    def conv(d, row0, a, h):
        col = a * A_QK + h * A_DK
        win = pad_ref[d, pl.ds(row0, CHUNK + 2 * CONV_HALO), col:col + A_DK].astype(F32)
        w = cw_ref[:, col:col + A_DK]
        acc = w[0:1, :] * win[lo:lo + CHUNK, :]
        for j in range(1, CONV_K):
            acc = acc + w[j:j + 1, :] * win[lo + j:lo + j + CHUNK, :]
        return _silu(acc)

    def body(s, carry):
        for h0 in range(0, A_HEADS, GDN_HEADS_PER_PASS):
            chains = []
            for d, sc_ref in enumerate((sc_f, sc_b)):
                sc = s if d == 0 else CPS - 1 - s
                row0 = pl.multiple_of(sc * CHUNK, CHUNK)
                for h in range(h0, h0 + GDN_HEADS_PER_PASS):
                    chains.append((d, h, sc, row0, sc_ref))
            qs = [_l2norm(conv(d, row0, 0, h)) * (A_DK ** -0.5) for d, h, sc, row0, _ in chains]
            ks = [_l2norm(conv(d, row0, 1, h)) for d, h, sc, row0, _ in chains]
            vs = [conv(d, row0, 2, h) for d, h, sc, row0, _ in chains]
            betas = [r[0, 0, d, h, pl.ds(sc, 1), :] for d, h, sc, row0, r in chains]
            gs = [r[0, 0, 2 + d, h, pl.ds(sc, 1), :] for d, h, sc, row0, r in chains]
            states = [st_ref[d, h] for d, h, sc, row0, _ in chains]
            outs, new_states = _delta_chunks(qs, ks, vs, betas, gs, states,
                                             [d == 1 for d, *_ in chains])
            for (d, h, sc, row0, _), o, st in zip(chains, outs, new_states):
                st_ref[d, h] = st
                out_ref = of_ref if d == 0 else ob_ref
                out_ref[0, pl.ds(row0, CHUNK), h * A_DV:(h + 1) * A_DV] = o.astype(out_ref.dtype)
        return carry

    lax.fori_loop(0, CPS, body, 0)


def _scal_layout(small):
    bsz, seq = small.shape[:2]
    s = small[..., :4 * A_HEADS].reshape(bsz, seq // SCAN_ROWS, CPS, CHUNK, 4, A_HEADS)
    return jnp.transpose(s, (0, 1, 4, 5, 2, 3))


def _conv_layout(conv_w):
    return jnp.pad(conv_w.astype(F32), ((0, 8 - CONV_K), (0, 0)))


def _gdn(main3, conv8, scal):
    bsz, seq, _ = main3.shape
    nb = seq // SCAN_ROWS
    hb = SCAN_ROWS // CONV_HALO
    nhalo = seq // CONV_HALO
    width = 2 * A_QK + A_V

    def fwd(b, n):
        return n

    def bwd(b, n):
        return nb - 1 - n

    def stream(blk_of):
        return [
            pl.BlockSpec((1, SCAN_ROWS, width), lambda b, n: (b, blk_of(b, n), 0)),
            pl.BlockSpec((1, CONV_HALO, width),
                         lambda b, n: (b, jnp.maximum(blk_of(b, n) * hb - 1, 0), 0)),
            pl.BlockSpec((1, CONV_HALO, width),
                         lambda b, n: (b, jnp.minimum((blk_of(b, n) + 1) * hb, nhalo - 1), 0)),
        ]

    def sc_spec(blk_of):
        return pl.BlockSpec((1, 1, 4, A_HEADS, CPS, CHUNK),
                            lambda b, n: (b, blk_of(b, n), 0, 0, 0, 0))

    out_sds = jax.ShapeDtypeStruct((bsz, seq, A_V), BF16)
    return pl.pallas_call(
        _gdn_kernel,
        grid=(bsz, nb),
        in_specs=stream(fwd) + stream(bwd) + [
            pl.BlockSpec((8, width), lambda b, n: (0, 0)),
            sc_spec(fwd), sc_spec(bwd),
        ],
        out_specs=[
            pl.BlockSpec((1, SCAN_ROWS, A_V), lambda b, n: (b, n, 0)),
            pl.BlockSpec((1, SCAN_ROWS, A_V), lambda b, n: (b, nb - 1 - n, 0)),
        ],
        out_shape=[out_sds, out_sds],
        scratch_shapes=[
            pltpu.VMEM((2, SCAN_ROWS + 2 * CONV_HALO, width), BF16),
            pltpu.VMEM((2, A_HEADS, A_DK, A_DV), F32),
        ],
        compiler_params=_cparams(("arbitrary", "arbitrary")),
        name="gdn",
    )(main3, main3, main3, main3, main3, main3, conv8, scal, scal)


def _gla_level_weights():
    c = CHUNK
    w = np.zeros((2, 8 * c, c), np.float32)
    for d in range(2):
        for l in range(GLA_LEVELS):
            s = c >> (l + 1)
            for i in range(c):
                mid = (i // (2 * s)) * 2 * s + s
                if d == 0:
                    ts = range(mid, i + 1) if i >= mid else range(i + 1, mid)
                else:
                    ts = range(i, mid) if i < mid else range(mid, i)
                for t in ts:
                    w[d, l * c + i, t] = 1.0
        for i in range(c):
            for t in range(c):
                before = t <= i if d == 0 else t >= i
                w[d, 6 * c + i, t] = 1.0 if before else 0.0
                w[d, 7 * c + i, t] = 0.0 if before else 1.0
    return w


def _gla_kernel(qk_f, v_f, sm_f, qk_b, v_b, sm_b, wg_ref, bg_ref, wl_ref, of_ref, ob_ref, st_ref):
    n = pl.program_id(1)

    @pl.when(n == 0)
    def _():
        st_ref[...] = jnp.zeros_like(st_ref)

    ii, jj = _chunk_iotas()
    eye = ii == jj
    tok = lax.broadcasted_iota(jnp.int32, (CHUNK, 1), 0)

    def body(s, carry):
        for d, (qk_ref, v_ref, sm_ref, out_ref) in enumerate(
                ((qk_f, v_f, sm_f, of_ref), (qk_b, v_b, sm_b, ob_ref))):
            sc = s if d == 0 else CPS - 1 - s
            rows = pl.ds(pl.multiple_of(sc * CHUNK, CHUNK), CHUNK)
            z = _dot(sm_ref[0, rows, :], wg_ref[d]) + bg_ref[d]
            la = (jnp.minimum(z, 0.0) - jnp.log(1.0 + jnp.exp(-jnp.abs(z)))) * (1.0 / GLA_GATE_NORM)
            hi = la.astype(BF16)
            r1 = la - hi.astype(F32)
            mid = r1.astype(BF16)
            low = (r1 - mid.astype(F32)).astype(BF16)
            wl = wl_ref[d]
            ex = (jnp.dot(wl, hi, preferred_element_type=F32)
                  + jnp.dot(wl, mid, preferred_element_type=F32)
                  + jnp.dot(wl, low, preferred_element_type=F32))
            e_all = jnp.exp(ex)
            q_all = qk_ref[0, rows, :B_QK].astype(F32) * (B_DK ** -0.5)
            k_all = qk_ref[0, rows, B_QK:].astype(F32)
            qs, ks, masks = [], [], []
            for l in range(GLA_LEVELS):
                half = CHUNK >> (l + 1)
                right = (tok // half) % 2 == 1
                q_side = right if d == 0 else jnp.logical_not(right)
                e_l = e_all[l * CHUNK:(l + 1) * CHUNK]
                qs.append(jnp.where(q_side, q_all * e_l, 0.0).astype(BF16))
                ks.append(jnp.where(q_side, 0.0, k_all * e_l).astype(BF16))
                masks.append(ii // (2 * half) == jj // (2 * half))
            e_b = e_all[6 * CHUNK:7 * CHUNK]
            q_dec = (q_all * e_b).astype(BF16)
            k_dec = (k_all * e_all[7 * CHUNK:]).astype(BF16)
            e_tot = e_b[CHUNK - 1:CHUNK] if d == 0 else e_b[0:1]
            for h in range(B_HEADS):
                ck = slice(h * B_DK, (h + 1) * B_DK)
                cv = slice(h * B_DV, (h + 1) * B_DV)
                attn = jnp.where(eye, _dot_nt(q_all[:, ck], k_all[:, ck]), 0.0)
                for l in range(GLA_LEVELS):
                    attn = attn + jnp.where(masks[l], _dot_nt(qs[l][:, ck], ks[l][:, ck]), 0.0)
                v = v_ref[0, rows, cv]
                st = st_ref[d, h]
                o = _dot_nt(q_dec[:, ck], st) + _dot(attn, v)
                st_ref[d, h] = st * e_tot[:, ck] + _dot_tn(v, k_dec[:, ck])
                out_ref[0, rows, cv] = o.astype(out_ref.dtype)
        return carry

    lax.fori_loop(0, CPS, body, 0)


def _gla(main3, small3, wg, bg, wl):
    bsz, seq, _ = main3.shape
    nb = seq // SCAN_ROWS
    qk_blk = (2 * A_QK + 2 * A_V) // (2 * B_QK)
    v_blk = (2 * A_QK + 2 * A_V + 2 * B_QK) // B_V

    def stream(blk_of):
        return [
            pl.BlockSpec((1, SCAN_ROWS, 2 * B_QK), lambda b, n: (b, blk_of(n), qk_blk)),
            pl.BlockSpec((1, SCAN_ROWS, B_V), lambda b, n: (b, blk_of(n), v_blk)),
            pl.BlockSpec((1, SCAN_ROWS, SMALL_WIDTH), lambda b, n: (b, blk_of(n), 0)),
        ]

    out_sds = jax.ShapeDtypeStruct((bsz, seq, B_V), BF16)
    return pl.pallas_call(
        _gla_kernel,
        grid=(bsz, nb),
        in_specs=stream(lambda n: n) + stream(lambda n: nb - 1 - n) + [
            pl.BlockSpec((2, SMALL_WIDTH, B_QK), lambda b, n: (0, 0, 0)),
            pl.BlockSpec((2, 1, B_QK), lambda b, n: (0, 0, 0)),
            pl.BlockSpec((2, 8 * CHUNK, CHUNK), lambda b, n: (0, 0, 0)),
        ],
        out_specs=[
            pl.BlockSpec((1, SCAN_ROWS, B_V), lambda b, n: (b, n, 0)),
            pl.BlockSpec((1, SCAN_ROWS, B_V), lambda b, n: (b, nb - 1 - n, 0)),
        ],
        out_shape=[out_sds, out_sds],
        scratch_shapes=[pltpu.VMEM((2, B_HEADS, B_DV, B_DK), F32)],
        compiler_params=_cparams(("arbitrary", "arbitrary")),
        name="gla",
    )(main3, main3, small3, main3, main3, small3, wg, bg, wl)


def _gla_gate_weights(gla_w_f, gla_b_f, gla_w_b, gla_b_b):
    wg = jnp.zeros((2, SMALL_WIDTH, B_QK), F32)
    wg = wg.at[0, LR_OFF:LR_OFF + GLA_LOWRANK].set(gla_w_f.astype(F32))
    wg = wg.at[1, LR_OFF + GLA_LOWRANK:LR_OFF + 2 * GLA_LOWRANK].set(gla_w_b.astype(F32))
    bg = jnp.stack([gla_b_f, gla_b_b]).astype(F32).reshape(2, 1, B_QK)
    return wg.astype(BF16), bg


def _out_proj_kernel(oaf, oab, obf, obb, ga, gb, x_ref, wo_ref, na_ref, nb_ref, ln2_ref,
                     wrh_ref, wrl_ref, h_ref, hn_ref, pt_ref, mix_ref):
    def head_norm(of_ref, ob_ref, g_ref, w_ref, width, heads, base):
        for h in range(heads):
            c = slice(h * width, (h + 1) * width)
            o = of_ref[:, c].astype(F32) + ob_ref[:, c].astype(F32)
            y = o * lax.rsqrt(jnp.mean(o * o, axis=-1, keepdims=True) + EPS) * w_ref[...]
            mix_ref[:, base + h * width:base + (h + 1) * width] = (
                y * _silu(g_ref[:, c].astype(F32))).astype(BF16)

    head_norm(oaf, oab, ga, na_ref, A_DV, A_HEADS, 0)
    head_norm(obf, obb, gb, nb_ref, B_DV, B_HEADS, A_V)
    hres = x_ref[...] + jnp.dot(mix_ref[...], wo_ref[...], preferred_element_type=F32)
    h_ref[...] = hres
    hn = hres * lax.rsqrt(jnp.mean(hres * hres, axis=-1, keepdims=True) + EPS) * ln2_ref[...]
    hi = hn.astype(BF16)
    hn_ref[...] = hi
    lo = (hn - hi.astype(F32)).astype(BF16)
    lt = _dot_nt(wrh_ref[...], hi) + _dot_nt(wrl_ref[...], hi) + _dot_nt(wrh_ref[...], lo)
    e = jnp.exp(lt - jnp.max(lt, axis=0, keepdims=True))
    pt_ref[...] = e / jnp.sum(e, axis=0, keepdims=True)


def _out_proj(oaf, oab, obf, obb, main, x2d, wo, na, nb, ln2, wrh, wrl, tm):
    t = x2d.shape[0]
    ga_blk = (2 * A_QK + A_V) // A_V
    gb_blk = (2 * A_QK + 2 * A_V + 2 * B_QK + B_V) // B_V
    row = lambda w: pl.BlockSpec((tm, w), lambda i: (i, 0))
    full = lambda a, b: pl.BlockSpec((a, b), lambda i: (0, 0))
    return pl.pallas_call(
        _out_proj_kernel,
        grid=(t // tm,),
        in_specs=[
            row(A_V), row(A_V), row(B_V), row(B_V),
            pl.BlockSpec((tm, A_V), lambda i: (i, ga_blk)),
            pl.BlockSpec((tm, B_V), lambda i: (i, gb_blk)),
            row(D_MODEL), full(D_MODEL, D_MODEL), full(1, A_DV), full(1, B_DV), full(1, D_MODEL),
            full(N_EXPERTS, D_MODEL), full(N_EXPERTS, D_MODEL),
        ],
        out_specs=[row(D_MODEL), row(D_MODEL), pl.BlockSpec((N_EXPERTS, tm), lambda i: (0, i))],
        out_shape=[
            jax.ShapeDtypeStruct((t, D_MODEL), F32),
            jax.ShapeDtypeStruct((t, D_MODEL), BF16),
            jax.ShapeDtypeStruct((N_EXPERTS, t), F32),
        ],
        scratch_shapes=[pltpu.VMEM((tm, D_MODEL), BF16)],
        compiler_params=_cparams(("arbitrary",)),
        name="out_proj",
    )(oaf, oab, obf, obb, main, main, x2d, wo, na, nb, ln2, wrh, wrl)


TOPK_LANES = 256


def _topk_kernel(p_ref, slot_ref, ts_ref, *, cap):
    n = p_ref.shape[1]
    n_steps = n // TOPK_LANES
    bits = pltpu.bitcast(p_ref[...], jnp.int32)

    def bisect(i, thr):
        cand = thr | jnp.left_shift(jnp.int32(1), 30 - i)
        cnt = jnp.sum(jnp.where(bits >= cand, 1.0, 0.0), axis=1, keepdims=True)
        return jnp.where(cnt >= cap, cand, thr)

    thr = lax.fori_loop(0, 31, bisect, jnp.zeros((N_EXPERTS, 1), jnp.int32))
    need = cap - jnp.sum(jnp.where(bits > thr, 1.0, 0.0), axis=1, keepdims=True)
    r = lax.broadcasted_iota(jnp.int32, (TOPK_LANES, TOPK_LANES), 0)
    c = lax.broadcasted_iota(jnp.int32, (TOPK_LANES, TOPK_LANES), 1)
    tri = jnp.where(r <= c, 1.0, 0.0).astype(BF16)
    step_lane = lax.broadcasted_iota(jnp.int32, (N_EXPERTS, n_steps), 1)

    ts_ref[...] = jnp.zeros_like(ts_ref)

    def step(j, carry):
        ties_before, sel_before = carry
        lanes = pl.ds(pl.multiple_of(j * TOPK_LANES, TOPK_LANES), TOPK_LANES)
        pb = pltpu.bitcast(p_ref[:, lanes], jnp.int32)
        tie = jnp.where(pb == thr, 1.0, 0.0)
        tie_incl = jnp.dot(tie.astype(BF16), tri, preferred_element_type=F32)
        sel = jnp.logical_or(pb > thr, jnp.logical_and(pb == thr, ties_before + tie_incl - tie < need))
        self_f = jnp.where(sel, 1.0, 0.0)
        sel_incl = jnp.dot(self_f.astype(BF16), tri, preferred_element_type=F32)
        pos = sel_before + sel_incl - self_f
        slot_ref[:, lanes] = jnp.where(sel, pos, -1.0).astype(jnp.int32)
        ts_ref[...] = jnp.where(step_lane == j, sel_before.astype(jnp.int32), ts_ref[...])
        return (ties_before + tie_incl[:, TOPK_LANES - 1:],
                sel_before + sel_incl[:, TOPK_LANES - 1:])

    zero = jnp.zeros((N_EXPERTS, 1), F32)
    lax.fori_loop(0, n_steps, step, (zero, zero))


def _topk(probs_t, cap):
    n = probs_t.shape[1]
    return pl.pallas_call(
        functools.partial(_topk_kernel, cap=cap),
        out_shape=[
            jax.ShapeDtypeStruct((N_EXPERTS, n), jnp.int32),
            jax.ShapeDtypeStruct((N_EXPERTS, n // TOPK_LANES), jnp.int32),
        ],
        compiler_params=pltpu.CompilerParams(vmem_limit_bytes=V7X_VMEM_LIMIT),
        name="topk",
    )(probs_t)


MOE_TILE = 1024
MOE_SUB = TOPK_LANES
MOE_SB = 512
MOE_RB = 128
CMB_TILE = 512
CMB_YB = 256
FL_VALID, FL_FIRST, FL_LAST = 1, 2, 4


def _moe_kernel(it_e, it_tile, it_g, it_s0, it_fl, it_ts, x_ref, slot_ref, p_ref,
                w1_ref, w3_ref, w2_ref, y_ref, xs_ref, g_ref):
    i = pl.program_id(0)
    fl = it_fl[i]
    e = it_e[i]
    s0 = it_s0[i]

    @pl.when((fl & FL_FIRST) != 0)
    def _():
        xs_ref[...] = jnp.zeros_like(xs_ref)
        g_ref[...] = jnp.zeros_like(g_ref)

    @pl.when((fl & FL_VALID) != 0)
    def _():
        srow = slot_ref[pl.ds(e, 1), :]
        prow = p_ref[pl.ds(e, 1), :]
        iota_s = lax.broadcasted_iota(jnp.int32, (MOE_RB, MOE_SUB), 0)
        for q in range(MOE_TILE // MOE_SUB):
            lo = jnp.maximum(it_ts[i * 5 + q], s0)
            hi = jnp.minimum(it_ts[i * 5 + q + 1], s0 + MOE_SB)

            @pl.when(hi > lo)
            def _():
                sq = srow[:, q * MOE_SUB:(q + 1) * MOE_SUB]
                pq = prow[:, q * MOE_SUB:(q + 1) * MOE_SUB]
                xq = x_ref[q * MOE_SUB:(q + 1) * MOE_SUB, :]

                def rows(r, carry):
                    roff = pl.multiple_of(r * MOE_RB, MOE_RB)
                    oh = (iota_s + (s0 + roff)) == sq
                    xs_ref[pl.ds(roff, MOE_RB), :] += jnp.dot(
                        jnp.where(oh, 1.0, 0.0).astype(BF16), xq, preferred_element_type=F32)
                    g_ref[pl.ds(roff, MOE_RB), :] += jnp.sum(jnp.where(oh, pq, 0.0), axis=1,
                                                            keepdims=True)
                    return carry

                lax.fori_loop((lo - s0) // MOE_RB, (hi - 1 - s0) // MOE_RB + 1, rows, 0)

    @pl.when((fl & FL_LAST) != 0)
    def _():
        xs = xs_ref[...].astype(BF16)
        h1 = jnp.dot(xs, w1_ref[0], preferred_element_type=F32)
        h3 = jnp.dot(xs, w3_ref[0], preferred_element_type=F32)
        hid = (_silu(h1) * h3).astype(BF16)
        y = jnp.dot(hid, w2_ref[0], preferred_element_type=F32) * g_ref[...]
        y_ref[...] = y.astype(y_ref.dtype)


def _moe_items(ts, cap):
    n_grp = ts.shape[1]
    per = MOE_TILE // MOE_SUB
    n_tile = n_grp // per
    n_sb = cap // MOE_SB
    tse = jnp.concatenate([ts, jnp.full((N_EXPERTS, 1), cap, jnp.int32)], axis=1)
    a = tse[:, 0:n_grp:per]
    b = tse[:, per::per]
    kf = a // MOE_SB
    cnt = jnp.where(b > a, (b - 1) // MOE_SB - kf + 1, 0).reshape(-1)
    incl = jnp.cumsum(cnt)
    total = incl[-1]
    n_items = N_EXPERTS * (n_tile + n_sb)
    idx = jnp.arange(n_items, dtype=jnp.int32)
    valid = idx < total
    idc = jnp.minimum(idx, total - 1)
    pair = jnp.searchsorted(incl, idc, side="right").astype(jnp.int32)
    r = idc - (incl[pair] - cnt[pair])
    e = pair // n_tile
    j = pair % n_tile
    k = kf.reshape(-1)[pair] + r
    s0 = k * MOE_SB
    aa = a.reshape(-1)[pair]
    bb = b.reshape(-1)[pair]
    first = jnp.logical_and(aa <= s0, s0 < bb)
    last = jnp.logical_and(aa <= s0 + MOE_SB - 1, s0 + MOE_SB - 1 < bb)
    fl = jnp.where(valid, FL_VALID + FL_FIRST * first + FL_LAST * last, 0).astype(jnp.int32)
    tsi = jnp.stack([tse[e, j * per + q] for q in range(per + 1)], axis=1).reshape(-1)
    return (e.astype(jnp.int32), j.astype(jnp.int32), (e * n_sb + k).astype(jnp.int32),
            s0.astype(jnp.int32), fl, tsi.astype(jnp.int32))


def _moe(hn, slot, probs_t, ts, w1, w3, w2, cap):
    n = hn.shape[0]
    items = _moe_items(ts, cap)
    n_items = items[0].shape[0]
    grid_spec = pltpu.PrefetchScalarGridSpec(
        num_scalar_prefetch=6,
        grid=(n_items,),
        in_specs=[
            pl.BlockSpec((MOE_TILE, D_MODEL), lambda i, e, t, g, s, f, ts: (t[i], 0)),
            pl.BlockSpec((N_EXPERTS, MOE_TILE), lambda i, e, t, g, s, f, ts: (0, t[i])),
            pl.BlockSpec((N_EXPERTS, MOE_TILE), lambda i, e, t, g, s, f, ts: (0, t[i])),
            pl.BlockSpec((1, D_MODEL, EXPERT_FF), lambda i, e, t, g, s, f, ts: (e[i], 0, 0)),
            pl.BlockSpec((1, D_MODEL, EXPERT_FF), lambda i, e, t, g, s, f, ts: (e[i], 0, 0)),
            pl.BlockSpec((1, EXPERT_FF, D_MODEL), lambda i, e, t, g, s, f, ts: (e[i], 0, 0)),
        ],
        out_specs=pl.BlockSpec((MOE_SB, D_MODEL), lambda i, e, t, g, s, f, ts: (g[i], 0)),
        scratch_shapes=[pltpu.VMEM((MOE_SB, D_MODEL), F32), pltpu.VMEM((MOE_SB, 1), F32)],
    )
    return pl.pallas_call(
        _moe_kernel,
        grid_spec=grid_spec,
        out_shape=jax.ShapeDtypeStruct((N_EXPERTS * cap, D_MODEL), BF16),
        compiler_params=_cparams(("arbitrary",)),
        name="moe_dispatch_mlp",
    )(*items, hn, slot, probs_t, w1, w3, w2)


def _combine_kernel(it_e, it_tile, it_yb, it_rel, it_fl, slot_ref, y_ref, h_ref, lnf_ref,
                    out_ref, acc_ref):
    i = pl.program_id(0)
    fl = it_fl[i]

    @pl.when((fl & FL_FIRST) != 0)
    def _():
        acc_ref[...] = jnp.zeros_like(acc_ref)

    @pl.when((fl & FL_VALID) != 0)
    def _():
        srow = slot_ref[pl.ds(it_e[i], 1), :]
        iota_s = lax.broadcasted_iota(jnp.int32, (CMB_YB, CMB_TILE), 0)
        oh = jnp.where((iota_s + it_rel[i]) == srow, 1.0, 0.0).astype(BF16)
        acc_ref[...] += _dot_tn(oh, y_ref[...])

    @pl.when((fl & FL_LAST) != 0)
    def _():
        hh = h_ref[...] + acc_ref[...]
        out_ref[...] = hh * lax.rsqrt(jnp.mean(hh * hh, axis=-1, keepdims=True) + EPS) * lnf_ref[...]


def _combine_items(ts, cap):
    n_grp = ts.shape[1]
    per = CMB_TILE // TOPK_LANES
    n_tile = n_grp // per
    tse = jnp.concatenate([ts, jnp.full((N_EXPERTS, 1), cap, jnp.int32)], axis=1)
    a = tse[:, 0:n_grp:per].T.reshape(-1)
    b = tse[:, per::per].T.reshape(-1)
    kf = jnp.minimum(a, cap - 1) // CMB_YB
    cnt = jnp.where(b > a, (b - 1) // CMB_YB - kf + 1, 1)
    incl = jnp.cumsum(cnt)
    total = incl[-1]
    n_items = n_tile * N_EXPERTS + N_EXPERTS * cap // CMB_YB
    idx = jnp.arange(n_items, dtype=jnp.int32)
    valid = idx < total
    idc = jnp.minimum(idx, total - 1)
    pair = jnp.searchsorted(incl, idc, side="right").astype(jnp.int32)
    r = idc - (incl[pair] - cnt[pair])
    tile = pair // N_EXPERTS
    e = pair % N_EXPERTS
    k = kf[pair] + r
    first = jnp.logical_and(e == 0, r == 0)
    last = jnp.logical_and(e == N_EXPERTS - 1, r == cnt[pair] - 1)
    fl = jnp.where(valid, FL_VALID + FL_FIRST * first + FL_LAST * last, 0).astype(jnp.int32)
    yb = e * (cap // CMB_YB) + k
    return (e.astype(jnp.int32), tile.astype(jnp.int32), yb.astype(jnp.int32),
            (k * CMB_YB).astype(jnp.int32), fl)


def _combine(y, slot, ts, h, ln_f, cap):
    n = h.shape[0]
    items = _combine_items(ts, cap)
    n_items = items[0].shape[0]
    grid_spec = pltpu.PrefetchScalarGridSpec(
        num_scalar_prefetch=5,
        grid=(n_items,),
        in_specs=[
            pl.BlockSpec((N_EXPERTS, CMB_TILE), lambda i, e, t, yb, rel, f: (0, t[i])),
            pl.BlockSpec((CMB_YB, D_MODEL), lambda i, e, t, yb, rel, f: (yb[i], 0)),
            pl.BlockSpec((CMB_TILE, D_MODEL), lambda i, e, t, yb, rel, f: (t[i], 0)),
            pl.BlockSpec((1, D_MODEL), lambda i, e, t, yb, rel, f: (0, 0)),
        ],
        out_specs=pl.BlockSpec((CMB_TILE, D_MODEL), lambda i, e, t, yb, rel, f: (t[i], 0)),
        scratch_shapes=[pltpu.VMEM((CMB_TILE, D_MODEL), F32)],
    )
    return pl.pallas_call(
        _combine_kernel,
        grid_spec=grid_spec,
        out_shape=jax.ShapeDtypeStruct((n, D_MODEL), F32),
        compiler_params=_cparams(("arbitrary",)),
        name="moe_combine",
    )(*items, slot, y, h, ln_f)


def _prep_in_weights(w_in, a_log_f, a_log_b, dt_bias_f, dt_bias_b):
    offs = np.cumsum([0, A_QK, A_QK, A_V, A_V, A_HEADS, A_HEADS, A_HEADS, A_HEADS,
                      B_QK, B_QK, B_V, B_V, GLA_LOWRANK, GLA_LOWRANK])
    seg = [w_in[:, offs[i]:offs[i + 1]] for i in range(14)]
    w_main = jnp.concatenate(seg[0:4] + seg[8:12], axis=1).astype(BF16)
    w_small = jnp.concatenate(seg[4:8] + seg[12:14], axis=1)
    w_small = jnp.pad(w_small, ((0, 0), (0, SMALL_WIDTH - w_small.shape[1]))).astype(BF16)
    pad = SMALL_WIDTH - 4 * A_HEADS
    z = jnp.zeros((2 * A_HEADS,), F32)
    a_row = jnp.concatenate([z, a_log_f.astype(F32), a_log_b.astype(F32), jnp.zeros((pad,), F32)])
    dt_row = jnp.concatenate([z, dt_bias_f.astype(F32), dt_bias_b.astype(F32), jnp.zeros((pad,), F32)])
    small_params = jnp.zeros((8, SMALL_WIDTH), F32).at[0].set(a_row).at[1].set(dt_row)
    return w_main, w_small, small_params


def kernel(x_prompt, x_sample, ln1, w_in, conv_w, a_log_f, a_log_b, dt_bias_f, dt_bias_b, norm_a, gla_w_f, gla_b_f, gla_w_b, gla_b_b, norm_b, w_out, ln2, w_router, w1, w3, w2, ln_f):
    w_main, w_small, small_params = _prep_in_weights(w_in[0], a_log_f[0], a_log_b[0],
                                                     dt_bias_f[0], dt_bias_b[0])
    conv8 = _conv_layout(conv_w[0])
    wg, bg = _gla_gate_weights(gla_w_f[0], gla_b_f[0], gla_w_b[0], gla_b_b[0])
    wl = jnp.asarray(_gla_level_weights(), BF16)
    wo = w_out[0].astype(BF16)
    wr_t = w_router[0].astype(F32).T
    wrh = wr_t.astype(BF16)
    wrl = (wr_t - wrh.astype(F32)).astype(BF16)
    w1b, w3b, w2b = w1[0].astype(BF16), w3[0].astype(BF16), w2[0].astype(BF16)
    row = lambda v: v.astype(F32).reshape(1, -1)
    outs = []
    for x in (x_prompt, x_sample):
        bsz, seq, _ = x.shape
        n = bsz * seq
        cap = EC_CAPACITY * n // N_EXPERTS
        x2d = x.reshape(n, D_MODEL)
        main, small = _in_proj(x2d, row(ln1[0]), w_main, w_small, small_params, tm=1024, tn=1024)
        main3 = main.reshape(bsz, seq, MAIN_WIDTH)
        small3 = small.reshape(bsz, seq, SMALL_WIDTH)
        oaf, oab = _gdn(main3, conv8, _scal_layout(small3))
        obf, obb = _gla(main3, small3, wg, bg, wl)
        flat = lambda o: o.reshape(n, -1)
        h, hn, probs_t = _out_proj(flat(oaf), flat(oab), flat(obf), flat(obb), main, x2d, wo,
                                   row(norm_a[0]), row(norm_b[0]), row(ln2[0]), wrh, wrl, tm=256)
        slot, ts = _topk(probs_t, cap)
        y = _moe(hn, slot, probs_t, ts, w1b, w3b, w2b, cap)
        out = _combine(y, slot, ts, h, row(ln_f), cap)
        outs.append(out.reshape(bsz, seq, D_MODEL))
    return tuple(outs)
```

```python
import functools

import jax
import jax.numpy as jnp
import numpy as np
from jax import lax
from jax.experimental import pallas as pl
from jax.experimental.pallas import tpu as pltpu

F32 = jnp.float32
BF16 = jnp.bfloat16

D_MODEL = 2048
A_HEADS, A_DK, A_DV = 8, 128, 128
B_HEADS, B_DK, B_DV = 4, 128, 256
GLA_LOWRANK = 16
GLA_GATE_NORM = 16.0
CONV_K = 5
CHUNK = 64
N_EXPERTS = 16
EC_CAPACITY = 2
EXPERT_FF = D_MODEL // 2
EPS = 1e-6

A_QK = A_HEADS * A_DK
A_V = A_HEADS * A_DV
B_QK = B_HEADS * B_DK
B_V = B_HEADS * B_DV
MAIN_WIDTH = 2 * A_QK + 2 * A_V + 2 * B_QK + 2 * B_V
SMALL_WIDTH = 128
LR_OFF = 4 * A_HEADS
V7X_VMEM_LIMIT = 56 * 1024 * 1024
CONV_HALO = 16
SCAN_ROWS = 256
CPS = SCAN_ROWS // CHUNK
GLA_LEVELS = 6
GDN_HEADS_PER_PASS = 4


def _cparams(sem):
    return pltpu.CompilerParams(dimension_semantics=sem, vmem_limit_bytes=V7X_VMEM_LIMIT)


def _dot(a, b):
    return jnp.dot(a.astype(BF16), b.astype(BF16), preferred_element_type=F32)


def _dot_nt(a, b):
    return lax.dot_general(a.astype(BF16), b.astype(BF16), (((1,), (1,)), ((), ())),
                           preferred_element_type=F32)


def _dot_tn(a, b):
    return lax.dot_general(a.astype(BF16), b.astype(BF16), (((0,), (0,)), ((), ())),
                           preferred_element_type=F32)


def _silu(x):
    return x * (1.0 / (1.0 + jnp.exp(-x)))


def _in_proj_kernel(x_ref, ln_ref, wm_ref, ws_ref, sp_ref, main_ref, small_ref, hn_ref):
    j = pl.program_id(1)

    @pl.when(j == 0)
    def _():
        xf = x_ref[...]
        y = xf * lax.rsqrt(jnp.mean(xf * xf, axis=-1, keepdims=True) + EPS) * ln_ref[...]
        hn = y.astype(BF16)
        hn_ref[...] = hn
        s = jnp.dot(hn, ws_ref[...], preferred_element_type=F32)
        lane = lax.broadcasted_iota(jnp.int32, s.shape, 1)
        neg_a = -jnp.exp(sp_ref[0:1, :])
        z = s + sp_ref[1:2, :]
        softplus = jnp.maximum(z, 0.0) + jnp.log(1.0 + jnp.exp(-jnp.abs(z)))
        sig = 1.0 / (1.0 + jnp.exp(-s))
        small_ref[...] = jnp.where(lane < 2 * A_HEADS, sig,
                                   jnp.where(lane < 4 * A_HEADS, neg_a * softplus, s))

    main_ref[...] = jnp.dot(hn_ref[...], wm_ref[...], preferred_element_type=F32).astype(BF16)


def _in_proj(x2d, ln1, w_main, w_small, small_params, tm, tn):
    t = x2d.shape[0]
    return pl.pallas_call(
        _in_proj_kernel,
        grid=(t // tm, MAIN_WIDTH // tn),
        in_specs=[
            pl.BlockSpec((tm, D_MODEL), lambda i, j: (i, 0)),
            pl.BlockSpec((1, D_MODEL), lambda i, j: (0, 0)),
            pl.BlockSpec((D_MODEL, tn), lambda i, j: (0, j)),
            pl.BlockSpec((D_MODEL, SMALL_WIDTH), lambda i, j: (0, 0)),
            pl.BlockSpec((8, SMALL_WIDTH), lambda i, j: (0, 0)),
        ],
        out_specs=[
            pl.BlockSpec((tm, tn), lambda i, j: (i, j)),
            pl.BlockSpec((tm, SMALL_WIDTH), lambda i, j: (i, 0)),
        ],
        out_shape=[
            jax.ShapeDtypeStruct((t, MAIN_WIDTH), BF16),
            jax.ShapeDtypeStruct((t, SMALL_WIDTH), F32),
        ],
        scratch_shapes=[pltpu.VMEM((tm, D_MODEL), BF16)],
        compiler_params=_cparams(("arbitrary", "arbitrary")),
        name="in_proj",
    )(x2d, ln1, w_main, w_small, small_params)


def _chunk_iotas():
    ii = lax.broadcasted_iota(jnp.int32, (CHUNK, CHUNK), 0)
    jj = lax.broadcasted_iota(jnp.int32, (CHUNK, CHUNK), 1)
    return ii, jj


def _row_to_col(row, eye):
    return jnp.sum(jnp.where(eye, row, 0.0), axis=1, keepdims=True)


def _col_to_row(col, eye):
    return jnp.sum(jnp.where(eye, col, 0.0), axis=0, keepdims=True)


def _l2norm(x):
    return x * lax.rsqrt(jnp.sum(x * x, axis=-1, keepdims=True) + EPS)


def _delta_chunks(qs, ks, vs, beta_rows, g_rows, states, revs):
    ii, jj = _chunk_iotas()
    eye = ii == jj
    nc = range(len(qs))
    incl = [(ii <= jj) if r else (ii >= jj) for r in revs]
    strict = [(ii < jj) if r else (ii > jj) for r in revs]
    gc_col = [jnp.sum(jnp.where(incl[c], g_rows[c], 0.0), axis=1, keepdims=True) for c in nc]
    gc_row = [_col_to_row(gc_col[c], eye) for c in nc]
    beta_col = [_row_to_col(beta_rows[c], eye) for c in nc]
    g_tot = [jnp.sum(g_rows[c], axis=1, keepdims=True) for c in nc]
    decay = [jnp.where(incl[c], jnp.exp(jnp.where(incl[c], gc_col[c] - gc_row[c], 0.0)), 0.0)
             for c in nc]
    kb = [ks[c] * beta_col[c] for c in nc]
    kk = [_dot_nt(kb[c], ks[c]) for c in nc]
    qk = [_dot_nt(qs[c], ks[c]) for c in nc]
    p = [jnp.where(strict[c], -kk[c] * decay[c], 0.0) for c in nc]
    toff = p
    for _ in range(5):
        p = [_dot(p[c], p[c]) for c in nc]
        tp = [_dot(toff[c], p[c]) for c in nc]
        toff = [toff[c] + p[c] + tp[c] for c in nc]
    e_gc = [jnp.exp(gc_col[c]) for c in nc]
    rhs = [jnp.concatenate([vs[c] * beta_col[c], kb[c] * e_gc[c]], axis=1) for c in nc]
    sol = [rhs[c] + _dot(toff[c], rhs[c]) for c in nc]
    attn = [jnp.where(incl[c], qk[c] * decay[c], 0.0) for c in nc]
    ws = [_dot(sol[c][:, A_DV:], states[c]) for c in nc]
    qs_state = [_dot(qs[c] * e_gc[c], states[c]) for c in nc]
    v_new = [sol[c][:, :A_DV] - ws[c] for c in nc]
    av = [_dot(attn[c], v_new[c]) for c in nc]
    kv = [_dot_tn(ks[c] * jnp.exp(g_tot[c] - gc_col[c]), v_new[c]) for c in nc]
    outs = [qs_state[c] + av[c] for c in nc]
    new_states = [states[c] * jnp.exp(g_tot[c]) + kv[c] for c in nc]
    return outs, new_states


def _qkv_conv_kernel(blk, prv, nxt, cw_ref, out_ref, pad_ref):
    n = pl.program_id(1)
    nb = pl.num_programs(1)
    rows = blk.shape[1]
    pad_ref[0:CONV_HALO, :] = jnp.where(n == 0, 0.0, prv[0].astype(F32))
    pad_ref[CONV_HALO:CONV_HALO + rows, :] = blk[0].astype(F32)
    pad_ref[CONV_HALO + rows:, :] = jnp.where(n == nb - 1, 0.0, nxt[0].astype(F32))
    lo = CONV_HALO - CONV_K // 2
    for g in range((2 * A_QK + A_V) // A_DK):
        cols = slice(g * A_DK, (g + 1) * A_DK)
        w = cw_ref[:, cols]
        acc = w[0:1, :] * pad_ref[lo:lo + rows, cols]
        for j in range(1, CONV_K):
            acc = acc + w[j:j + 1, :] * pad_ref[lo + j:lo + j + rows, cols]
        y = _silu(acc)
        if g < A_HEADS:
            y = _l2norm(y) * (A_DK ** -0.5)
        elif g < 2 * A_HEADS:
            y = _l2norm(y)
        out_ref[0, :, cols] = y.astype(out_ref.dtype)


def _qkv_conv(main3, conv8, rows):
    bsz, seq, _ = main3.shape
    width = 2 * A_QK + A_V
    hb = rows // CONV_HALO
    nhalo = seq // CONV_HALO
    return pl.pallas_call(
        _qkv_conv_kernel,
        grid=(bsz, seq // rows),
        in_specs=[
            pl.BlockSpec((1, rows, width), lambda b, n: (b, n, 0)),
            pl.BlockSpec((1, CONV_HALO, width), lambda b, n: (b, jnp.maximum(n * hb - 1, 0), 0)),
            pl.BlockSpec((1, CONV_HALO, width),
                         lambda b, n: (b, jnp.minimum((n + 1) * hb, nhalo - 1), 0)),
            pl.BlockSpec((8, width), lambda b, n: (0, 0)),
        ],
        out_specs=pl.BlockSpec((1, rows, width), lambda b, n: (b, n, 0)),
        out_shape=jax.ShapeDtypeStruct((bsz, seq, width), BF16),
        scratch_shapes=[pltpu.VMEM((rows + 2 * CONV_HALO, width), F32)],
        compiler_params=_cparams(("arbitrary", "arbitrary")),
        name="qkv_conv",
    )(main3, main3, main3, conv8)


def _gdn_kernel(blk_f, blk_b, sc_f, sc_b, of_ref, ob_ref, st_ref):
    n = pl.program_id(1)

    @pl.when(n == 0)
    def _():
        st_ref[...] = jnp.zeros_like(st_ref)

    def load(d, row0, a, h):
        blk = blk_f if d == 0 else blk_b
        col = a * A_QK + h * A_DK
        return blk[0, pl.ds(row0, CHUNK), col:col + A_DK].astype(F32)

    def body(s, carry):
        for h0 in range(0, A_HEADS, GDN_HEADS_PER_PASS):
            chains = []
            for d, sc_ref in enumerate((sc_f, sc_b)):
                sc = s if d == 0 else CPS - 1 - s
                row0 = pl.multiple_of(sc * CHUNK, CHUNK)
                for h in range(h0, h0 + GDN_HEADS_PER_PASS):
                    chains.append((d, h, sc, row0, sc_ref))
            qs = [load(d, row0, 0, h) for d, h, sc, row0, _ in chains]
            ks = [load(d, row0, 1, h) for d, h, sc, row0, _ in chains]
            vs = [load(d, row0, 2, h) for d, h, sc, row0, _ in chains]
            betas = [r[0, 0, d, h, pl.ds(sc, 1), :] for d, h, sc, row0, r in chains]
            gs = [r[0, 0, 2 + d, h, pl.ds(sc, 1), :] for d, h, sc, row0, r in chains]
            states = [st_ref[d, h] for d, h, sc, row0, _ in chains]
            outs, new_states = _delta_chunks(qs, ks, vs, betas, gs, states,
                                             [d == 1 for d, *_ in chains])
            for (d, h, sc, row0, _), o, st in zip(chains, outs, new_states):
                st_ref[d, h] = st
                out_ref = of_ref if d == 0 else ob_ref
                out_ref[0, pl.ds(row0, CHUNK), h * A_DV:(h + 1) * A_DV] = o.astype(out_ref.dtype)
        return carry

    lax.fori_loop(0, CPS, body, 0)


def _scal_layout(small):
    bsz, seq = small.shape[:2]
    s = small[..., :4 * A_HEADS].reshape(bsz, seq // SCAN_ROWS, CPS, CHUNK, 4, A_HEADS)
    return jnp.transpose(s, (0, 1, 4, 5, 2, 3))


def _conv_layout(conv_w):
    return jnp.pad(conv_w.astype(F32), ((0, 8 - CONV_K), (0, 0)))


def _gdn(qkv, scal):
    bsz, seq, width = qkv.shape
    nb = seq // SCAN_ROWS

    def specs(blk_of):
        return (pl.BlockSpec((1, SCAN_ROWS, width), lambda b, n: (b, blk_of(n), 0)),
                pl.BlockSpec((1, 1, 4, A_HEADS, CPS, CHUNK), lambda b, n: (b, blk_of(n), 0, 0, 0, 0)))

    qf, sf = specs(lambda n: n)
    qb, sb = specs(lambda n: nb - 1 - n)
    out_sds = jax.ShapeDtypeStruct((bsz, seq, A_V), BF16)
    return pl.pallas_call(
        _gdn_kernel,
        grid=(bsz, nb),
        in_specs=[qf, qb, sf, sb],
        out_specs=[
            pl.BlockSpec((1, SCAN_ROWS, A_V), lambda b, n: (b, n, 0)),
            pl.BlockSpec((1, SCAN_ROWS, A_V), lambda b, n: (b, nb - 1 - n, 0)),
        ],
        out_shape=[out_sds, out_sds],
        scratch_shapes=[pltpu.VMEM((2, A_HEADS, A_DK, A_DV), F32)],
        compiler_params=_cparams(("arbitrary", "arbitrary")),
        name="gdn",
    )(qkv, qkv, scal, scal)


def _gla_level_weights():
    c = CHUNK
    w = np.zeros((2, 8 * c, c), np.float32)
    for d in range(2):
        for l in range(GLA_LEVELS):
            s = c >> (l + 1)
            for i in range(c):
                mid = (i // (2 * s)) * 2 * s + s
                if d == 0:
                    ts = range(mid, i + 1) if i >= mid else range(i + 1, mid)
                else:
                    ts = range(i, mid) if i < mid else range(mid, i)
                for t in ts:
                    w[d, l * c + i, t] = 1.0
        for i in range(c):
            for t in range(c):
                before = t <= i if d == 0 else t >= i
                w[d, 6 * c + i, t] = 1.0 if before else 0.0
                w[d, 7 * c + i, t] = 0.0 if before else 1.0
    return w


def _gla_kernel(qk_f, v_f, sm_f, qk_b, v_b, sm_b, wg_ref, bg_ref, wl_ref, of_ref, ob_ref, st_ref):
    n = pl.program_id(1)

    @pl.when(n == 0)
    def _():
        st_ref[...] = jnp.zeros_like(st_ref)

    ii, jj = _chunk_iotas()
    eye = ii == jj
    tok = lax.broadcasted_iota(jnp.int32, (CHUNK, 1), 0)

    streams = ((qk_f, v_f, sm_f, of_ref), (qk_b, v_b, sm_b, ob_ref))
    dirs = range(2)
    chains = [(d, h) for d in dirs for h in range(B_HEADS)]
    masks = [ii // (CHUNK >> l) == jj // (CHUNK >> l) for l in range(GLA_LEVELS)]

    def body(s, carry):
        rows = [pl.ds(pl.multiple_of((s if d == 0 else CPS - 1 - s) * CHUNK, CHUNK), CHUNK)
                for d in dirs]
        z = [_dot(streams[d][2][0, rows[d], :], wg_ref[d]) + bg_ref[d] for d in dirs]
        la = [(jnp.minimum(z[d], 0.0) - jnp.log(1.0 + jnp.exp(-jnp.abs(z[d]))))
              * (1.0 / GLA_GATE_NORM) for d in dirs]
        hi = [la[d].astype(BF16) for d in dirs]
        r1 = [la[d] - hi[d].astype(F32) for d in dirs]
        mid = [r1[d].astype(BF16) for d in dirs]
        low = [(r1[d] - mid[d].astype(F32)).astype(BF16) for d in dirs]
        ex = [jnp.dot(wl_ref[d], hi[d], preferred_element_type=F32)
              + jnp.dot(wl_ref[d], mid[d], preferred_element_type=F32)
              + jnp.dot(wl_ref[d], low[d], preferred_element_type=F32) for d in dirs]
        e_all = [jnp.exp(ex[d]) for d in dirs]
        q_all = [streams[d][0][0, rows[d], :B_QK].astype(F32) * (B_DK ** -0.5) for d in dirs]
        k_all = [streams[d][0][0, rows[d], B_QK:].astype(F32) for d in dirs]
        qs, ks = [], []
        for l in range(GLA_LEVELS):
            half = CHUNK >> (l + 1)
            right = (tok // half) % 2 == 1
            q_side = [right, jnp.logical_not(right)]
            e_l = [e_all[d][l * CHUNK:(l + 1) * CHUNK] for d in dirs]
            qs.append([jnp.where(q_side[d], q_all[d] * e_l[d], 0.0).astype(BF16) for d in dirs])
            ks.append([jnp.where(q_side[d], 0.0, k_all[d] * e_l[d]).astype(BF16) for d in dirs])
        e_b = [e_all[d][6 * CHUNK:7 * CHUNK] for d in dirs]
        q_dec = [(q_all[d] * e_b[d]).astype(BF16) for d in dirs]
        k_dec = [(k_all[d] * e_all[d][7 * CHUNK:]).astype(BF16) for d in dirs]
        e_tot = [e_b[0][CHUNK - 1:CHUNK], e_b[1][0:1]]
        ck = [slice(h * B_DK, (h + 1) * B_DK) for h in range(B_HEADS)]
        cv = [slice(h * B_DV, (h + 1) * B_DV) for h in range(B_HEADS)]
        attn = [jnp.where(eye, _dot_nt(q_all[d][:, ck[h]], k_all[d][:, ck[h]]), 0.0)
                for d, h in chains]
        for l in range(GLA_LEVELS):
            part = [_dot_nt(qs[l][d][:, ck[h]], ks[l][d][:, ck[h]]) for d, h in chains]
            attn = [attn[c] + jnp.where(masks[l], part[c], 0.0) for c in range(len(chains))]
        v = [streams[d][1][0, rows[d], cv[h]] for d, h in chains]
        st = [st_ref[d, h] for d, h in chains]
        o_state = [_dot_nt(q_dec[d][:, ck[h]], st[c]) for c, (d, h) in enumerate(chains)]
        o_local = [_dot(attn[c], v[c]) for c in range(len(chains))]
        kv = [_dot_tn(v[c], k_dec[d][:, ck[h]]) for c, (d, h) in enumerate(chains)]
        for c, (d, h) in enumerate(chains):
            st_ref[d, h] = st[c] * e_tot[d][:, ck[h]] + kv[c]
            streams[d][3][0, rows[d], cv[h]] = (o_state[c] + o_local[c]).astype(of_ref.dtype)
        return carry

    lax.fori_loop(0, CPS, body, 0)


def _gla(main3, small3, wg, bg, wl):
    bsz, seq, _ = main3.shape
    nb = seq // SCAN_ROWS
    qk_blk = (2 * A_QK + 2 * A_V) // (2 * B_QK)
    v_blk = (2 * A_QK + 2 * A_V + 2 * B_QK) // B_V

    def stream(blk_of):
        return [
            pl.BlockSpec((1, SCAN_ROWS, 2 * B_QK), lambda b, n: (b, blk_of(n), qk_blk)),
            pl.BlockSpec((1, SCAN_ROWS, B_V), lambda b, n: (b, blk_of(n), v_blk)),
            pl.BlockSpec((1, SCAN_ROWS, SMALL_WIDTH), lambda b, n: (b, blk_of(n), 0)),
        ]

    out_sds = jax.ShapeDtypeStruct((bsz, seq, B_V), BF16)
    return pl.pallas_call(
        _gla_kernel,
        grid=(bsz, nb),
        in_specs=stream(lambda n: n) + stream(lambda n: nb - 1 - n) + [
            pl.BlockSpec((2, SMALL_WIDTH, B_QK), lambda b, n: (0, 0, 0)),
            pl.BlockSpec((2, 1, B_QK), lambda b, n: (0, 0, 0)),
            pl.BlockSpec((2, 8 * CHUNK, CHUNK), lambda b, n: (0, 0, 0)),
        ],
        out_specs=[
            pl.BlockSpec((1, SCAN_ROWS, B_V), lambda b, n: (b, n, 0)),
            pl.BlockSpec((1, SCAN_ROWS, B_V), lambda b, n: (b, nb - 1 - n, 0)),
        ],
        out_shape=[out_sds, out_sds],
        scratch_shapes=[pltpu.VMEM((2, B_HEADS, B_DV, B_DK), F32)],
        compiler_params=_cparams(("arbitrary", "arbitrary")),
        name="gla",
    )(main3, main3, small3, main3, main3, small3, wg, bg, wl)


def _gla_gate_weights(gla_w_f, gla_b_f, gla_w_b, gla_b_b):
    wg = jnp.zeros((2, SMALL_WIDTH, B_QK), F32)
    wg = wg.at[0, LR_OFF:LR_OFF + GLA_LOWRANK].set(gla_w_f.astype(F32))
    wg = wg.at[1, LR_OFF + GLA_LOWRANK:LR_OFF + 2 * GLA_LOWRANK].set(gla_w_b.astype(F32))
    bg = jnp.stack([gla_b_f, gla_b_b]).astype(F32).reshape(2, 1, B_QK)
    return wg.astype(BF16), bg


def _out_proj_kernel(oaf, oab, obf, obb, ga, gb, x_ref, wo_ref, na_ref, nb_ref, ln2_ref,
                     wrh_ref, wrl_ref, h_ref, hn_ref, pt_ref, mix_ref):
    def head_norm(of_ref, ob_ref, g_ref, w_ref, width, heads, base):
        for h in range(heads):
            c = slice(h * width, (h + 1) * width)
            o = of_ref[:, c].astype(F32) + ob_ref[:, c].astype(F32)
            y = o * lax.rsqrt(jnp.mean(o * o, axis=-1, keepdims=True) + EPS) * w_ref[...]
            mix_ref[:, base + h * width:base + (h + 1) * width] = (
                y * _silu(g_ref[:, c].astype(F32))).astype(BF16)

    head_norm(oaf, oab, ga, na_ref, A_DV, A_HEADS, 0)
    head_norm(obf, obb, gb, nb_ref, B_DV, B_HEADS, A_V)
    hres = x_ref[...] + jnp.dot(mix_ref[...], wo_ref[...], preferred_element_type=F32)
    h_ref[...] = hres
    hn = hres * lax.rsqrt(jnp.mean(hres * hres, axis=-1, keepdims=True) + EPS) * ln2_ref[...]
    hi = hn.astype(BF16)
    hn_ref[...] = hi
    lo = (hn - hi.astype(F32)).astype(BF16)
    lt = _dot_nt(wrh_ref[...], hi) + _dot_nt(wrl_ref[...], hi) + _dot_nt(wrh_ref[...], lo)
    e = jnp.exp(lt - jnp.max(lt, axis=0, keepdims=True))
    pt_ref[...] = e / jnp.sum(e, axis=0, keepdims=True)


def _out_proj(oaf, oab, obf, obb, main, x2d, wo, na, nb, ln2, wrh, wrl, tm):
    t = x2d.shape[0]
    ga_blk = (2 * A_QK + A_V) // A_V
    gb_blk = (2 * A_QK + 2 * A_V + 2 * B_QK + B_V) // B_V
    row = lambda w: pl.BlockSpec((tm, w), lambda i: (i, 0))
    full = lambda a, b: pl.BlockSpec((a, b), lambda i: (0, 0))
    return pl.pallas_call(
        _out_proj_kernel,
        grid=(t // tm,),
        in_specs=[
            row(A_V), row(A_V), row(B_V), row(B_V),
            pl.BlockSpec((tm, A_V), lambda i: (i, ga_blk)),
            pl.BlockSpec((tm, B_V), lambda i: (i, gb_blk)),
            row(D_MODEL), full(D_MODEL, D_MODEL), full(1, A_DV), full(1, B_DV), full(1, D_MODEL),
            full(N_EXPERTS, D_MODEL), full(N_EXPERTS, D_MODEL),
        ],
        out_specs=[row(D_MODEL), row(D_MODEL), pl.BlockSpec((N_EXPERTS, tm), lambda i: (0, i))],
        out_shape=[
            jax.ShapeDtypeStruct((t, D_MODEL), F32),
            jax.ShapeDtypeStruct((t, D_MODEL), BF16),
            jax.ShapeDtypeStruct((N_EXPERTS, t), F32),
        ],
        scratch_shapes=[pltpu.VMEM((tm, D_MODEL), BF16)],
        compiler_params=_cparams(("arbitrary",)),
        name="out_proj",
    )(oaf, oab, obf, obb, main, main, x2d, wo, na, nb, ln2, wrh, wrl)


TOPK_LANES = 256


def _topk_kernel(p_ref, slot_ref, ts_ref, *, cap):
    n = p_ref.shape[1]
    n_steps = n // TOPK_LANES
    bits = pltpu.bitcast(p_ref[...], jnp.int32)

    def bisect(i, thr):
        cand = thr | jnp.left_shift(jnp.int32(1), 30 - i)
        cnt = jnp.sum(jnp.where(bits >= cand, 1.0, 0.0), axis=1, keepdims=True)
        return jnp.where(cnt >= cap, cand, thr)

    thr = lax.fori_loop(0, 31, bisect, jnp.zeros((N_EXPERTS, 1), jnp.int32))
    need = cap - jnp.sum(jnp.where(bits > thr, 1.0, 0.0), axis=1, keepdims=True)
    r = lax.broadcasted_iota(jnp.int32, (TOPK_LANES, TOPK_LANES), 0)
    c = lax.broadcasted_iota(jnp.int32, (TOPK_LANES, TOPK_LANES), 1)
    tri = jnp.where(r <= c, 1.0, 0.0).astype(BF16)
    step_lane = lax.broadcasted_iota(jnp.int32, (N_EXPERTS, n_steps), 1)

    ts_ref[...] = jnp.zeros_like(ts_ref)

    def step(j, carry):
        ties_before, sel_before = carry
        lanes = pl.ds(pl.multiple_of(j * TOPK_LANES, TOPK_LANES), TOPK_LANES)
        pb = pltpu.bitcast(p_ref[:, lanes], jnp.int32)
        tie = jnp.where(pb == thr, 1.0, 0.0)
        tie_incl = jnp.dot(tie.astype(BF16), tri, preferred_element_type=F32)
        sel = jnp.logical_or(pb > thr, jnp.logical_and(pb == thr, ties_before + tie_incl - tie < need))
        self_f = jnp.where(sel, 1.0, 0.0)
        sel_incl = jnp.dot(self_f.astype(BF16), tri, preferred_element_type=F32)
        pos = sel_before + sel_incl - self_f
        slot_ref[:, lanes] = jnp.where(sel, pos, -1.0).astype(jnp.int32)
        ts_ref[...] = jnp.where(step_lane == j, sel_before.astype(jnp.int32), ts_ref[...])
        return (ties_before + tie_incl[:, TOPK_LANES - 1:],
                sel_before + sel_incl[:, TOPK_LANES - 1:])

    zero = jnp.zeros((N_EXPERTS, 1), F32)
    lax.fori_loop(0, n_steps, step, (zero, zero))


def _topk(probs_t, cap):
    n = probs_t.shape[1]
    return pl.pallas_call(
        functools.partial(_topk_kernel, cap=cap),
        out_shape=[
            jax.ShapeDtypeStruct((N_EXPERTS, n), jnp.int32),
            jax.ShapeDtypeStruct((N_EXPERTS, n // TOPK_LANES), jnp.int32),
        ],
        compiler_params=pltpu.CompilerParams(vmem_limit_bytes=V7X_VMEM_LIMIT),
        name="topk",
    )(probs_t)


MOE_TILE = 1024
MOE_SUB = TOPK_LANES
MOE_SB = 512
MOE_RB = 128
CMB_TILE = 512
CMB_YB = 256
FL_VALID, FL_FIRST, FL_LAST = 1, 2, 4


def _moe_kernel(it_e, it_tile, it_g, it_s0, it_fl, it_ts, x_ref, slot_ref, p_ref,
                w1_ref, w3_ref, w2_ref, y_ref, xs_ref, g_ref):
    i = pl.program_id(0)
    fl = it_fl[i]
    e = it_e[i]
    s0 = it_s0[i]

    @pl.when((fl & FL_FIRST) != 0)
    def _():
        xs_ref[...] = jnp.zeros_like(xs_ref)
        g_ref[...] = jnp.zeros_like(g_ref)

    @pl.when((fl & FL_VALID) != 0)
    def _():
        srow = slot_ref[pl.ds(e, 1), :]
        prow = p_ref[pl.ds(e, 1), :]
        iota_s = lax.broadcasted_iota(jnp.int32, (MOE_RB, MOE_SUB), 0)
        for q in range(MOE_TILE // MOE_SUB):
            lo = jnp.maximum(it_ts[i * 5 + q], s0)
            hi = jnp.minimum(it_ts[i * 5 + q + 1], s0 + MOE_SB)

            @pl.when(hi > lo)
            def _():
                sq = srow[:, q * MOE_SUB:(q + 1) * MOE_SUB]
                pq = prow[:, q * MOE_SUB:(q + 1) * MOE_SUB]
                xq = x_ref[q * MOE_SUB:(q + 1) * MOE_SUB, :]

                def rows(r, carry):
                    roff = pl.multiple_of(r * MOE_RB, MOE_RB)
                    oh = (iota_s + (s0 + roff)) == sq
                    xs_ref[pl.ds(roff, MOE_RB), :] += jnp.dot(
                        jnp.where(oh, 1.0, 0.0).astype(BF16), xq, preferred_element_type=F32)
                    g_ref[pl.ds(roff, MOE_RB), :] += jnp.sum(jnp.where(oh, pq, 0.0), axis=1,
                                                            keepdims=True)
                    return carry

                lax.fori_loop((lo - s0) // MOE_RB, (hi - 1 - s0) // MOE_RB + 1, rows, 0)

    @pl.when((fl & FL_LAST) != 0)
    def _():
        xs = xs_ref[...].astype(BF16)
        h1 = jnp.dot(xs, w1_ref[0], preferred_element_type=F32)
        h3 = jnp.dot(xs, w3_ref[0], preferred_element_type=F32)
        hid = (_silu(h1) * h3).astype(BF16)
        y = jnp.dot(hid, w2_ref[0], preferred_element_type=F32) * g_ref[...]
        y_ref[...] = y.astype(y_ref.dtype)


def _moe_items(ts, cap):
    n_grp = ts.shape[1]
    per = MOE_TILE // MOE_SUB
    n_tile = n_grp // per
    n_sb = cap // MOE_SB
    tse = jnp.concatenate([ts, jnp.full((N_EXPERTS, 1), cap, jnp.int32)], axis=1)
    a = tse[:, 0:n_grp:per]
    b = tse[:, per::per]
    kf = a // MOE_SB
    cnt = jnp.where(b > a, (b - 1) // MOE_SB - kf + 1, 0).reshape(-1)
    incl = jnp.cumsum(cnt)
    total = incl[-1]
    n_items = N_EXPERTS * (n_tile + n_sb)
    idx = jnp.arange(n_items, dtype=jnp.int32)
    valid = idx < total
    idc = jnp.minimum(idx, total - 1)
    pair = jnp.searchsorted(incl, idc, side="right").astype(jnp.int32)
    r = idc - (incl[pair] - cnt[pair])
    e = pair // n_tile
    j = pair % n_tile
    k = kf.reshape(-1)[pair] + r
    s0 = k * MOE_SB
    aa = a.reshape(-1)[pair]
    bb = b.reshape(-1)[pair]
    first = jnp.logical_and(aa <= s0, s0 < bb)
    last = jnp.logical_and(aa <= s0 + MOE_SB - 1, s0 + MOE_SB - 1 < bb)
    fl = jnp.where(valid, FL_VALID + FL_FIRST * first + FL_LAST * last, 0).astype(jnp.int32)
    tsi = jnp.stack([tse[e, j * per + q] for q in range(per + 1)], axis=1).reshape(-1)
    return (e.astype(jnp.int32), j.astype(jnp.int32), (e * n_sb + k).astype(jnp.int32),
            s0.astype(jnp.int32), fl, tsi.astype(jnp.int32))


def _moe(hn, slot, probs_t, ts, w1, w3, w2, cap):
    n = hn.shape[0]
    items = _moe_items(ts, cap)
    n_items = items[0].shape[0]
    grid_spec = pltpu.PrefetchScalarGridSpec(
        num_scalar_prefetch=6,
        grid=(n_items,),
        in_specs=[
            pl.BlockSpec((MOE_TILE, D_MODEL), lambda i, e, t, g, s, f, ts: (t[i], 0)),
            pl.BlockSpec((N_EXPERTS, MOE_TILE), lambda i, e, t, g, s, f, ts: (0, t[i])),
            pl.BlockSpec((N_EXPERTS, MOE_TILE), lambda i, e, t, g, s, f, ts: (0, t[i])),
            pl.BlockSpec((1, D_MODEL, EXPERT_FF), lambda i, e, t, g, s, f, ts: (e[i], 0, 0)),
            pl.BlockSpec((1, D_MODEL, EXPERT_FF), lambda i, e, t, g, s, f, ts: (e[i], 0, 0)),
            pl.BlockSpec((1, EXPERT_FF, D_MODEL), lambda i, e, t, g, s, f, ts: (e[i], 0, 0)),
        ],
        out_specs=pl.BlockSpec((MOE_SB, D_MODEL), lambda i, e, t, g, s, f, ts: (g[i], 0)),
        scratch_shapes=[pltpu.VMEM((MOE_SB, D_MODEL), F32), pltpu.VMEM((MOE_SB, 1), F32)],
    )
    return pl.pallas_call(
        _moe_kernel,
        grid_spec=grid_spec,
        out_shape=jax.ShapeDtypeStruct((N_EXPERTS * cap, D_MODEL), BF16),
        compiler_params=_cparams(("arbitrary",)),
        name="moe_dispatch_mlp",
    )(*items, hn, slot, probs_t, w1, w3, w2)


def _combine_kernel(it_e, it_tile, it_yb, it_rel, it_fl, slot_ref, y_ref, h_ref, lnf_ref,
                    out_ref, acc_ref):
    i = pl.program_id(0)
    fl = it_fl[i]

    @pl.when((fl & FL_FIRST) != 0)
    def _():
        acc_ref[...] = jnp.zeros_like(acc_ref)

    @pl.when((fl & FL_VALID) != 0)
    def _():
        srow = slot_ref[pl.ds(it_e[i], 1), :]
        iota_s = lax.broadcasted_iota(jnp.int32, (CMB_YB, CMB_TILE), 0)
        oh = jnp.where((iota_s + it_rel[i]) == srow, 1.0, 0.0).astype(BF16)
        acc_ref[...] += _dot_tn(oh, y_ref[...])

    @pl.when((fl & FL_LAST) != 0)
    def _():
        hh = h_ref[...] + acc_ref[...]
        out_ref[...] = hh * lax.rsqrt(jnp.mean(hh * hh, axis=-1, keepdims=True) + EPS) * lnf_ref[...]


def _combine_items(ts, cap):
    n_grp = ts.shape[1]
    per = CMB_TILE // TOPK_LANES
    n_tile = n_grp // per
    tse = jnp.concatenate([ts, jnp.full((N_EXPERTS, 1), cap, jnp.int32)], axis=1)
    a = tse[:, 0:n_grp:per].T.reshape(-1)
    b = tse[:, per::per].T.reshape(-1)
    kf = jnp.minimum(a, cap - 1) // CMB_YB
    cnt = jnp.where(b > a, (b - 1) // CMB_YB - kf + 1, 1)
    incl = jnp.cumsum(cnt)
    total = incl[-1]
    n_items = n_tile * N_EXPERTS + N_EXPERTS * cap // CMB_YB
    idx = jnp.arange(n_items, dtype=jnp.int32)
    valid = idx < total
    idc = jnp.minimum(idx, total - 1)
    pair = jnp.searchsorted(incl, idc, side="right").astype(jnp.int32)
    r = idc - (incl[pair] - cnt[pair])
    tile = pair // N_EXPERTS
    e = pair % N_EXPERTS
    k = kf[pair] + r
    first = jnp.logical_and(e == 0, r == 0)
    last = jnp.logical_and(e == N_EXPERTS - 1, r == cnt[pair] - 1)
    fl = jnp.where(valid, FL_VALID + FL_FIRST * first + FL_LAST * last, 0).astype(jnp.int32)
    yb = e * (cap // CMB_YB) + k
    return (e.astype(jnp.int32), tile.astype(jnp.int32), yb.astype(jnp.int32),
            (k * CMB_YB).astype(jnp.int32), fl)


def _combine(y, slot, ts, h, ln_f, cap):
    n = h.shape[0]
    items = _combine_items(ts, cap)
    n_items = items[0].shape[0]
    grid_spec = pltpu.PrefetchScalarGridSpec(
        num_scalar_prefetch=5,
        grid=(n_items,),
        in_specs=[
            pl.BlockSpec((N_EXPERTS, CMB_TILE), lambda i, e, t, yb, rel, f: (0, t[i])),
            pl.BlockSpec((CMB_YB, D_MODEL), lambda i, e, t, yb, rel, f: (yb[i], 0)),
            pl.BlockSpec((CMB_TILE, D_MODEL), lambda i, e, t, yb, rel, f: (t[i], 0)),
            pl.BlockSpec((1, D_MODEL), lambda i, e, t, yb, rel, f: (0, 0)),
        ],
        out_specs=pl.BlockSpec((CMB_TILE, D_MODEL), lambda i, e, t, yb, rel, f: (t[i], 0)),
        scratch_shapes=[pltpu.VMEM((CMB_TILE, D_MODEL), F32)],
    )
    return pl.pallas_call(
        _combine_kernel,
        grid_spec=grid_spec,
        out_shape=jax.ShapeDtypeStruct((n, D_MODEL), F32),
        compiler_params=_cparams(("arbitrary",)),
        name="moe_combine",
    )(*items, slot, y, h, ln_f)


def _prep_in_weights(w_in, a_log_f, a_log_b, dt_bias_f, dt_bias_b):
    offs = np.cumsum([0, A_QK, A_QK, A_V, A_V, A_HEADS, A_HEADS, A_HEADS, A_HEADS,
                      B_QK, B_QK, B_V, B_V, GLA_LOWRANK, GLA_LOWRANK])
    seg = [w_in[:, offs[i]:offs[i + 1]] for i in range(14)]
    w_main = jnp.concatenate(seg[0:4] + seg[8:12], axis=1).astype(BF16)
    w_small = jnp.concatenate(seg[4:8] + seg[12:14], axis=1)
    w_small = jnp.pad(w_small, ((0, 0), (0, SMALL_WIDTH - w_small.shape[1]))).astype(BF16)
    pad = SMALL_WIDTH - 4 * A_HEADS
    z = jnp.zeros((2 * A_HEADS,), F32)
    a_row = jnp.concatenate([z, a_log_f.astype(F32), a_log_b.astype(F32), jnp.zeros((pad,), F32)])
    dt_row = jnp.concatenate([z, dt_bias_f.astype(F32), dt_bias_b.astype(F32), jnp.zeros((pad,), F32)])
    small_params = jnp.zeros((8, SMALL_WIDTH), F32).at[0].set(a_row).at[1].set(dt_row)
    return w_main, w_small, small_params


def kernel(x_prompt, x_sample, ln1, w_in, conv_w, a_log_f, a_log_b, dt_bias_f, dt_bias_b, norm_a, gla_w_f, gla_b_f, gla_w_b, gla_b_b, norm_b, w_out, ln2, w_router, w1, w3, w2, ln_f):
    w_main, w_small, small_params = _prep_in_weights(w_in[0], a_log_f[0], a_log_b[0],
                                                     dt_bias_f[0], dt_bias_b[0])
    conv8 = _conv_layout(conv_w[0])
    wg, bg = _gla_gate_weights(gla_w_f[0], gla_b_f[0], gla_w_b[0], gla_b_b[0])
    wl = jnp.asarray(_gla_level_weights(), BF16)
    wo = w_out[0].astype(BF16)
    wr_t = w_router[0].astype(F32).T
    wrh = wr_t.astype(BF16)
    wrl = (wr_t - wrh.astype(F32)).astype(BF16)
    w1b, w3b, w2b = w1[0].astype(BF16), w3[0].astype(BF16), w2[0].astype(BF16)
    row = lambda v: v.astype(F32).reshape(1, -1)
    outs = []
    for x in (x_prompt, x_sample):
        bsz, seq, _ = x.shape
        n = bsz * seq
        cap = EC_CAPACITY * n // N_EXPERTS
        x2d = x.reshape(n, D_MODEL)
        main, small = _in_proj(x2d, row(ln1[0]), w_main, w_small, small_params, tm=1024, tn=1024)
        main3 = main.reshape(bsz, seq, MAIN_WIDTH)
        small3 = small.reshape(bsz, seq, SMALL_WIDTH)
        oaf, oab = _gdn(_qkv_conv(main3, conv8, rows=SCAN_ROWS), _scal_layout(small3))
        obf, obb = _gla(main3, small3, wg, bg, wl)
        flat = lambda o: o.reshape(n, -1)
        h, hn, probs_t = _out_proj(flat(oaf), flat(oab), flat(obf), flat(obb), main, x2d, wo,
                                   row(norm_a[0]), row(norm_b[0]), row(ln2[0]), wrh, wrl, tm=256)
        slot, ts = _topk(probs_t, cap)
        y = _moe(hn, slot, probs_t, ts, w1b, w3b, w2b, cap)
        out = _combine(y, slot, ts, h, row(ln_f), cap)
        outs.append(out.reshape(bsz, seq, D_MODEL))
    return tuple(outs)
```

```python
import functools

import jax
import jax.numpy as jnp
import numpy as np
from jax import lax
from jax.experimental import pallas as pl
from jax.experimental.pallas import tpu as pltpu

F32 = jnp.float32
BF16 = jnp.bfloat16

D_MODEL = 2048
A_HEADS, A_DK, A_DV = 8, 128, 128
B_HEADS, B_DK, B_DV = 4, 128, 256
GLA_LOWRANK = 16
GLA_GATE_NORM = 16.0
CONV_K = 5
CHUNK = 64
N_EXPERTS = 16
EC_CAPACITY = 2
EXPERT_FF = D_MODEL // 2
EPS = 1e-6

A_QK = A_HEADS * A_DK
A_V = A_HEADS * A_DV
B_QK = B_HEADS * B_DK
B_V = B_HEADS * B_DV
MAIN_WIDTH = 2 * A_QK + 2 * A_V + 2 * B_QK + 2 * B_V
SMALL_WIDTH = 128
LR_OFF = 4 * A_HEADS
V7X_VMEM_LIMIT = 56 * 1024 * 1024
CONV_HALO = 16
SCAN_ROWS = 256
CPS = SCAN_ROWS // CHUNK
GLA_LEVELS = 6
GATE_LANES = 128
GDN_HEADS_PER_PASS = 4


def _cparams(sem):
    return pltpu.CompilerParams(dimension_semantics=sem, vmem_limit_bytes=V7X_VMEM_LIMIT)


def _dot(a, b):
    return jnp.dot(a.astype(BF16), b.astype(BF16), preferred_element_type=F32)


def _dot_nt(a, b):
    return lax.dot_general(a.astype(BF16), b.astype(BF16), (((1,), (1,)), ((), ())),
                           preferred_element_type=F32)


def _dot_tn(a, b):
    return lax.dot_general(a.astype(BF16), b.astype(BF16), (((0,), (0,)), ((), ())),
                           preferred_element_type=F32)


def _silu(x):
    return x * (1.0 / (1.0 + jnp.exp(-x)))


def _in_proj_kernel(x_ref, ln_ref, wm_ref, ws_ref, sp_ref, main_ref, small_ref, hn_ref):
    j = pl.program_id(1)

    @pl.when(j == 0)
    def _():
        xf = x_ref[...]
        y = xf * lax.rsqrt(jnp.mean(xf * xf, axis=-1, keepdims=True) + EPS) * ln_ref[...]
        hn = y.astype(BF16)
        hn_ref[...] = hn
        s = jnp.dot(hn, ws_ref[...], preferred_element_type=F32)
        lane = lax.broadcasted_iota(jnp.int32, s.shape, 1)
        neg_a = -jnp.exp(sp_ref[0:1, :])
        z = s + sp_ref[1:2, :]
        softplus = jnp.maximum(z, 0.0) + jnp.log(1.0 + jnp.exp(-jnp.abs(z)))
        sig = 1.0 / (1.0 + jnp.exp(-s))
        small_ref[...] = jnp.where(lane < 2 * A_HEADS, sig,
                                   jnp.where(lane < 4 * A_HEADS, neg_a * softplus, s))

    main_ref[...] = jnp.dot(hn_ref[...], wm_ref[...], preferred_element_type=F32).astype(BF16)


def _in_proj(x2d, ln1, w_main, w_small, small_params, tm, tn):
    t = x2d.shape[0]
    return pl.pallas_call(
        _in_proj_kernel,
        grid=(t // tm, MAIN_WIDTH // tn),
        in_specs=[
            pl.BlockSpec((tm, D_MODEL), lambda i, j: (i, 0)),
            pl.BlockSpec((1, D_MODEL), lambda i, j: (0, 0)),
            pl.BlockSpec((D_MODEL, tn), lambda i, j: (0, j)),
            pl.BlockSpec((D_MODEL, SMALL_WIDTH), lambda i, j: (0, 0)),
            pl.BlockSpec((8, SMALL_WIDTH), lambda i, j: (0, 0)),
        ],
        out_specs=[
            pl.BlockSpec((tm, tn), lambda i, j: (i, j)),
            pl.BlockSpec((tm, SMALL_WIDTH), lambda i, j: (i, 0)),
        ],
        out_shape=[
            jax.ShapeDtypeStruct((t, MAIN_WIDTH), BF16),
            jax.ShapeDtypeStruct((t, SMALL_WIDTH), F32),
        ],
        scratch_shapes=[pltpu.VMEM((tm, D_MODEL), BF16)],
        compiler_params=_cparams(("arbitrary", "arbitrary")),
        name="in_proj",
    )(x2d, ln1, w_main, w_small, small_params)


def _chunk_iotas():
    ii = lax.broadcasted_iota(jnp.int32, (CHUNK, CHUNK), 0)
    jj = lax.broadcasted_iota(jnp.int32, (CHUNK, CHUNK), 1)
    return ii, jj


def _row_to_col(row, eye):
    return jnp.sum(jnp.where(eye, row, 0.0), axis=1, keepdims=True)


def _col_to_row(col, eye):
    return jnp.sum(jnp.where(eye, col, 0.0), axis=0, keepdims=True)


def _l2norm(x):
    return x * lax.rsqrt(jnp.sum(x * x, axis=-1, keepdims=True) + EPS)


def _delta_chunks(qs, ks, vs, beta_rows, g_rows, states, revs):
    ii, jj = _chunk_iotas()
    eye = ii == jj
    nc = range(len(qs))
    incl = [(ii <= jj) if r else (ii >= jj) for r in revs]
    strict = [(ii < jj) if r else (ii > jj) for r in revs]
    gc_col = [jnp.sum(jnp.where(incl[c], g_rows[c], 0.0), axis=1, keepdims=True) for c in nc]
    gc_row = [_col_to_row(gc_col[c], eye) for c in nc]
    beta_col = [_row_to_col(beta_rows[c], eye) for c in nc]
    g_tot = [jnp.sum(g_rows[c], axis=1, keepdims=True) for c in nc]
    decay = [jnp.where(incl[c], jnp.exp(jnp.where(incl[c], gc_col[c] - gc_row[c], 0.0)), 0.0)
             for c in nc]
    kb = [ks[c] * beta_col[c] for c in nc]
    kk = [_dot_nt(kb[c], ks[c]) for c in nc]
    qk = [_dot_nt(qs[c], ks[c]) for c in nc]
    p = [jnp.where(strict[c], -kk[c] * decay[c], 0.0) for c in nc]
    toff = p
    for _ in range(5):
        p = [_dot(p[c], p[c]) for c in nc]
        tp = [_dot(toff[c], p[c]) for c in nc]
        toff = [toff[c] + p[c] + tp[c] for c in nc]
    e_gc = [jnp.exp(gc_col[c]) for c in nc]
    rhs = [jnp.concatenate([vs[c] * beta_col[c], kb[c] * e_gc[c]], axis=1) for c in nc]
    sol = [rhs[c] + _dot(toff[c], rhs[c]) for c in nc]
    attn = [jnp.where(incl[c], qk[c] * decay[c], 0.0) for c in nc]
    ws = [_dot(sol[c][:, A_DV:], states[c]) for c in nc]
    qs_state = [_dot(qs[c] * e_gc[c], states[c]) for c in nc]
    v_new = [sol[c][:, :A_DV] - ws[c] for c in nc]
    av = [_dot(attn[c], v_new[c]) for c in nc]
    kv = [_dot_tn(ks[c] * jnp.exp(g_tot[c] - gc_col[c]), v_new[c]) for c in nc]
    outs = [qs_state[c] + av[c] for c in nc]
    new_states = [states[c] * jnp.exp(g_tot[c]) + kv[c] for c in nc]
    return outs, new_states


def _qkv_conv_kernel(blk, prv, nxt, cw_ref, out_ref, pad_ref):
    n = pl.program_id(1)
    nb = pl.num_programs(1)
    rows = blk.shape[1]
    pad_ref[0:CONV_HALO, :] = jnp.where(n == 0, 0.0, prv[0].astype(F32))
    pad_ref[CONV_HALO:CONV_HALO + rows, :] = blk[0].astype(F32)
    pad_ref[CONV_HALO + rows:, :] = jnp.where(n == nb - 1, 0.0, nxt[0].astype(F32))
    lo = CONV_HALO - CONV_K // 2
    for g in range((2 * A_QK + A_V) // A_DK):
        cols = slice(g * A_DK, (g + 1) * A_DK)
        w = cw_ref[:, cols]
        acc = w[0:1, :] * pad_ref[lo:lo + rows, cols]
        for j in range(1, CONV_K):
            acc = acc + w[j:j + 1, :] * pad_ref[lo + j:lo + j + rows, cols]
        y = _silu(acc)
        if g < A_HEADS:
            y = _l2norm(y) * (A_DK ** -0.5)
        elif g < 2 * A_HEADS:
            y = _l2norm(y)
        out_ref[0, :, cols] = y.astype(out_ref.dtype)


def _qkv_conv(main3, conv8, rows):
    bsz, seq, _ = main3.shape
    width = 2 * A_QK + A_V
    hb = rows // CONV_HALO
    nhalo = seq // CONV_HALO
    return pl.pallas_call(
        _qkv_conv_kernel,
        grid=(bsz, seq // rows),
        in_specs=[
            pl.BlockSpec((1, rows, width), lambda b, n: (b, n, 0)),
            pl.BlockSpec((1, CONV_HALO, width), lambda b, n: (b, jnp.maximum(n * hb - 1, 0), 0)),
            pl.BlockSpec((1, CONV_HALO, width),
                         lambda b, n: (b, jnp.minimum((n + 1) * hb, nhalo - 1), 0)),
            pl.BlockSpec((8, width), lambda b, n: (0, 0)),
        ],
        out_specs=pl.BlockSpec((1, rows, width), lambda b, n: (b, n, 0)),
        out_shape=jax.ShapeDtypeStruct((bsz, seq, width), BF16),
        scratch_shapes=[pltpu.VMEM((rows + 2 * CONV_HALO, width), F32)],
        compiler_params=_cparams(("arbitrary", "arbitrary")),
        name="qkv_conv",
    )(main3, main3, main3, conv8)


def _gdn_kernel(blk_f, blk_b, sc_f, sc_b, of_ref, ob_ref, st_ref):
    n = pl.program_id(1)

    @pl.when(n == 0)
    def _():
        st_ref[...] = jnp.zeros_like(st_ref)

    def load(d, row0, a, h):
        blk = blk_f if d == 0 else blk_b
        col = a * A_QK + h * A_DK
        return blk[0, pl.ds(row0, CHUNK), col:col + A_DK].astype(F32)

    def body(s, carry):
        for h0 in range(0, A_HEADS, GDN_HEADS_PER_PASS):
            chains = []
            for d, sc_ref in enumerate((sc_f, sc_b)):
                sc = s if d == 0 else CPS - 1 - s
                row0 = pl.multiple_of(sc * CHUNK, CHUNK)
                for h in range(h0, h0 + GDN_HEADS_PER_PASS):
                    chains.append((d, h, sc, row0, sc_ref))
            qs = [load(d, row0, 0, h) for d, h, sc, row0, _ in chains]
            ks = [load(d, row0, 1, h) for d, h, sc, row0, _ in chains]
            vs = [load(d, row0, 2, h) for d, h, sc, row0, _ in chains]
            betas = [r[0, 0, d, h, pl.ds(sc, 1), :] for d, h, sc, row0, r in chains]
            gs = [r[0, 0, 2 + d, h, pl.ds(sc, 1), :] for d, h, sc, row0, r in chains]
            states = [st_ref[d, h] for d, h, sc, row0, _ in chains]
            outs, new_states = _delta_chunks(qs, ks, vs, betas, gs, states,
                                             [d == 1 for d, *_ in chains])
            for (d, h, sc, row0, _), o, st in zip(chains, outs, new_states):
                st_ref[d, h] = st
                out_ref = of_ref if d == 0 else ob_ref
                out_ref[0, pl.ds(row0, CHUNK), h * A_DV:(h + 1) * A_DV] = o.astype(out_ref.dtype)
        return carry

    lax.fori_loop(0, CPS, body, 0)


def _scal_layout(small):
    bsz, seq = small.shape[:2]
    s = small[..., :4 * A_HEADS].reshape(bsz, seq // SCAN_ROWS, CPS, CHUNK, 4, A_HEADS)
    return jnp.transpose(s, (0, 1, 4, 5, 2, 3))


def _conv_layout(conv_w):
    return jnp.pad(conv_w.astype(F32), ((0, 8 - CONV_K), (0, 0)))


def _gdn(qkv, scal):
    bsz, seq, width = qkv.shape
    nb = seq // SCAN_ROWS

    def specs(blk_of):
        return (pl.BlockSpec((1, SCAN_ROWS, width), lambda b, n: (b, blk_of(n), 0)),
                pl.BlockSpec((1, 1, 4, A_HEADS, CPS, CHUNK), lambda b, n: (b, blk_of(n), 0, 0, 0, 0)))

    qf, sf = specs(lambda n: n)
    qb, sb = specs(lambda n: nb - 1 - n)
    out_sds = jax.ShapeDtypeStruct((bsz, seq, A_V), BF16)
    return pl.pallas_call(
        _gdn_kernel,
        grid=(bsz, nb),
        in_specs=[qf, qb, sf, sb],
        out_specs=[
            pl.BlockSpec((1, SCAN_ROWS, A_V), lambda b, n: (b, n, 0)),
            pl.BlockSpec((1, SCAN_ROWS, A_V), lambda b, n: (b, nb - 1 - n, 0)),
        ],
        out_shape=[out_sds, out_sds],
        scratch_shapes=[pltpu.VMEM((2, A_HEADS, A_DK, A_DV), F32)],
        compiler_params=_cparams(("arbitrary", "arbitrary")),
        name="gdn",
    )(qkv, qkv, scal, scal)


def _gla_level_weights():
    c = CHUNK
    w = np.zeros((2, 8 * c, c), np.float32)
    for d in range(2):
        for l in range(GLA_LEVELS):
            s = c >> (l + 1)
            for i in range(c):
                mid = (i // (2 * s)) * 2 * s + s
                if d == 0:
                    ts = range(mid, i + 1) if i >= mid else range(i + 1, mid)
                else:
                    ts = range(i, mid) if i < mid else range(mid, i)
                for t in ts:
                    w[d, l * c + i, t] = 1.0
        for i in range(c):
            for t in range(c):
                before = t <= i if d == 0 else t >= i
                w[d, 6 * c + i, t] = 1.0 if before else 0.0
                w[d, 7 * c + i, t] = 0.0 if before else 1.0
    return w


def _gla_kernel(qk_f, v_f, sm_f, qk_b, v_b, sm_b, wg_ref, bg_ref, wl_ref, of_ref, ob_ref, st_ref):
    n = pl.program_id(1)

    @pl.when(n == 0)
    def _():
        st_ref[...] = jnp.zeros_like(st_ref)

    ii, jj = _chunk_iotas()
    eye = ii == jj
    tok = lax.broadcasted_iota(jnp.int32, (CHUNK, 1), 0)

    streams = ((qk_f, v_f, sm_f, of_ref), (qk_b, v_b, sm_b, ob_ref))
    dirs = range(2)
    chains = [(d, h) for d in dirs for h in range(B_HEADS)]
    masks = [ii // (CHUNK >> l) == jj // (CHUNK >> l) for l in range(GLA_LEVELS)]

    def body(s, carry):
        rows = [pl.ds(pl.multiple_of((s if d == 0 else CPS - 1 - s) * CHUNK, CHUNK), CHUNK)
                for d in dirs]
        z = [_dot(streams[d][2][0, rows[d], :], wg_ref[d]) + bg_ref[d] for d in dirs]
        la = [(jnp.minimum(z[d], 0.0) - jnp.log(1.0 + jnp.exp(-jnp.abs(z[d]))))
              * (1.0 / GLA_GATE_NORM) for d in dirs]
        hi = [la[d].astype(BF16) for d in dirs]
        r1 = [la[d] - hi[d].astype(F32) for d in dirs]
        mid = [r1[d].astype(BF16) for d in dirs]
        low = [(r1[d] - mid[d].astype(F32)).astype(BF16) for d in dirs]
        ex = [jnp.dot(wl_ref[d], hi[d], preferred_element_type=F32)
              + jnp.dot(wl_ref[d], mid[d], preferred_element_type=F32)
              + jnp.dot(wl_ref[d], low[d], preferred_element_type=F32) for d in dirs]
        e_all = [jnp.exp(ex[d]) for d in dirs]
        q_all = [streams[d][0][0, rows[d], :B_QK].astype(F32) * (B_DK ** -0.5) for d in dirs]
        k_all = [streams[d][0][0, rows[d], B_QK:].astype(F32) for d in dirs]
        qs, ks = [], []
        for l in range(GLA_LEVELS):
            half = CHUNK >> (l + 1)
            right = (tok // half) % 2 == 1
            q_side = [right, jnp.logical_not(right)]
            e_l = [e_all[d][l * CHUNK:(l + 1) * CHUNK] for d in dirs]
            qs.append([jnp.where(q_side[d], q_all[d] * e_l[d], 0.0).astype(BF16) for d in dirs])
            ks.append([jnp.where(q_side[d], 0.0, k_all[d] * e_l[d]).astype(BF16) for d in dirs])
        e_b = [e_all[d][6 * CHUNK:7 * CHUNK] for d in dirs]
        q_dec = [(q_all[d] * e_b[d]).astype(BF16) for d in dirs]
        k_dec = [(k_all[d] * e_all[d][7 * CHUNK:]).astype(BF16) for d in dirs]
        e_tot = [e_b[0][CHUNK - 1:CHUNK], e_b[1][0:1]]
        ck = [slice(h * B_DK, (h + 1) * B_DK) for h in range(B_HEADS)]
        cv = [slice(h * B_DV, (h + 1) * B_DV) for h in range(B_HEADS)]
        attn = [jnp.where(eye, _dot_nt(q_all[d][:, ck[h]], k_all[d][:, ck[h]]), 0.0)
                for d, h in chains]
        for l in range(GLA_LEVELS):
            part = [_dot_nt(qs[l][d][:, ck[h]], ks[l][d][:, ck[h]]) for d, h in chains]
            attn = [attn[c] + jnp.where(masks[l], part[c], 0.0) for c in range(len(chains))]
        v = [streams[d][1][0, rows[d], cv[h]] for d, h in chains]
        st = [st_ref[d, h] for d, h in chains]
        o_state = [_dot_nt(q_dec[d][:, ck[h]], st[c]) for c, (d, h) in enumerate(chains)]
        o_local = [_dot(attn[c], v[c]) for c in range(len(chains))]
        kv = [_dot_tn(v[c], k_dec[d][:, ck[h]]) for c, (d, h) in enumerate(chains)]
        for c, (d, h) in enumerate(chains):
            st_ref[d, h] = st[c] * e_tot[d][:, ck[h]] + kv[c]
            streams[d][3][0, rows[d], cv[h]] = (o_state[c] + o_local[c]).astype(of_ref.dtype)
        return carry

    lax.fori_loop(0, CPS, body, 0)


def _gla(main3, small3, wg, bg, wl):
    bsz, seq, _ = main3.shape
    nb = seq // SCAN_ROWS
    qk_blk = (2 * A_QK + 2 * A_V) // (2 * B_QK)
    v_blk = (2 * A_QK + 2 * A_V + 2 * B_QK) // B_V

    def stream(blk_of):
        return [
            pl.BlockSpec((1, SCAN_ROWS, 2 * B_QK), lambda b, n: (b, blk_of(n), qk_blk)),
            pl.BlockSpec((1, SCAN_ROWS, B_V), lambda b, n: (b, blk_of(n), v_blk)),
            pl.BlockSpec((1, SCAN_ROWS, SMALL_WIDTH), lambda b, n: (b, blk_of(n), 0)),
        ]

    out_sds = jax.ShapeDtypeStruct((bsz, seq, B_V), BF16)
    return pl.pallas_call(
        _gla_kernel,
        grid=(bsz, nb),
        in_specs=stream(lambda n: n) + stream(lambda n: nb - 1 - n) + [
            pl.BlockSpec((2, SMALL_WIDTH, B_QK), lambda b, n: (0, 0, 0)),
            pl.BlockSpec((2, 1, B_QK), lambda b, n: (0, 0, 0)),
            pl.BlockSpec((2, 8 * CHUNK, CHUNK), lambda b, n: (0, 0, 0)),
        ],
        out_specs=[
            pl.BlockSpec((1, SCAN_ROWS, B_V), lambda b, n: (b, n, 0)),
            pl.BlockSpec((1, SCAN_ROWS, B_V), lambda b, n: (b, nb - 1 - n, 0)),
        ],
        out_shape=[out_sds, out_sds],
        scratch_shapes=[pltpu.VMEM((2, B_HEADS, B_DV, B_DK), F32)],
        compiler_params=_cparams(("arbitrary", "arbitrary")),
        name="gla",
    )(main3, main3, small3, main3, main3, small3, wg, bg, wl)


def _gla_gate_weights(gla_w_f, gla_b_f, gla_w_b, gla_b_b):
    wg = jnp.zeros((2, SMALL_WIDTH, B_QK), F32)
    wg = wg.at[0, LR_OFF:LR_OFF + GLA_LOWRANK].set(gla_w_f.astype(F32))
    wg = wg.at[1, LR_OFF + GLA_LOWRANK:LR_OFF + 2 * GLA_LOWRANK].set(gla_w_b.astype(F32))
    bg = jnp.stack([gla_b_f, gla_b_b]).astype(F32).reshape(2, 1, B_QK)
    return wg.astype(BF16), bg


def _out_proj_kernel(oaf, oab, obf, obb, ga, gb, x_ref, wo_ref, na_ref, nb_ref, ln2_ref,
                     wrh_ref, wrl_ref, wrh2_ref, wrl2_ref, h_ref, hn_ref, pt_ref, p3_ref, mix_ref):
    def head_norm(of_ref, ob_ref, g_ref, w_ref, width, heads, base):
        for h in range(heads):
            c = slice(h * width, (h + 1) * width)
            o = of_ref[:, c].astype(F32) + ob_ref[:, c].astype(F32)
            y = o * lax.rsqrt(jnp.mean(o * o, axis=-1, keepdims=True) + EPS) * w_ref[...]
            mix_ref[:, base + h * width:base + (h + 1) * width] = (
                y * _silu(g_ref[:, c].astype(F32))).astype(BF16)

    head_norm(oaf, oab, ga, na_ref, A_DV, A_HEADS, 0)
    head_norm(obf, obb, gb, nb_ref, B_DV, B_HEADS, A_V)
    hres = x_ref[...] + jnp.dot(mix_ref[...], wo_ref[...], preferred_element_type=F32)
    h_ref[...] = hres
    hn = hres * lax.rsqrt(jnp.mean(hres * hres, axis=-1, keepdims=True) + EPS) * ln2_ref[...]
    hi = hn.astype(BF16)
    hn_ref[...] = hi
    lo = (hn - hi.astype(F32)).astype(BF16)
    lt = _dot_nt(wrh_ref[...], hi) + _dot_nt(wrl_ref[...], hi) + _dot_nt(wrh_ref[...], lo)
    e = jnp.exp(lt - jnp.max(lt, axis=0, keepdims=True))
    pt_ref[...] = e / jnp.sum(e, axis=0, keepdims=True)
    l2 = _dot(hi, wrh2_ref[...]) + _dot(hi, wrl2_ref[...]) + _dot(lo, wrh2_ref[...])
    lane = lax.broadcasted_iota(jnp.int32, l2.shape, 1)
    l2 = jnp.where(lane < N_EXPERTS, l2, -jnp.inf)
    e2 = jnp.exp(l2 - jnp.max(l2, axis=1, keepdims=True))
    p2 = e2 / jnp.sum(e2, axis=1, keepdims=True)
    g_hi = p2.astype(BF16)
    r1 = p2 - g_hi.astype(F32)
    g_mid = r1.astype(BF16)
    g_lo = (r1 - g_mid.astype(F32)).astype(BF16)
    p3 = (g_hi.astype(F32) + pltpu.roll(g_mid.astype(F32), N_EXPERTS, 1)
          + pltpu.roll(g_lo.astype(F32), 2 * N_EXPERTS, 1))
    p3_ref[...] = p3.astype(BF16)


def _out_proj(oaf, oab, obf, obb, main, x2d, wo, na, nb, ln2, wrh, wrl, wrh2, wrl2, tm):
    t = x2d.shape[0]
    ga_blk = (2 * A_QK + A_V) // A_V
    gb_blk = (2 * A_QK + 2 * A_V + 2 * B_QK + B_V) // B_V
    row = lambda w: pl.BlockSpec((tm, w), lambda i: (i, 0))
    full = lambda a, b: pl.BlockSpec((a, b), lambda i: (0, 0))
    return pl.pallas_call(
        _out_proj_kernel,
        grid=(t // tm,),
        in_specs=[
            row(A_V), row(A_V), row(B_V), row(B_V),
            pl.BlockSpec((tm, A_V), lambda i: (i, ga_blk)),
            pl.BlockSpec((tm, B_V), lambda i: (i, gb_blk)),
            row(D_MODEL), full(D_MODEL, D_MODEL), full(1, A_DV), full(1, B_DV), full(1, D_MODEL),
            full(N_EXPERTS, D_MODEL), full(N_EXPERTS, D_MODEL),
            full(D_MODEL, GATE_LANES), full(D_MODEL, GATE_LANES),
        ],
        out_specs=[row(D_MODEL), row(D_MODEL), pl.BlockSpec((N_EXPERTS, tm), lambda i: (0, i)),
                   row(GATE_LANES)],
        out_shape=[
            jax.ShapeDtypeStruct((t, D_MODEL), F32),
            jax.ShapeDtypeStruct((t, D_MODEL), BF16),
            jax.ShapeDtypeStruct((N_EXPERTS, t), F32),
            jax.ShapeDtypeStruct((t, GATE_LANES), BF16),
        ],
        scratch_shapes=[pltpu.VMEM((tm, D_MODEL), BF16)],
        compiler_params=_cparams(("arbitrary",)),
        name="out_proj",
    )(oaf, oab, obf, obb, main, main, x2d, wo, na, nb, ln2, wrh, wrl, wrh2, wrl2)


TOPK_LANES = 256


def _topk_kernel(p_ref, slot_ref, ts_ref, *, cap):
    n = p_ref.shape[1]
    n_steps = n // TOPK_LANES
    bits = pltpu.bitcast(p_ref[...], jnp.int32)

    def bisect(i, thr):
        cand = thr | jnp.left_shift(jnp.int32(1), 30 - i)
        cnt = jnp.sum(jnp.where(bits >= cand, 1.0, 0.0), axis=1, keepdims=True)
        return jnp.where(cnt >= cap, cand, thr)

    thr = lax.fori_loop(0, 31, bisect, jnp.zeros((N_EXPERTS, 1), jnp.int32))
    need = cap - jnp.sum(jnp.where(bits > thr, 1.0, 0.0), axis=1, keepdims=True)
    r = lax.broadcasted_iota(jnp.int32, (TOPK_LANES, TOPK_LANES), 0)
    c = lax.broadcasted_iota(jnp.int32, (TOPK_LANES, TOPK_LANES), 1)
    tri = jnp.where(r <= c, 1.0, 0.0).astype(BF16)
    step_lane = lax.broadcasted_iota(jnp.int32, (N_EXPERTS, n_steps), 1)

    ts_ref[...] = jnp.zeros_like(ts_ref)

    def step(j, carry):
        ties_before, sel_before = carry
        lanes = pl.ds(pl.multiple_of(j * TOPK_LANES, TOPK_LANES), TOPK_LANES)
        pb = pltpu.bitcast(p_ref[:, lanes], jnp.int32)
        tie = jnp.where(pb == thr, 1.0, 0.0)
        tie_incl = jnp.dot(tie.astype(BF16), tri, preferred_element_type=F32)
        sel = jnp.logical_or(pb > thr, jnp.logical_and(pb == thr, ties_before + tie_incl - tie < need))
        self_f = jnp.where(sel, 1.0, 0.0)
        sel_incl = jnp.dot(self_f.astype(BF16), tri, preferred_element_type=F32)
        pos = sel_before + sel_incl - self_f
        slot_ref[:, lanes] = jnp.where(sel, pos, -1.0).astype(jnp.int32)
        ts_ref[...] = jnp.where(step_lane == j, sel_before.astype(jnp.int32), ts_ref[...])
        return (ties_before + tie_incl[:, TOPK_LANES - 1:],
                sel_before + sel_incl[:, TOPK_LANES - 1:])

    zero = jnp.zeros((N_EXPERTS, 1), F32)
    lax.fori_loop(0, n_steps, step, (zero, zero))


def _topk(probs_t, cap):
    n = probs_t.shape[1]
    return pl.pallas_call(
        functools.partial(_topk_kernel, cap=cap),
        out_shape=[
            jax.ShapeDtypeStruct((N_EXPERTS, n), jnp.int32),
            jax.ShapeDtypeStruct((N_EXPERTS, n // TOPK_LANES), jnp.int32),
        ],
        compiler_params=pltpu.CompilerParams(vmem_limit_bytes=V7X_VMEM_LIMIT),
        name="topk",
    )(probs_t)


MOE_TILE = 1024
MOE_SUB = TOPK_LANES
MOE_SB = 512
MOE_RB = 128
CMB_TILE = 512
CMB_YB = 256
FL_VALID, FL_FIRST, FL_LAST = 1, 2, 4


def _moe_kernel(it_e, it_tile, it_g, it_s0, it_fl, it_ts, x_ref, slot_ref, p_ref,
                w1_ref, w3_ref, w2_ref, y_ref, xs_ref, g_ref):
    i = pl.program_id(0)
    fl = it_fl[i]
    e = it_e[i]
    s0 = it_s0[i]

    @pl.when((fl & FL_FIRST) != 0)
    def _():
        xs_ref[...] = jnp.zeros_like(xs_ref)
        g_ref[...] = jnp.zeros_like(g_ref)

    @pl.when((fl & FL_VALID) != 0)
    def _():
        srow = slot_ref[pl.ds(e, 1), :]
        prow = p_ref[pl.ds(e, 1), :]
        iota_s = lax.broadcasted_iota(jnp.int32, (MOE_RB, MOE_SUB), 0)
        for q in range(MOE_TILE // MOE_SUB):
            lo = jnp.maximum(it_ts[i * 5 + q], s0)
            hi = jnp.minimum(it_ts[i * 5 + q + 1], s0 + MOE_SB)

            @pl.when(hi > lo)
            def _():
                sq = srow[:, q * MOE_SUB:(q + 1) * MOE_SUB]
                pq = prow[:, q * MOE_SUB:(q + 1) * MOE_SUB]
                xq = x_ref[q * MOE_SUB:(q + 1) * MOE_SUB, :]

                def rows(r, carry):
                    roff = pl.multiple_of(r * MOE_RB, MOE_RB)
                    oh = (iota_s + (s0 + roff)) == sq
                    xs_ref[pl.ds(roff, MOE_RB), :] += jnp.dot(
                        jnp.where(oh, 1.0, 0.0).astype(BF16), xq, preferred_element_type=F32)
                    g_ref[pl.ds(roff, MOE_RB), :] += jnp.sum(jnp.where(oh, pq, 0.0), axis=1,
                                                            keepdims=True)
                    return carry

                lax.fori_loop((lo - s0) // MOE_RB, (hi - 1 - s0) // MOE_RB + 1, rows, 0)

    @pl.when((fl & FL_LAST) != 0)
    def _():
        xs = xs_ref[...].astype(BF16)
        h1 = jnp.dot(xs, w1_ref[0], preferred_element_type=F32)
        h3 = jnp.dot(xs, w3_ref[0], preferred_element_type=F32)
        hid = (_silu(h1) * h3).astype(BF16)
        y = jnp.dot(hid, w2_ref[0], preferred_element_type=F32) * g_ref[...]
        y_ref[...] = y.astype(y_ref.dtype)


def _moe_items(ts, cap):
    n_grp = ts.shape[1]
    per = MOE_TILE // MOE_SUB
    n_tile = n_grp // per
    n_sb = cap // MOE_SB
    tse = jnp.concatenate([ts, jnp.full((N_EXPERTS, 1), cap, jnp.int32)], axis=1)
    a = tse[:, 0:n_grp:per]
    b = tse[:, per::per]
    kf = a // MOE_SB
    cnt = jnp.where(b > a, (b - 1) // MOE_SB - kf + 1, 0).reshape(-1)
    incl = jnp.cumsum(cnt)
    total = incl[-1]
    n_items = N_EXPERTS * (n_tile + n_sb)
    idx = jnp.arange(n_items, dtype=jnp.int32)
    valid = idx < total
    idc = jnp.minimum(idx, total - 1)
    pair = jnp.searchsorted(incl, idc, side="right").astype(jnp.int32)
    r = idc - (incl[pair] - cnt[pair])
    e = pair // n_tile
    j = pair % n_tile
    k = kf.reshape(-1)[pair] + r
    s0 = k * MOE_SB
    aa = a.reshape(-1)[pair]
    bb = b.reshape(-1)[pair]
    first = jnp.logical_and(aa <= s0, s0 < bb)
    last = jnp.logical_and(aa <= s0 + MOE_SB - 1, s0 + MOE_SB - 1 < bb)
    fl = jnp.where(valid, FL_VALID + FL_FIRST * first + FL_LAST * last, 0).astype(jnp.int32)
    tsi = jnp.stack([tse[e, j * per + q] for q in range(per + 1)], axis=1).reshape(-1)
    return (e.astype(jnp.int32), j.astype(jnp.int32), (e * n_sb + k).astype(jnp.int32),
            s0.astype(jnp.int32), fl, tsi.astype(jnp.int32))


def _moe(hn, slot, probs_t, ts, w1, w3, w2, cap):
    n = hn.shape[0]
    items = _moe_items(ts, cap)
    n_items = items[0].shape[0]
    grid_spec = pltpu.PrefetchScalarGridSpec(
        num_scalar_prefetch=6,
        grid=(n_items,),
        in_specs=[
            pl.BlockSpec((MOE_TILE, D_MODEL), lambda i, e, t, g, s, f, ts: (t[i], 0)),
            pl.BlockSpec((N_EXPERTS, MOE_TILE), lambda i, e, t, g, s, f, ts: (0, t[i])),
            pl.BlockSpec((N_EXPERTS, MOE_TILE), lambda i, e, t, g, s, f, ts: (0, t[i])),
            pl.BlockSpec((1, D_MODEL, EXPERT_FF), lambda i, e, t, g, s, f, ts: (e[i], 0, 0)),
            pl.BlockSpec((1, D_MODEL, EXPERT_FF), lambda i, e, t, g, s, f, ts: (e[i], 0, 0)),
            pl.BlockSpec((1, EXPERT_FF, D_MODEL), lambda i, e, t, g, s, f, ts: (e[i], 0, 0)),
        ],
        out_specs=pl.BlockSpec((MOE_SB, D_MODEL), lambda i, e, t, g, s, f, ts: (g[i], 0)),
        scratch_shapes=[pltpu.VMEM((MOE_SB, D_MODEL), F32), pltpu.VMEM((MOE_SB, 1), F32)],
    )
    return pl.pallas_call(
        _moe_kernel,
        grid_spec=grid_spec,
        out_shape=jax.ShapeDtypeStruct((N_EXPERTS * cap, D_MODEL), BF16),
        compiler_params=_cparams(("arbitrary",)),
        name="moe_dispatch_mlp",
    )(*items, hn, slot, probs_t, w1, w3, w2)


def _combine_kernel(it_e, it_tile, it_yb, it_rel, it_fl, slot_ref, y_ref, h_ref, lnf_ref,
                    out_ref, acc_ref):
    i = pl.program_id(0)
    fl = it_fl[i]

    @pl.when((fl & FL_FIRST) != 0)
    def _():
        acc_ref[...] = jnp.zeros_like(acc_ref)

    @pl.when((fl & FL_VALID) != 0)
    def _():
        srow = slot_ref[pl.ds(it_e[i], 1), :]
        iota_s = lax.broadcasted_iota(jnp.int32, (CMB_YB, CMB_TILE), 0)
        oh = jnp.where((iota_s + it_rel[i]) == srow, 1.0, 0.0).astype(BF16)
        acc_ref[...] += _dot_tn(oh, y_ref[...])

    @pl.when((fl & FL_LAST) != 0)
    def _():
        hh = h_ref[...] + acc_ref[...]
        out_ref[...] = hh * lax.rsqrt(jnp.mean(hh * hh, axis=-1, keepdims=True) + EPS) * lnf_ref[...]


def _combine_items(ts, cap):
    n_grp = ts.shape[1]
    per = CMB_TILE // TOPK_LANES
    n_tile = n_grp // per
    tse = jnp.concatenate([ts, jnp.full((N_EXPERTS, 1), cap, jnp.int32)], axis=1)
    a = tse[:, 0:n_grp:per].T.reshape(-1)
    b = tse[:, per::per].T.reshape(-1)
    kf = jnp.minimum(a, cap - 1) // CMB_YB
    cnt = jnp.where(b > a, (b - 1) // CMB_YB - kf + 1, 1)
    incl = jnp.cumsum(cnt)
    total = incl[-1]
    n_items = n_tile * N_EXPERTS + N_EXPERTS * cap // CMB_YB
    idx = jnp.arange(n_items, dtype=jnp.int32)
    valid = idx < total
    idc = jnp.minimum(idx, total - 1)
    pair = jnp.searchsorted(incl, idc, side="right").astype(jnp.int32)
    r = idc - (incl[pair] - cnt[pair])
    tile = pair // N_EXPERTS
    e = pair % N_EXPERTS
    k = kf[pair] + r
    first = jnp.logical_and(e == 0, r == 0)
    last = jnp.logical_and(e == N_EXPERTS - 1, r == cnt[pair] - 1)
    fl = jnp.where(valid, FL_VALID + FL_FIRST * first + FL_LAST * last, 0).astype(jnp.int32)
    yb = e * (cap // CMB_YB) + k
    return (e.astype(jnp.int32), tile.astype(jnp.int32), yb.astype(jnp.int32),
            (k * CMB_YB).astype(jnp.int32), fl)


def _combine(y, slot, ts, h, ln_f, cap):
    n = h.shape[0]
    items = _combine_items(ts, cap)
    n_items = items[0].shape[0]
    grid_spec = pltpu.PrefetchScalarGridSpec(
        num_scalar_prefetch=5,
        grid=(n_items,),
        in_specs=[
            pl.BlockSpec((N_EXPERTS, CMB_TILE), lambda i, e, t, yb, rel, f: (0, t[i])),
            pl.BlockSpec((CMB_YB, D_MODEL), lambda i, e, t, yb, rel, f: (yb[i], 0)),
            pl.BlockSpec((CMB_TILE, D_MODEL), lambda i, e, t, yb, rel, f: (t[i], 0)),
            pl.BlockSpec((1, D_MODEL), lambda i, e, t, yb, rel, f: (0, 0)),
        ],
        out_specs=pl.BlockSpec((CMB_TILE, D_MODEL), lambda i, e, t, yb, rel, f: (t[i], 0)),
        scratch_shapes=[pltpu.VMEM((CMB_TILE, D_MODEL), F32)],
    )
    return pl.pallas_call(
        _combine_kernel,
        grid_spec=grid_spec,
        out_shape=jax.ShapeDtypeStruct((n, D_MODEL), F32),
        compiler_params=_cparams(("arbitrary",)),
        name="moe_combine",
    )(*items, slot, y, h, ln_f)


RT_TOK = TOPK_LANES
RT_ROWS = 128
RT_SPECS = 2 * N_EXPERTS
RT_GROUP = 4
XS_WIDTH = D_MODEL + GATE_LANES
SF_VALID, SF_FIRST, SF_LAST = 1, 2, 4
BF_ACTIVE, BF_FIRST = 1, 2


def _route_steps(ts, cap):
    n_sub = ts.shape[1]
    tse = jnp.concatenate([ts, jnp.full((N_EXPERTS, 1), cap, jnp.int32)], axis=1)
    a, b = tse[:, :-1], tse[:, 1:]
    kf = jnp.minimum(a, cap - 1) // RT_ROWS
    kl = jnp.where(b > a, (b - 1) // RT_ROWS, kf)
    rounds = jnp.max((kl - kf + 2) // 2, axis=0)
    incl = jnp.cumsum(rounds)
    total = incl[-1]
    n_steps = 2 * n_sub
    idx = jnp.arange(n_steps, dtype=jnp.int32)
    valid = idx < total
    idc = jnp.minimum(idx, total - 1)
    j = jnp.searchsorted(incl, idc, side="right").astype(jnp.int32)
    r = idc - (incl[j] - rounds[j])
    k0 = (kf[:, j] + 2 * r)[:, :, None]
    par = jnp.arange(2, dtype=jnp.int32)[None, None, :]
    cand = k0 + (par - k0) % 2
    a_s, b_s, kl_s = a[:, j][:, :, None], b[:, j][:, :, None], kl[:, j][:, :, None]
    active = valid[None, :, None] & (cand <= kl_s) & (b_s > a_s)
    first = active & (a_s <= cand * RT_ROWS) & (cand * RT_ROWS < b_s)
    held = lax.cummax(jnp.where(active, cand, -1), axis=1)
    held = jnp.where(held < 0, par, held)
    order = lambda v: jnp.transpose(v, (1, 0, 2)).reshape(-1).astype(jnp.int32)
    bflag = BF_ACTIVE * active + BF_FIRST * first
    sflag = jnp.where(valid, SF_VALID + SF_FIRST * (r == 0) + SF_LAST * (r == rounds[j] - 1), 0)
    return j, order(held), order(bflag), sflag.astype(jnp.int32)


def _route_groups():
    return [[(RT_GROUP // 2 * g + q // 2, q % 2) for q in range(RT_GROUP)]
            for g in range(RT_SPECS // RT_GROUP)]


def _one_hots(i, blk, flg, slot_ref, specs):
    iota = lax.broadcasted_iota(jnp.int32, (RT_ROWS, RT_TOK), 0)
    ohs = []
    for e, par in specs:
        k = i * RT_SPECS + e * 2 + par
        s0 = jnp.where((flg[k] & BF_ACTIVE) != 0, blk[k] * RT_ROWS, -(1 << 30))
        ohs.append(jnp.where((iota + s0) == slot_ref[e:e + 1, :], 1.0, 0.0).astype(BF16))
    return jnp.concatenate(ohs, axis=0)


def _dispatch_kernel(tile, blk, flg, sflg, x_ref, p3_ref, slot_ref, *outs):
    i = pl.program_id(0)

    @pl.when(i == 0)
    def _():
        for out in outs:
            out[...] = jnp.zeros_like(out)

    @pl.when((sflg[i] & SF_VALID) != 0)
    def _():
        xa = jnp.concatenate([x_ref[...], p3_ref[...]], axis=1)
        for specs in _route_groups():
            res = jnp.dot(_one_hots(i, blk, flg, slot_ref, specs), xa, preferred_element_type=F32)
            for q, (e, par) in enumerate(specs):
                out = outs[e * 2 + par]
                first = (flg[i * RT_SPECS + e * 2 + par] & BF_FIRST) != 0
                prev = out[...]
                prev = jnp.where(first, jnp.zeros_like(prev), prev)
                out[...] = prev + res[q * RT_ROWS:(q + 1) * RT_ROWS].astype(out.dtype)


def _dispatch(hn, p3, slot, steps, cap):
    n_steps = steps[3].shape[0]

    def out_spec(e, par):
        return pl.BlockSpec((RT_ROWS, XS_WIDTH),
                            lambda i, t, b, f, s: (b[i * RT_SPECS + e * 2 + par] // 2, 0))

    grid_spec = pltpu.PrefetchScalarGridSpec(
        num_scalar_prefetch=4,
        grid=(n_steps,),
        in_specs=[
            pl.BlockSpec((RT_TOK, D_MODEL), lambda i, t, b, f, s: (t[i], 0)),
            pl.BlockSpec((RT_TOK, GATE_LANES), lambda i, t, b, f, s: (t[i], 0)),
            pl.BlockSpec((N_EXPERTS, RT_TOK), lambda i, t, b, f, s: (0, t[i])),
        ],
        out_specs=[out_spec(e, par) for e in range(N_EXPERTS) for par in range(2)],
    )
    sds = jax.ShapeDtypeStruct((cap // 2, XS_WIDTH), BF16)
    return pl.pallas_call(
        _dispatch_kernel,
        grid_spec=grid_spec,
        out_shape=[sds] * RT_SPECS,
        compiler_params=_cparams(("arbitrary",)),
        name="moe_dispatch",
    )(*steps, hn, p3, slot)


MLP_ROWS = 4 * RT_ROWS


def _expert_mlp_kernel(xe_ref, xo_ref, w1_ref, w3_ref, w2_ref, y_ref, *, e):
    r = RT_ROWS
    xa = jnp.concatenate([xe_ref[0:r], xo_ref[0:r], xe_ref[r:2 * r], xo_ref[r:2 * r]], axis=0)
    xs = xa[:, :D_MODEL]
    g3 = xa[:, D_MODEL:].astype(F32)
    lane = lax.broadcasted_iota(jnp.int32, g3.shape, 1)
    mine = jnp.logical_and((lane & (N_EXPERTS - 1)) == e, lane < 3 * N_EXPERTS)
    gate = jnp.sum(jnp.where(mine, g3, 0.0), axis=1, keepdims=True)
    h1 = jnp.dot(xs, w1_ref[0], preferred_element_type=F32)
    h3 = jnp.dot(xs, w3_ref[0], preferred_element_type=F32)
    hid = (_silu(h1) * h3).astype(BF16)
    y_ref[...] = (jnp.dot(hid, w2_ref[0], preferred_element_type=F32) * gate).astype(y_ref.dtype)


def _expert_mlp(xe, xo, w1, w3, w2, e, cap):
    half = pl.BlockSpec((MLP_ROWS // 2, XS_WIDTH), lambda m: (m, 0))
    return pl.pallas_call(
        functools.partial(_expert_mlp_kernel, e=e),
        grid=(cap // MLP_ROWS,),
        in_specs=[
            half, half,
            pl.BlockSpec((1, D_MODEL, EXPERT_FF), lambda m: (e, 0, 0)),
            pl.BlockSpec((1, D_MODEL, EXPERT_FF), lambda m: (e, 0, 0)),
            pl.BlockSpec((1, EXPERT_FF, D_MODEL), lambda m: (e, 0, 0)),
        ],
        out_specs=pl.BlockSpec((MLP_ROWS, D_MODEL), lambda m: (m, 0)),
        out_shape=jax.ShapeDtypeStruct((cap, D_MODEL), BF16),
        compiler_params=_cparams(("arbitrary",)),
        name="expert_mlp",
    )(xe, xo, w1, w3, w2)


def _combine2_kernel(tile, blk, flg, sflg, slot_ref, h_ref, lnf_ref, *rest):
    ys = rest[:RT_SPECS]
    out_ref, acc_ref = rest[RT_SPECS], rest[RT_SPECS + 1]
    i = pl.program_id(0)
    sf = sflg[i]

    @pl.when((sf & SF_FIRST) != 0)
    def _():
        acc_ref[...] = jnp.zeros_like(acc_ref)

    @pl.when((sf & SF_VALID) != 0)
    def _():
        total = None
        for specs in _route_groups():
            ycat = jnp.concatenate([ys[e * 2 + par][...] for e, par in specs], axis=0)
            t = _dot_tn(_one_hots(i, blk, flg, slot_ref, specs), ycat)
            total = t if total is None else total + t
        acc_ref[...] += total

    @pl.when((sf & SF_LAST) != 0)
    def _():
        hh = h_ref[...] + acc_ref[...]
        out_ref[...] = hh * lax.rsqrt(jnp.mean(hh * hh, axis=-1, keepdims=True) + EPS) * lnf_ref[...]


def _combine2(ys, slot, steps, h, ln_f):
    n = h.shape[0]
    n_steps = steps[3].shape[0]

    def y_spec(e, par):
        return pl.BlockSpec((RT_ROWS, D_MODEL),
                            lambda i, t, b, f, s: (b[i * RT_SPECS + e * 2 + par], 0))

    grid_spec = pltpu.PrefetchScalarGridSpec(
        num_scalar_prefetch=4,
        grid=(n_steps,),
        in_specs=[
            pl.BlockSpec((N_EXPERTS, RT_TOK), lambda i, t, b, f, s: (0, t[i])),
            pl.BlockSpec((RT_TOK, D_MODEL), lambda i, t, b, f, s: (t[i], 0)),
            pl.BlockSpec((1, D_MODEL), lambda i, t, b, f, s: (0, 0)),
        ] + [y_spec(e, par) for e in range(N_EXPERTS) for par in range(2)],
        out_specs=pl.BlockSpec((RT_TOK, D_MODEL), lambda i, t, b, f, s: (t[i], 0)),
        scratch_shapes=[pltpu.VMEM((RT_TOK, D_MODEL), F32)],
    )
    return pl.pallas_call(
        _combine2_kernel,
        grid_spec=grid_spec,
        out_shape=jax.ShapeDtypeStruct((n, D_MODEL), F32),
        compiler_params=_cparams(("arbitrary",)),
        name="moe_combine",
    )(*steps, slot, h, ln_f, *[ys[e] for e in range(N_EXPERTS) for _ in range(2)])

def _prep_in_weights(w_in, a_log_f, a_log_b, dt_bias_f, dt_bias_b):
    offs = np.cumsum([0, A_QK, A_QK, A_V, A_V, A_HEADS, A_HEADS, A_HEADS, A_HEADS,
                      B_QK, B_QK, B_V, B_V, GLA_LOWRANK, GLA_LOWRANK])
    seg = [w_in[:, offs[i]:offs[i + 1]] for i in range(14)]
    w_main = jnp.concatenate(seg[0:4] + seg[8:12], axis=1).astype(BF16)
    w_small = jnp.concatenate(seg[4:8] + seg[12:14], axis=1)
    w_small = jnp.pad(w_small, ((0, 0), (0, SMALL_WIDTH - w_small.shape[1]))).astype(BF16)
    pad = SMALL_WIDTH - 4 * A_HEADS
    z = jnp.zeros((2 * A_HEADS,), F32)
    a_row = jnp.concatenate([z, a_log_f.astype(F32), a_log_b.astype(F32), jnp.zeros((pad,), F32)])
    dt_row = jnp.concatenate([z, dt_bias_f.astype(F32), dt_bias_b.astype(F32), jnp.zeros((pad,), F32)])
    small_params = jnp.zeros((8, SMALL_WIDTH), F32).at[0].set(a_row).at[1].set(dt_row)
    return w_main, w_small, small_params


def kernel(x_prompt, x_sample, ln1, w_in, conv_w, a_log_f, a_log_b, dt_bias_f, dt_bias_b, norm_a, gla_w_f, gla_b_f, gla_w_b, gla_b_b, norm_b, w_out, ln2, w_router, w1, w3, w2, ln_f):
    w_main, w_small, small_params = _prep_in_weights(w_in[0], a_log_f[0], a_log_b[0],
                                                     dt_bias_f[0], dt_bias_b[0])
    conv8 = _conv_layout(conv_w[0])
    wg, bg = _gla_gate_weights(gla_w_f[0], gla_b_f[0], gla_w_b[0], gla_b_b[0])
    wl = jnp.asarray(_gla_level_weights(), BF16)
    wo = w_out[0].astype(BF16)
    wr_t = w_router[0].astype(F32).T
    wrh = wr_t.astype(BF16)
    wrl = (wr_t - wrh.astype(F32)).astype(BF16)
    wr_pad = jnp.pad(w_router[0].astype(F32), ((0, 0), (0, GATE_LANES - N_EXPERTS)))
    wrh2 = wr_pad.astype(BF16)
    wrl2 = (wr_pad - wrh2.astype(F32)).astype(BF16)
    w1b, w3b, w2b = w1[0].astype(BF16), w3[0].astype(BF16), w2[0].astype(BF16)
    row = lambda v: v.astype(F32).reshape(1, -1)
    outs = []
    for x in (x_prompt, x_sample):
        bsz, seq, _ = x.shape
        n = bsz * seq
        cap = EC_CAPACITY * n // N_EXPERTS
        x2d = x.reshape(n, D_MODEL)
        main, small = _in_proj(x2d, row(ln1[0]), w_main, w_small, small_params, tm=1024, tn=1024)
        main3 = main.reshape(bsz, seq, MAIN_WIDTH)
        small3 = small.reshape(bsz, seq, SMALL_WIDTH)
        oaf, oab = _gdn(_qkv_conv(main3, conv8, rows=SCAN_ROWS), _scal_layout(small3))
        obf, obb = _gla(main3, small3, wg, bg, wl)
        flat = lambda o: o.reshape(n, -1)
        h, hn, probs_t, p3 = _out_proj(flat(oaf), flat(oab), flat(obf), flat(obb), main, x2d, wo,
                                       row(norm_a[0]), row(norm_b[0]), row(ln2[0]),
                                       wrh, wrl, wrh2, wrl2, tm=256)
        slot, ts = _topk(probs_t, cap)
        steps = _route_steps(ts, cap)
        xs = _dispatch(hn, p3, slot, steps, cap)
        ys = [_expert_mlp(xs[2 * e], xs[2 * e + 1], w1b, w3b, w2b, e, cap) for e in range(N_EXPERTS)]
        out = _combine2(ys, slot, steps, h, row(ln_f))
        outs.append(out.reshape(bsz, seq, D_MODEL))
    return tuple(outs)
```

```python
import functools

import jax
import jax.numpy as jnp
import numpy as np
from jax import lax
from jax.experimental import pallas as pl
from jax.experimental.pallas import tpu as pltpu

F32 = jnp.float32
BF16 = jnp.bfloat16

D_MODEL = 2048
A_HEADS, A_DK, A_DV = 8, 128, 128
B_HEADS, B_DK, B_DV = 4, 128, 256
GLA_LOWRANK = 16
GLA_GATE_NORM = 16.0
CONV_K = 5
CHUNK = 64
N_EXPERTS = 16
EC_CAPACITY = 2
EXPERT_FF = D_MODEL // 2
EPS = 1e-6

A_QK = A_HEADS * A_DK
A_V = A_HEADS * A_DV
B_QK = B_HEADS * B_DK
B_V = B_HEADS * B_DV
MAIN_WIDTH = 2 * A_QK + 2 * A_V + 2 * B_QK + 2 * B_V
SMALL_WIDTH = 128
LR_OFF = 4 * A_HEADS
V7X_VMEM_LIMIT = 56 * 1024 * 1024
CONV_HALO = 16
SCAN_ROWS = 256
CPS = SCAN_ROWS // CHUNK
GLA_LEVELS = 6
GATE_LANES = 128
GDN_HEADS_PER_PASS = 8


def _cparams(sem):
    return pltpu.CompilerParams(dimension_semantics=sem, vmem_limit_bytes=V7X_VMEM_LIMIT)


def _dot(a, b):
    return jnp.dot(a.astype(BF16), b.astype(BF16), preferred_element_type=F32)


def _dot_nt(a, b):
    return lax.dot_general(a.astype(BF16), b.astype(BF16), (((1,), (1,)), ((), ())),
                           preferred_element_type=F32)


def _dot_tn(a, b):
    return lax.dot_general(a.astype(BF16), b.astype(BF16), (((0,), (0,)), ((), ())),
                           preferred_element_type=F32)


def _silu(x):
    return x * (1.0 / (1.0 + jnp.exp(-x)))


def _in_proj_kernel(x_ref, ln_ref, wm_ref, ws_ref, sp_ref, main_ref, small_ref, hn_ref):
    j = pl.program_id(1)

    @pl.when(j == 0)
    def _():
        xf = x_ref[...]
        y = xf * lax.rsqrt(jnp.mean(xf * xf, axis=-1, keepdims=True) + EPS) * ln_ref[...]
        hn = y.astype(BF16)
        hn_ref[...] = hn
        s = jnp.dot(hn, ws_ref[...], preferred_element_type=F32)
        lane = lax.broadcasted_iota(jnp.int32, s.shape, 1)
        neg_a = -jnp.exp(sp_ref[0:1, :])
        z = s + sp_ref[1:2, :]
        softplus = jnp.maximum(z, 0.0) + jnp.log(1.0 + jnp.exp(-jnp.abs(z)))
        sig = 1.0 / (1.0 + jnp.exp(-s))
        small_ref[...] = jnp.where(lane < 2 * A_HEADS, sig,
                                   jnp.where(lane < 4 * A_HEADS, neg_a * softplus, s))

    main_ref[...] = jnp.dot(hn_ref[...], wm_ref[...], preferred_element_type=F32).astype(BF16)


def _in_proj(x2d, ln1, w_main, w_small, small_params, tm, tn):
    t = x2d.shape[0]
    return pl.pallas_call(
        _in_proj_kernel,
        grid=(t // tm, MAIN_WIDTH // tn),
        in_specs=[
            pl.BlockSpec((tm, D_MODEL), lambda i, j: (i, 0)),
            pl.BlockSpec((1, D_MODEL), lambda i, j: (0, 0)),
            pl.BlockSpec((D_MODEL, tn), lambda i, j: (0, j)),
            pl.BlockSpec((D_MODEL, SMALL_WIDTH), lambda i, j: (0, 0)),
            pl.BlockSpec((8, SMALL_WIDTH), lambda i, j: (0, 0)),
        ],
        out_specs=[
            pl.BlockSpec((tm, tn), lambda i, j: (i, j)),
            pl.BlockSpec((tm, SMALL_WIDTH), lambda i, j: (i, 0)),
        ],
        out_shape=[
            jax.ShapeDtypeStruct((t, MAIN_WIDTH), BF16),
            jax.ShapeDtypeStruct((t, SMALL_WIDTH), F32),
        ],
        scratch_shapes=[pltpu.VMEM((tm, D_MODEL), BF16)],
        compiler_params=_cparams(("arbitrary", "arbitrary")),
        name="in_proj",
    )(x2d, ln1, w_main, w_small, small_params)


def _chunk_iotas():
    ii = lax.broadcasted_iota(jnp.int32, (CHUNK, CHUNK), 0)
    jj = lax.broadcasted_iota(jnp.int32, (CHUNK, CHUNK), 1)
    return ii, jj


def _row_to_col(row, eye):
    return jnp.sum(jnp.where(eye, row, 0.0), axis=1, keepdims=True)


def _col_to_row(col, eye):
    return jnp.sum(jnp.where(eye, col, 0.0), axis=0, keepdims=True)


def _l2norm(x):
    return x * lax.rsqrt(jnp.sum(x * x, axis=-1, keepdims=True) + EPS)


def _delta_chunks(qs, ks, vs, beta_rows, g_rows, states, revs):
    ii, jj = _chunk_iotas()
    eye = ii == jj
    nc = range(len(qs))
    incl = [(ii <= jj) if r else (ii >= jj) for r in revs]
    strict = [(ii < jj) if r else (ii > jj) for r in revs]
    gc_col = [jnp.sum(jnp.where(incl[c], g_rows[c], 0.0), axis=1, keepdims=True) for c in nc]
    gc_row = [_col_to_row(gc_col[c], eye) for c in nc]
    beta_col = [_row_to_col(beta_rows[c], eye) for c in nc]
    g_tot = [jnp.sum(g_rows[c], axis=1, keepdims=True) for c in nc]
    decay = [jnp.where(incl[c], jnp.exp(jnp.where(incl[c], gc_col[c] - gc_row[c], 0.0)), 0.0)
             for c in nc]
    kb = [ks[c] * beta_col[c] for c in nc]
    kk = [_dot_nt(kb[c], ks[c]) for c in nc]
    qk = [_dot_nt(qs[c], ks[c]) for c in nc]
    p = [jnp.where(strict[c], -kk[c] * decay[c], 0.0) for c in nc]
    toff = p
    for _ in range(5):
        p = [_dot(p[c], p[c]) for c in nc]
        tp = [_dot(toff[c], p[c]) for c in nc]
        toff = [toff[c] + p[c] + tp[c] for c in nc]
    e_gc = [jnp.exp(gc_col[c]) for c in nc]
    rhs = [jnp.concatenate([vs[c] * beta_col[c], kb[c] * e_gc[c]], axis=1) for c in nc]
    sol = [rhs[c] + _dot(toff[c], rhs[c]) for c in nc]
    attn = [jnp.where(incl[c], qk[c] * decay[c], 0.0) for c in nc]
    ws = [_dot(sol[c][:, A_DV:], states[c]) for c in nc]
    qs_state = [_dot(qs[c] * e_gc[c], states[c]) for c in nc]
    v_new = [sol[c][:, :A_DV] - ws[c] for c in nc]
    av = [_dot(attn[c], v_new[c]) for c in nc]
    kv = [_dot_tn(ks[c] * jnp.exp(g_tot[c] - gc_col[c]), v_new[c]) for c in nc]
    outs = [qs_state[c] + av[c] for c in nc]
    new_states = [states[c] * jnp.exp(g_tot[c]) + kv[c] for c in nc]
    return outs, new_states


def _qkv_conv_kernel(blk, prv, nxt, cw_ref, out_ref, pad_ref):
    n = pl.program_id(1)
    nb = pl.num_programs(1)
    rows = blk.shape[1]
    pad_ref[0:CONV_HALO, :] = jnp.where(n == 0, 0.0, prv[0].astype(F32))
    pad_ref[CONV_HALO:CONV_HALO + rows, :] = blk[0].astype(F32)
    pad_ref[CONV_HALO + rows:, :] = jnp.where(n == nb - 1, 0.0, nxt[0].astype(F32))
    lo = CONV_HALO - CONV_K // 2
    for g in range((2 * A_QK + A_V) // A_DK):
        cols = slice(g * A_DK, (g + 1) * A_DK)
        w = cw_ref[:, cols]
        acc = w[0:1, :] * pad_ref[lo:lo + rows, cols]
        for j in range(1, CONV_K):
            acc = acc + w[j:j + 1, :] * pad_ref[lo + j:lo + j + rows, cols]
        y = _silu(acc)
        if g < A_HEADS:
            y = _l2norm(y) * (A_DK ** -0.5)
        elif g < 2 * A_HEADS:
            y = _l2norm(y)
        out_ref[0, :, cols] = y.astype(out_ref.dtype)


def _qkv_conv(main3, conv8, rows):
    bsz, seq, _ = main3.shape
    width = 2 * A_QK + A_V
    hb = rows // CONV_HALO
    nhalo = seq // CONV_HALO
    return pl.pallas_call(
        _qkv_conv_kernel,
        grid=(bsz, seq // rows),
        in_specs=[
            pl.BlockSpec((1, rows, width), lambda b, n: (b, n, 0)),
            pl.BlockSpec((1, CONV_HALO, width), lambda b, n: (b, jnp.maximum(n * hb - 1, 0), 0)),
            pl.BlockSpec((1, CONV_HALO, width),
                         lambda b, n: (b, jnp.minimum((n + 1) * hb, nhalo - 1), 0)),
            pl.BlockSpec((8, width), lambda b, n: (0, 0)),
        ],
        out_specs=pl.BlockSpec((1, rows, width), lambda b, n: (b, n, 0)),
        out_shape=jax.ShapeDtypeStruct((bsz, seq, width), BF16),
        scratch_shapes=[pltpu.VMEM((rows + 2 * CONV_HALO, width), F32)],
        compiler_params=_cparams(("arbitrary", "arbitrary")),
        name="qkv_conv",
    )(main3, main3, main3, conv8)


def _gdn_kernel(blk_f, blk_b, sc_f, sc_b, of_ref, ob_ref, st_ref):
    n = pl.program_id(1)

    @pl.when(n == 0)
    def _():
        st_ref[...] = jnp.zeros_like(st_ref)

    def load(d, row0, a, h):
        blk = blk_f if d == 0 else blk_b
        col = a * A_QK + h * A_DK
        return blk[0, pl.ds(row0, CHUNK), col:col + A_DK].astype(F32)

    def body(s, carry):
        for h0 in range(0, A_HEADS, GDN_HEADS_PER_PASS):
            chains = []
            for d, sc_ref in enumerate((sc_f, sc_b)):
                sc = s if d == 0 else CPS - 1 - s
                row0 = pl.multiple_of(sc * CHUNK, CHUNK)
                for h in range(h0, h0 + GDN_HEADS_PER_PASS):
                    chains.append((d, h, sc, row0, sc_ref))
            qs = [load(d, row0, 0, h) for d, h, sc, row0, _ in chains]
            ks = [load(d, row0, 1, h) for d, h, sc, row0, _ in chains]
            vs = [load(d, row0, 2, h) for d, h, sc, row0, _ in chains]
            betas = [r[0, 0, d, h, pl.ds(sc, 1), :] for d, h, sc, row0, r in chains]
            gs = [r[0, 0, 2 + d, h, pl.ds(sc, 1), :] for d, h, sc, row0, r in chains]
            states = [st_ref[d, h] for d, h, sc, row0, _ in chains]
            outs, new_states = _delta_chunks(qs, ks, vs, betas, gs, states,
                                             [d == 1 for d, *_ in chains])
            for (d, h, sc, row0, _), o, st in zip(chains, outs, new_states):
                st_ref[d, h] = st
                out_ref = of_ref if d == 0 else ob_ref
                out_ref[0, pl.ds(row0, CHUNK), h * A_DV:(h + 1) * A_DV] = o.astype(out_ref.dtype)
        return carry

    lax.fori_loop(0, CPS, body, 0)


def _scal_layout(small):
    bsz, seq = small.shape[:2]
    s = small[..., :4 * A_HEADS].reshape(bsz, seq // SCAN_ROWS, CPS, CHUNK, 4, A_HEADS)
    return jnp.transpose(s, (0, 1, 4, 5, 2, 3))


def _conv_layout(conv_w):
    return jnp.pad(conv_w.astype(F32), ((0, 8 - CONV_K), (0, 0)))


def _gdn(qkv, scal):
    bsz, seq, width = qkv.shape
    nb = seq // SCAN_ROWS

    def specs(blk_of):
        return (pl.BlockSpec((1, SCAN_ROWS, width), lambda b, n: (b, blk_of(n), 0)),
                pl.BlockSpec((1, 1, 4, A_HEADS, CPS, CHUNK), lambda b, n: (b, blk_of(n), 0, 0, 0, 0)))

    qf, sf = specs(lambda n: n)
    qb, sb = specs(lambda n: nb - 1 - n)
    out_sds = jax.ShapeDtypeStruct((bsz, seq, A_V), BF16)
    return pl.pallas_call(
        _gdn_kernel,
        grid=(bsz, nb),
        in_specs=[qf, qb, sf, sb],
        out_specs=[
            pl.BlockSpec((1, SCAN_ROWS, A_V), lambda b, n: (b, n, 0)),
            pl.BlockSpec((1, SCAN_ROWS, A_V), lambda b, n: (b, nb - 1 - n, 0)),
        ],
        out_shape=[out_sds, out_sds],
        scratch_shapes=[pltpu.VMEM((2, A_HEADS, A_DK, A_DV), F32)],
        compiler_params=_cparams(("arbitrary", "arbitrary")),
        name="gdn",
    )(qkv, qkv, scal, scal)


GLA_MM_LEVELS = (4, 5)


def _gla_level_weights():
    c = CHUNK
    w = np.zeros((2, 3 * c, c), np.float32)
    for d in range(2):
        for blk, l in enumerate(GLA_MM_LEVELS):
            s = c >> (l + 1)
            for i in range(c):
                mid = (i // (2 * s)) * 2 * s + s
                if d == 0:
                    ts = range(mid, i + 1) if i >= mid else range(i + 1, mid)
                else:
                    ts = range(i, mid) if i < mid else range(mid, i)
                for t in ts:
                    w[d, blk * c + i, t] = 1.0
        for i in range(c):
            for t in range(c):
                before = t <= i if d == 0 else t >= i
                w[d, 2 * c + i, t] = 1.0 if before else 0.0
    return w


def _gla_kernel(qk_f, v_f, sm_f, qk_b, v_b, sm_b, wg_ref, bg_ref, wl_ref, of_ref, ob_ref, st_ref):
    n = pl.program_id(1)

    @pl.when(n == 0)
    def _():
        st_ref[...] = jnp.zeros_like(st_ref)

    ii, jj = _chunk_iotas()
    eye = ii == jj
    tok = lax.broadcasted_iota(jnp.int32, (CHUNK, 1), 0)

    streams = ((qk_f, v_f, sm_f, of_ref), (qk_b, v_b, sm_b, ob_ref))
    dirs = range(2)
    chains = [(d, h) for d in dirs for h in range(B_HEADS)]
    masks = [ii // (CHUNK >> l) == jj // (CHUNK >> l) for l in range(GLA_LEVELS)]

    def body(s, carry):
        rows = [pl.ds(pl.multiple_of((s if d == 0 else CPS - 1 - s) * CHUNK, CHUNK), CHUNK)
                for d in dirs]
        z = [_dot(streams[d][2][0, rows[d], :], wg_ref[d]) + bg_ref[d] for d in dirs]
        la = [(jnp.minimum(z[d], 0.0) - jnp.log(1.0 + jnp.exp(-jnp.abs(z[d]))))
              * (1.0 / GLA_GATE_NORM) for d in dirs]
        hi = [la[d].astype(BF16) for d in dirs]
        r1 = [la[d] - hi[d].astype(F32) for d in dirs]
        mid = [r1[d].astype(BF16) for d in dirs]
        low = [(r1[d] - mid[d].astype(F32)).astype(BF16) for d in dirs]
        ex = [jnp.dot(wl_ref[d], hi[d], preferred_element_type=F32)
              + jnp.dot(wl_ref[d], mid[d], preferred_element_type=F32)
              + jnp.dot(wl_ref[d], low[d], preferred_element_type=F32) for d in dirs]
        bcum = [ex[d][2 * CHUNK:] for d in dirs]
        b_tot = [bcum[0][CHUNK - 1:CHUNK], bcum[1][0:1]]
        q_all = [streams[d][0][0, rows[d], :B_QK].astype(F32) * (B_DK ** -0.5) for d in dirs]
        k_all = [streams[d][0][0, rows[d], B_QK:].astype(F32) for d in dirs]
        qs, ks = [], []
        for l in range(GLA_LEVELS):
            half = CHUNK >> (l + 1)
            right = (tok // half) % 2 == 1
            q_side = [right, jnp.logical_not(right)]
            if l in GLA_MM_LEVELS:
                blk = GLA_MM_LEVELS.index(l)
                expo = [ex[d][blk * CHUNK:(blk + 1) * CHUNK] for d in dirs]
            else:
                expo = []
                for d in dirs:
                    ref = jnp.concatenate(
                        [jnp.broadcast_to(bcum[d][g + half - 1 + d:g + half + d], (2 * half, B_QK))
                         for g in range(0, CHUNK, 2 * half)], axis=0)
                    expo.append(jnp.where(q_side[d], bcum[d] - ref, ref - bcum[d]))
            e_l = [jnp.exp(expo[d]) for d in dirs]
            qs.append([jnp.where(q_side[d], q_all[d] * e_l[d], 0.0).astype(BF16) for d in dirs])
            ks.append([jnp.where(q_side[d], 0.0, k_all[d] * e_l[d]).astype(BF16) for d in dirs])
        q_dec = [(q_all[d] * jnp.exp(bcum[d])).astype(BF16) for d in dirs]
        k_dec = [(k_all[d] * jnp.exp(b_tot[d] - bcum[d])).astype(BF16) for d in dirs]
        e_tot = [jnp.exp(b_tot[d]) for d in dirs]
        ck = [slice(h * B_DK, (h + 1) * B_DK) for h in range(B_HEADS)]
        cv = [slice(h * B_DV, (h + 1) * B_DV) for h in range(B_HEADS)]
        attn = [jnp.where(eye, _dot_nt(q_all[d][:, ck[h]], k_all[d][:, ck[h]]), 0.0)
                for d, h in chains]
        for l in range(GLA_LEVELS):
            part = [_dot_nt(qs[l][d][:, ck[h]], ks[l][d][:, ck[h]]) for d, h in chains]
            attn = [attn[c] + jnp.where(masks[l], part[c], 0.0) for c in range(len(chains))]
        v = [streams[d][1][0, rows[d], cv[h]] for d, h in chains]
        st = [st_ref[d, h] for d, h in chains]
        o_state = [_dot_nt(q_dec[d][:, ck[h]], st[c]) for c, (d, h) in enumerate(chains)]
        o_local = [_dot(attn[c], v[c]) for c in range(len(chains))]
        kv = [_dot_tn(v[c], k_dec[d][:, ck[h]]) for c, (d, h) in enumerate(chains)]
        for c, (d, h) in enumerate(chains):
            st_ref[d, h] = st[c] * e_tot[d][:, ck[h]] + kv[c]
            streams[d][3][0, rows[d], cv[h]] = (o_state[c] + o_local[c]).astype(of_ref.dtype)
        return carry

    lax.fori_loop(0, CPS, body, 0)


def _gla(main3, small3, wg, bg, wl):
    bsz, seq, _ = main3.shape
    nb = seq // SCAN_ROWS
    qk_blk = (2 * A_QK + 2 * A_V) // (2 * B_QK)
    v_blk = (2 * A_QK + 2 * A_V + 2 * B_QK) // B_V

    def stream(blk_of):
        return [
            pl.BlockSpec((1, SCAN_ROWS, 2 * B_QK), lambda b, n: (b, blk_of(n), qk_blk)),
            pl.BlockSpec((1, SCAN_ROWS, B_V), lambda b, n: (b, blk_of(n), v_blk)),
            pl.BlockSpec((1, SCAN_ROWS, SMALL_WIDTH), lambda b, n: (b, blk_of(n), 0)),
        ]

    out_sds = jax.ShapeDtypeStruct((bsz, seq, B_V), BF16)
    return pl.pallas_call(
        _gla_kernel,
        grid=(bsz, nb),
        in_specs=stream(lambda n: n) + stream(lambda n: nb - 1 - n) + [
            pl.BlockSpec((2, SMALL_WIDTH, B_QK), lambda b, n: (0, 0, 0)),
            pl.BlockSpec((2, 1, B_QK), lambda b, n: (0, 0, 0)),
            pl.BlockSpec((2, 3 * CHUNK, CHUNK), lambda b, n: (0, 0, 0)),
        ],
        out_specs=[
            pl.BlockSpec((1, SCAN_ROWS, B_V), lambda b, n: (b, n, 0)),
            pl.BlockSpec((1, SCAN_ROWS, B_V), lambda b, n: (b, nb - 1 - n, 0)),
        ],
        out_shape=[out_sds, out_sds],
        scratch_shapes=[pltpu.VMEM((2, B_HEADS, B_DV, B_DK), F32)],
        compiler_params=_cparams(("arbitrary", "arbitrary")),
        name="gla",
    )(main3, main3, small3, main3, main3, small3, wg, bg, wl)


def _gla_gate_weights(gla_w_f, gla_b_f, gla_w_b, gla_b_b):
    wg = jnp.zeros((2, SMALL_WIDTH, B_QK), F32)
    wg = wg.at[0, LR_OFF:LR_OFF + GLA_LOWRANK].set(gla_w_f.astype(F32))
    wg = wg.at[1, LR_OFF + GLA_LOWRANK:LR_OFF + 2 * GLA_LOWRANK].set(gla_w_b.astype(F32))
    bg = jnp.stack([gla_b_f, gla_b_b]).astype(F32).reshape(2, 1, B_QK)
    return wg.astype(BF16), bg


def _out_proj_kernel(oaf, oab, obf, obb, ga, gb, x_ref, wo_ref, na_ref, nb_ref, ln2_ref,
                     wrh_ref, wrl_ref, wrh2_ref, wrl2_ref, h_ref, hn_ref, pt_ref, p3_ref, mix_ref):
    def head_norm(of_ref, ob_ref, g_ref, w_ref, width, heads, base):
        for h in range(heads):
            c = slice(h * width, (h + 1) * width)
            o = of_ref[:, c].astype(F32) + ob_ref[:, c].astype(F32)
            y = o * lax.rsqrt(jnp.mean(o * o, axis=-1, keepdims=True) + EPS) * w_ref[...]
            mix_ref[:, base + h * width:base + (h + 1) * width] = (
                y * _silu(g_ref[:, c].astype(F32))).astype(BF16)

    head_norm(oaf, oab, ga, na_ref, A_DV, A_HEADS, 0)
    head_norm(obf, obb, gb, nb_ref, B_DV, B_HEADS, A_V)
    hres = x_ref[...] + jnp.dot(mix_ref[...], wo_ref[...], preferred_element_type=F32)
    h_ref[...] = hres
    hn = hres * lax.rsqrt(jnp.mean(hres * hres, axis=-1, keepdims=True) + EPS) * ln2_ref[...]
    hi = hn.astype(BF16)
    hn_ref[...] = hi
    lo = (hn - hi.astype(F32)).astype(BF16)
    lt = _dot_nt(wrh_ref[...], hi) + _dot_nt(wrl_ref[...], hi) + _dot_nt(wrh_ref[...], lo)
    e = jnp.exp(lt - jnp.max(lt, axis=0, keepdims=True))
    pt_ref[...] = e / jnp.sum(e, axis=0, keepdims=True)
    l2 = _dot(hi, wrh2_ref[...]) + _dot(hi, wrl2_ref[...]) + _dot(lo, wrh2_ref[...])
    lane = lax.broadcasted_iota(jnp.int32, l2.shape, 1)
    l2 = jnp.where(lane < N_EXPERTS, l2, -jnp.inf)
    e2 = jnp.exp(l2 - jnp.max(l2, axis=1, keepdims=True))
    p2 = e2 / jnp.sum(e2, axis=1, keepdims=True)
    g_hi = p2.astype(BF16)
    r1 = p2 - g_hi.astype(F32)
    g_mid = r1.astype(BF16)
    g_lo = (r1 - g_mid.astype(F32)).astype(BF16)
    p3 = (g_hi.astype(F32) + pltpu.roll(g_mid.astype(F32), N_EXPERTS, 1)
          + pltpu.roll(g_lo.astype(F32), 2 * N_EXPERTS, 1))
    p3_ref[...] = p3.astype(BF16)


def _out_proj(oaf, oab, obf, obb, main, x2d, wo, na, nb, ln2, wrh, wrl, wrh2, wrl2, tm):
    t = x2d.shape[0]
    ga_blk = (2 * A_QK + A_V) // A_V
    gb_blk = (2 * A_QK + 2 * A_V + 2 * B_QK + B_V) // B_V
    row = lambda w: pl.BlockSpec((tm, w), lambda i: (i, 0))
    full = lambda a, b: pl.BlockSpec((a, b), lambda i: (0, 0))
    return pl.pallas_call(
        _out_proj_kernel,
        grid=(t // tm,),
        in_specs=[
            row(A_V), row(A_V), row(B_V), row(B_V),
            pl.BlockSpec((tm, A_V), lambda i: (i, ga_blk)),
            pl.BlockSpec((tm, B_V), lambda i: (i, gb_blk)),
            row(D_MODEL), full(D_MODEL, D_MODEL), full(1, A_DV), full(1, B_DV), full(1, D_MODEL),
            full(N_EXPERTS, D_MODEL), full(N_EXPERTS, D_MODEL),
            full(D_MODEL, GATE_LANES), full(D_MODEL, GATE_LANES),
        ],
        out_specs=[row(D_MODEL), row(D_MODEL), pl.BlockSpec((N_EXPERTS, tm), lambda i: (0, i)),
                   row(GATE_LANES)],
        out_shape=[
            jax.ShapeDtypeStruct((t, D_MODEL), F32),
            jax.ShapeDtypeStruct((t, D_MODEL), BF16),
            jax.ShapeDtypeStruct((N_EXPERTS, t), F32),
            jax.ShapeDtypeStruct((t, GATE_LANES), BF16),
        ],
        scratch_shapes=[pltpu.VMEM((tm, D_MODEL), BF16)],
        compiler_params=_cparams(("arbitrary",)),
        name="out_proj",
    )(oaf, oab, obf, obb, main, main, x2d, wo, na, nb, ln2, wrh, wrl, wrh2, wrl2)


TOPK_LANES = 256


def _topk_kernel(p_ref, slot_ref, ts_ref, *, cap):
    n = p_ref.shape[1]
    n_steps = n // TOPK_LANES
    bits = pltpu.bitcast(p_ref[...], jnp.int32)

    def bisect(i, thr):
        cand = thr | jnp.left_shift(jnp.int32(1), 30 - i)
        cnt = jnp.sum(jnp.where(bits >= cand, 1.0, 0.0), axis=1, keepdims=True)
        return jnp.where(cnt >= cap, cand, thr)

    thr = lax.fori_loop(0, 31, bisect, jnp.zeros((N_EXPERTS, 1), jnp.int32))
    need = cap - jnp.sum(jnp.where(bits > thr, 1.0, 0.0), axis=1, keepdims=True)
    r = lax.broadcasted_iota(jnp.int32, (TOPK_LANES, TOPK_LANES), 0)
    c = lax.broadcasted_iota(jnp.int32, (TOPK_LANES, TOPK_LANES), 1)
    tri = jnp.where(r <= c, 1.0, 0.0).astype(BF16)
    step_lane = lax.broadcasted_iota(jnp.int32, (N_EXPERTS, n_steps), 1)

    ts_ref[...] = jnp.zeros_like(ts_ref)

    def step(j, carry):
        ties_before, sel_before = carry
        lanes = pl.ds(pl.multiple_of(j * TOPK_LANES, TOPK_LANES), TOPK_LANES)
        pb = pltpu.bitcast(p_ref[:, lanes], jnp.int32)
        tie = jnp.where(pb == thr, 1.0, 0.0)
        tie_incl = jnp.dot(tie.astype(BF16), tri, preferred_element_type=F32)
        sel = jnp.logical_or(pb > thr, jnp.logical_and(pb == thr, ties_before + tie_incl - tie < need))
        self_f = jnp.where(sel, 1.0, 0.0)
        sel_incl = jnp.dot(self_f.astype(BF16), tri, preferred_element_type=F32)
        pos = sel_before + sel_incl - self_f
        slot_ref[:, lanes] = jnp.where(sel, pos, -1.0).astype(jnp.int32)
        ts_ref[...] = jnp.where(step_lane == j, sel_before.astype(jnp.int32), ts_ref[...])
        return (ties_before + tie_incl[:, TOPK_LANES - 1:],
                sel_before + sel_incl[:, TOPK_LANES - 1:])

    zero = jnp.zeros((N_EXPERTS, 1), F32)
    lax.fori_loop(0, n_steps, step, (zero, zero))


def _topk(probs_t, cap):
    n = probs_t.shape[1]
    return pl.pallas_call(
        functools.partial(_topk_kernel, cap=cap),
        out_shape=[
            jax.ShapeDtypeStruct((N_EXPERTS, n), jnp.int32),
            jax.ShapeDtypeStruct((N_EXPERTS, n // TOPK_LANES), jnp.int32),
        ],
        compiler_params=pltpu.CompilerParams(vmem_limit_bytes=V7X_VMEM_LIMIT),
        name="topk",
    )(probs_t)


MOE_TILE = 1024
MOE_SUB = TOPK_LANES
MOE_SB = 512
MOE_RB = 128
CMB_TILE = 512
CMB_YB = 256
FL_VALID, FL_FIRST, FL_LAST = 1, 2, 4


def _moe_kernel(it_e, it_tile, it_g, it_s0, it_fl, it_ts, x_ref, slot_ref, p_ref,
                w1_ref, w3_ref, w2_ref, y_ref, xs_ref, g_ref):
    i = pl.program_id(0)
    fl = it_fl[i]
    e = it_e[i]
    s0 = it_s0[i]

    @pl.when((fl & FL_FIRST) != 0)
    def _():
        xs_ref[...] = jnp.zeros_like(xs_ref)
        g_ref[...] = jnp.zeros_like(g_ref)

    @pl.when((fl & FL_VALID) != 0)
    def _():
        srow = slot_ref[pl.ds(e, 1), :]
        prow = p_ref[pl.ds(e, 1), :]
        iota_s = lax.broadcasted_iota(jnp.int32, (MOE_RB, MOE_SUB), 0)
        for q in range(MOE_TILE // MOE_SUB):
            lo = jnp.maximum(it_ts[i * 5 + q], s0)
            hi = jnp.minimum(it_ts[i * 5 + q + 1], s0 + MOE_SB)

            @pl.when(hi > lo)
            def _():
                sq = srow[:, q * MOE_SUB:(q + 1) * MOE_SUB]
                pq = prow[:, q * MOE_SUB:(q + 1) * MOE_SUB]
                xq = x_ref[q * MOE_SUB:(q + 1) * MOE_SUB, :]

                def rows(r, carry):
                    roff = pl.multiple_of(r * MOE_RB, MOE_RB)
                    oh = (iota_s + (s0 + roff)) == sq
                    xs_ref[pl.ds(roff, MOE_RB), :] += jnp.dot(
                        jnp.where(oh, 1.0, 0.0).astype(BF16), xq, preferred_element_type=F32)
                    g_ref[pl.ds(roff, MOE_RB), :] += jnp.sum(jnp.where(oh, pq, 0.0), axis=1,
                                                            keepdims=True)
                    return carry

                lax.fori_loop((lo - s0) // MOE_RB, (hi - 1 - s0) // MOE_RB + 1, rows, 0)

    @pl.when((fl & FL_LAST) != 0)
    def _():
        xs = xs_ref[...].astype(BF16)
        h1 = jnp.dot(xs, w1_ref[0], preferred_element_type=F32)
        h3 = jnp.dot(xs, w3_ref[0], preferred_element_type=F32)
        hid = (_silu(h1) * h3).astype(BF16)
        y = jnp.dot(hid, w2_ref[0], preferred_element_type=F32) * g_ref[...]
        y_ref[...] = y.astype(y_ref.dtype)


def _moe_items(ts, cap):
    n_grp = ts.shape[1]
    per = MOE_TILE // MOE_SUB
    n_tile = n_grp // per
    n_sb = cap // MOE_SB
    tse = jnp.concatenate([ts, jnp.full((N_EXPERTS, 1), cap, jnp.int32)], axis=1)
    a = tse[:, 0:n_grp:per]
    b = tse[:, per::per]
    kf = a // MOE_SB
    cnt = jnp.where(b > a, (b - 1) // MOE_SB - kf + 1, 0).reshape(-1)
    incl = jnp.cumsum(cnt)
    total = incl[-1]
    n_items = N_EXPERTS * (n_tile + n_sb)
    idx = jnp.arange(n_items, dtype=jnp.int32)
    valid = idx < total
    idc = jnp.minimum(idx, total - 1)
    pair = jnp.searchsorted(incl, idc, side="right").astype(jnp.int32)
    r = idc - (incl[pair] - cnt[pair])
    e = pair // n_tile
    j = pair % n_tile
    k = kf.reshape(-1)[pair] + r
    s0 = k * MOE_SB
    aa = a.reshape(-1)[pair]
    bb = b.reshape(-1)[pair]
    first = jnp.logical_and(aa <= s0, s0 < bb)
    last = jnp.logical_and(aa <= s0 + MOE_SB - 1, s0 + MOE_SB - 1 < bb)
    fl = jnp.where(valid, FL_VALID + FL_FIRST * first + FL_LAST * last, 0).astype(jnp.int32)
    tsi = jnp.stack([tse[e, j * per + q] for q in range(per + 1)], axis=1).reshape(-1)
    return (e.astype(jnp.int32), j.astype(jnp.int32), (e * n_sb + k).astype(jnp.int32),
            s0.astype(jnp.int32), fl, tsi.astype(jnp.int32))


def _moe(hn, slot, probs_t, ts, w1, w3, w2, cap):
    n = hn.shape[0]
    items = _moe_items(ts, cap)
    n_items = items[0].shape[0]
    grid_spec = pltpu.PrefetchScalarGridSpec(
        num_scalar_prefetch=6,
        grid=(n_items,),
        in_specs=[
            pl.BlockSpec((MOE_TILE, D_MODEL), lambda i, e, t, g, s, f, ts: (t[i], 0)),
            pl.BlockSpec((N_EXPERTS, MOE_TILE), lambda i, e, t, g, s, f, ts: (0, t[i])),
            pl.BlockSpec((N_EXPERTS, MOE_TILE), lambda i, e, t, g, s, f, ts: (0, t[i])),
            pl.BlockSpec((1, D_MODEL, EXPERT_FF), lambda i, e, t, g, s, f, ts: (e[i], 0, 0)),
            pl.BlockSpec((1, D_MODEL, EXPERT_FF), lambda i, e, t, g, s, f, ts: (e[i], 0, 0)),
            pl.BlockSpec((1, EXPERT_FF, D_MODEL), lambda i, e, t, g, s, f, ts: (e[i], 0, 0)),
        ],
        out_specs=pl.BlockSpec((MOE_SB, D_MODEL), lambda i, e, t, g, s, f, ts: (g[i], 0)),
        scratch_shapes=[pltpu.VMEM((MOE_SB, D_MODEL), F32), pltpu.VMEM((MOE_SB, 1), F32)],
    )
    return pl.pallas_call(
        _moe_kernel,
        grid_spec=grid_spec,
        out_shape=jax.ShapeDtypeStruct((N_EXPERTS * cap, D_MODEL), BF16),
        compiler_params=_cparams(("arbitrary",)),
        name="moe_dispatch_mlp",
    )(*items, hn, slot, probs_t, w1, w3, w2)


def _combine_kernel(it_e, it_tile, it_yb, it_rel, it_fl, slot_ref, y_ref, h_ref, lnf_ref,
                    out_ref, acc_ref):
    i = pl.program_id(0)
    fl = it_fl[i]

    @pl.when((fl & FL_FIRST) != 0)
    def _():
        acc_ref[...] = jnp.zeros_like(acc_ref)

    @pl.when((fl & FL_VALID) != 0)
    def _():
        srow = slot_ref[pl.ds(it_e[i], 1), :]
        iota_s = lax.broadcasted_iota(jnp.int32, (CMB_YB, CMB_TILE), 0)
        oh = jnp.where((iota_s + it_rel[i]) == srow, 1.0, 0.0).astype(BF16)
        acc_ref[...] += _dot_tn(oh, y_ref[...])

    @pl.when((fl & FL_LAST) != 0)
    def _():
        hh = h_ref[...] + acc_ref[...]
        out_ref[...] = hh * lax.rsqrt(jnp.mean(hh * hh, axis=-1, keepdims=True) + EPS) * lnf_ref[...]


def _combine_items(ts, cap):
    n_grp = ts.shape[1]
    per = CMB_TILE // TOPK_LANES
    n_tile = n_grp // per
    tse = jnp.concatenate([ts, jnp.full((N_EXPERTS, 1), cap, jnp.int32)], axis=1)
    a = tse[:, 0:n_grp:per].T.reshape(-1)
    b = tse[:, per::per].T.reshape(-1)
    kf = jnp.minimum(a, cap - 1) // CMB_YB
    cnt = jnp.where(b > a, (b - 1) // CMB_YB - kf + 1, 1)
    incl = jnp.cumsum(cnt)
    total = incl[-1]
    n_items = n_tile * N_EXPERTS + N_EXPERTS * cap // CMB_YB
    idx = jnp.arange(n_items, dtype=jnp.int32)
    valid = idx < total
    idc = jnp.minimum(idx, total - 1)
    pair = jnp.searchsorted(incl, idc, side="right").astype(jnp.int32)
    r = idc - (incl[pair] - cnt[pair])
    tile = pair // N_EXPERTS
    e = pair % N_EXPERTS
    k = kf[pair] + r
    first = jnp.logical_and(e == 0, r == 0)
    last = jnp.logical_and(e == N_EXPERTS - 1, r == cnt[pair] - 1)
    fl = jnp.where(valid, FL_VALID + FL_FIRST * first + FL_LAST * last, 0).astype(jnp.int32)
    yb = e * (cap // CMB_YB) + k
    return (e.astype(jnp.int32), tile.astype(jnp.int32), yb.astype(jnp.int32),
            (k * CMB_YB).astype(jnp.int32), fl)


def _combine(y, slot, ts, h, ln_f, cap):
    n = h.shape[0]
    items = _combine_items(ts, cap)
    n_items = items[0].shape[0]
    grid_spec = pltpu.PrefetchScalarGridSpec(
        num_scalar_prefetch=5,
        grid=(n_items,),
        in_specs=[
            pl.BlockSpec((N_EXPERTS, CMB_TILE), lambda i, e, t, yb, rel, f: (0, t[i])),
            pl.BlockSpec((CMB_YB, D_MODEL), lambda i, e, t, yb, rel, f: (yb[i], 0)),
            pl.BlockSpec((CMB_TILE, D_MODEL), lambda i, e, t, yb, rel, f: (t[i], 0)),
            pl.BlockSpec((1, D_MODEL), lambda i, e, t, yb, rel, f: (0, 0)),
        ],
        out_specs=pl.BlockSpec((CMB_TILE, D_MODEL), lambda i, e, t, yb, rel, f: (t[i], 0)),
        scratch_shapes=[pltpu.VMEM((CMB_TILE, D_MODEL), F32)],
    )
    return pl.pallas_call(
        _combine_kernel,
        grid_spec=grid_spec,
        out_shape=jax.ShapeDtypeStruct((n, D_MODEL), F32),
        compiler_params=_cparams(("arbitrary",)),
        name="moe_combine",
    )(*items, slot, y, h, ln_f)


RT_TOK = TOPK_LANES
DSP_TOK = 2 * TOPK_LANES
RT_ROWS = 128
RT_SPECS = 2 * N_EXPERTS
RT_GROUP = 4
XS_WIDTH = D_MODEL + GATE_LANES
SF_VALID, SF_FIRST, SF_LAST = 1, 2, 4
BF_ACTIVE, BF_FIRST = 1, 2


def _route_steps(ts, cap, tok):
    ts = ts[:, ::tok // TOPK_LANES]
    n_sub = ts.shape[1]
    tse = jnp.concatenate([ts, jnp.full((N_EXPERTS, 1), cap, jnp.int32)], axis=1)
    a, b = tse[:, :-1], tse[:, 1:]
    kf = jnp.minimum(a, cap - 1) // RT_ROWS
    kl = jnp.where(b > a, (b - 1) // RT_ROWS, kf)
    rounds = jnp.max((kl - kf + 2) // 2, axis=0)
    incl = jnp.cumsum(rounds)
    total = incl[-1]
    n_steps = ((tok // RT_ROWS + 2) // 2) * n_sub
    idx = jnp.arange(n_steps, dtype=jnp.int32)
    valid = idx < total
    idc = jnp.minimum(idx, total - 1)
    j = jnp.searchsorted(incl, idc, side="right").astype(jnp.int32)
    r = idc - (incl[j] - rounds[j])
    k0 = (kf[:, j] + 2 * r)[:, :, None]
    par = jnp.arange(2, dtype=jnp.int32)[None, None, :]
    cand = k0 + (par - k0) % 2
    a_s, b_s, kl_s = a[:, j][:, :, None], b[:, j][:, :, None], kl[:, j][:, :, None]
    active = valid[None, :, None] & (cand <= kl_s) & (b_s > a_s)
    first = active & (a_s <= cand * RT_ROWS) & (cand * RT_ROWS < b_s)
    held = lax.cummax(jnp.where(active, cand, -1), axis=1)
    held = jnp.where(held < 0, par, held)
    order = lambda v: jnp.transpose(v, (1, 0, 2)).reshape(-1).astype(jnp.int32)
    bflag = BF_ACTIVE * active + BF_FIRST * first
    sflag = jnp.where(valid, SF_VALID + SF_FIRST * (r == 0) + SF_LAST * (r == rounds[j] - 1), 0)
    return (j, order(held), order(bflag), sflag.astype(jnp.int32)), total.astype(jnp.int32)


def _route_groups():
    return [[(RT_GROUP // 2 * g + q // 2, q % 2) for q in range(RT_GROUP)]
            for g in range(RT_SPECS // RT_GROUP)]


def _one_hots(i, blk, flg, slot_ref, specs):
    iota = lax.broadcasted_iota(jnp.int32, (RT_ROWS, slot_ref.shape[1]), 0)
    ohs = []
    for e, par in specs:
        k = i * RT_SPECS + e * 2 + par
        s0 = jnp.where((flg[k] & BF_ACTIVE) != 0, blk[k] * RT_ROWS, -(1 << 30))
        ohs.append(jnp.where((iota + s0) == slot_ref[e:e + 1, :], 1.0, 0.0).astype(BF16))
    return jnp.concatenate(ohs, axis=0)


def _dispatch_kernel(tile, blk, flg, sflg, x_ref, p3_ref, slot_ref, *outs):
    i = pl.program_id(0)

    @pl.when(i == 0)
    def _():
        for out in outs:
            out[...] = jnp.zeros_like(out)

    @pl.when((sflg[i] & SF_VALID) != 0)
    def _():
        xa = jnp.concatenate([x_ref[...], p3_ref[...]], axis=1)
        for specs in _route_groups():
            res = jnp.dot(_one_hots(i, blk, flg, slot_ref, specs), xa, preferred_element_type=F32)
            for q, (e, par) in enumerate(specs):
                out = outs[e * 2 + par]
                first = (flg[i * RT_SPECS + e * 2 + par] & BF_FIRST) != 0
                prev = out[...]
                prev = jnp.where(first, jnp.zeros_like(prev), prev)
                out[...] = prev + res[q * RT_ROWS:(q + 1) * RT_ROWS].astype(out.dtype)


def _dispatch(hn, p3, slot, steps, n_steps, cap):
    def out_spec(e, par):
        return pl.BlockSpec((RT_ROWS, XS_WIDTH),
                            lambda i, t, b, f, s: (b[i * RT_SPECS + e * 2 + par] // 2, 0))

    grid_spec = pltpu.PrefetchScalarGridSpec(
        num_scalar_prefetch=4,
        grid=(n_steps,),
        in_specs=[
            pl.BlockSpec((DSP_TOK, D_MODEL), lambda i, t, b, f, s: (t[i], 0)),
            pl.BlockSpec((DSP_TOK, GATE_LANES), lambda i, t, b, f, s: (t[i], 0)),
            pl.BlockSpec((N_EXPERTS, DSP_TOK), lambda i, t, b, f, s: (0, t[i])),
        ],
        out_specs=[out_spec(e, par) for e in range(N_EXPERTS) for par in range(2)],
    )
    sds = jax.ShapeDtypeStruct((cap // 2, XS_WIDTH), BF16)
    return pl.pallas_call(
        _dispatch_kernel,
        grid_spec=grid_spec,
        out_shape=[sds] * RT_SPECS,
        compiler_params=_cparams(("arbitrary",)),
        name="moe_dispatch",
    )(*steps, hn, p3, slot)


MLP_ROWS = 4 * RT_ROWS


def _expert_mlp_kernel(xe_ref, xo_ref, w1_ref, w3_ref, w2_ref, y_ref, *, e):
    r = RT_ROWS
    xa = jnp.concatenate([xe_ref[0:r], xo_ref[0:r], xe_ref[r:2 * r], xo_ref[r:2 * r]], axis=0)
    xs = xa[:, :D_MODEL]
    g3 = xa[:, D_MODEL:].astype(F32)
    lane = lax.broadcasted_iota(jnp.int32, g3.shape, 1)
    mine = jnp.logical_and((lane & (N_EXPERTS - 1)) == e, lane < 3 * N_EXPERTS)
    gate = jnp.sum(jnp.where(mine, g3, 0.0), axis=1, keepdims=True)
    h1 = jnp.dot(xs, w1_ref[0], preferred_element_type=F32)
    h3 = jnp.dot(xs, w3_ref[0], preferred_element_type=F32)
    hid = (_silu(h1) * h3).astype(BF16)
    y_ref[...] = (jnp.dot(hid, w2_ref[0], preferred_element_type=F32) * gate).astype(y_ref.dtype)


def _expert_mlp(xe, xo, w1, w3, w2, e, cap):
    half = pl.BlockSpec((MLP_ROWS // 2, XS_WIDTH), lambda m: (m, 0))
    return pl.pallas_call(
        functools.partial(_expert_mlp_kernel, e=e),
        grid=(cap // MLP_ROWS,),
        in_specs=[
            half, half,
            pl.BlockSpec((1, D_MODEL, EXPERT_FF), lambda m: (e, 0, 0)),
            pl.BlockSpec((1, D_MODEL, EXPERT_FF), lambda m: (e, 0, 0)),
            pl.BlockSpec((1, EXPERT_FF, D_MODEL), lambda m: (e, 0, 0)),
        ],
        out_specs=pl.BlockSpec((MLP_ROWS, D_MODEL), lambda m: (m, 0)),
        out_shape=jax.ShapeDtypeStruct((cap, D_MODEL), BF16),
        compiler_params=_cparams(("arbitrary",)),
        name="expert_mlp",
    )(xe, xo, w1, w3, w2)


def _combine2_kernel(tile, blk, flg, sflg, slot_ref, h_ref, lnf_ref, *rest):
    ys = rest[:RT_SPECS]
    out_ref, acc_ref = rest[RT_SPECS], rest[RT_SPECS + 1]
    i = pl.program_id(0)
    sf = sflg[i]

    @pl.when((sf & SF_FIRST) != 0)
    def _():
        acc_ref[...] = jnp.zeros_like(acc_ref)

    @pl.when((sf & SF_VALID) != 0)
    def _():
        total = None
        for specs in _route_groups():
            ycat = jnp.concatenate([ys[e * 2 + par][...] for e, par in specs], axis=0)
            t = _dot_tn(_one_hots(i, blk, flg, slot_ref, specs), ycat)
            total = t if total is None else total + t
        acc_ref[...] += total

    @pl.when((sf & SF_LAST) != 0)
    def _():
        hh = h_ref[...] + acc_ref[...]
        out_ref[...] = hh * lax.rsqrt(jnp.mean(hh * hh, axis=-1, keepdims=True) + EPS) * lnf_ref[...]


def _combine2(ys, slot, steps, n_steps, h, ln_f):
    n = h.shape[0]

    def y_spec(e, par):
        return pl.BlockSpec((RT_ROWS, D_MODEL),
                            lambda i, t, b, f, s: (b[i * RT_SPECS + e * 2 + par], 0))

    grid_spec = pltpu.PrefetchScalarGridSpec(
        num_scalar_prefetch=4,
        grid=(n_steps,),
        in_specs=[
            pl.BlockSpec((N_EXPERTS, RT_TOK), lambda i, t, b, f, s: (0, t[i])),
            pl.BlockSpec((RT_TOK, D_MODEL), lambda i, t, b, f, s: (t[i], 0)),
            pl.BlockSpec((1, D_MODEL), lambda i, t, b, f, s: (0, 0)),
        ] + [y_spec(e, par) for e in range(N_EXPERTS) for par in range(2)],
        out_specs=pl.BlockSpec((RT_TOK, D_MODEL), lambda i, t, b, f, s: (t[i], 0)),
        scratch_shapes=[pltpu.VMEM((RT_TOK, D_MODEL), F32)],
    )
    return pl.pallas_call(
        _combine2_kernel,
        grid_spec=grid_spec,
        out_shape=jax.ShapeDtypeStruct((n, D_MODEL), F32),
        compiler_params=_cparams(("arbitrary",)),
        name="moe_combine",
    )(*steps, slot, h, ln_f, *[ys[e] for e in range(N_EXPERTS) for _ in range(2)])

def _prep_in_weights(w_in, a_log_f, a_log_b, dt_bias_f, dt_bias_b):
    offs = np.cumsum([0, A_QK, A_QK, A_V, A_V, A_HEADS, A_HEADS, A_HEADS, A_HEADS,
                      B_QK, B_QK, B_V, B_V, GLA_LOWRANK, GLA_LOWRANK])
    seg = [w_in[:, offs[i]:offs[i + 1]] for i in range(14)]
    w_main = jnp.concatenate(seg[0:4] + seg[8:12], axis=1).astype(BF16)
    w_small = jnp.concatenate(seg[4:8] + seg[12:14], axis=1)
    w_small = jnp.pad(w_small, ((0, 0), (0, SMALL_WIDTH - w_small.shape[1]))).astype(BF16)
    pad = SMALL_WIDTH - 4 * A_HEADS
    z = jnp.zeros((2 * A_HEADS,), F32)
    a_row = jnp.concatenate([z, a_log_f.astype(F32), a_log_b.astype(F32), jnp.zeros((pad,), F32)])
    dt_row = jnp.concatenate([z, dt_bias_f.astype(F32), dt_bias_b.astype(F32), jnp.zeros((pad,), F32)])
    small_params = jnp.zeros((8, SMALL_WIDTH), F32).at[0].set(a_row).at[1].set(dt_row)
    return w_main, w_small, small_params


def kernel(x_prompt, x_sample, ln1, w_in, conv_w, a_log_f, a_log_b, dt_bias_f, dt_bias_b, norm_a, gla_w_f, gla_b_f, gla_w_b, gla_b_b, norm_b, w_out, ln2, w_router, w1, w3, w2, ln_f):
    w_main, w_small, small_params = _prep_in_weights(w_in[0], a_log_f[0], a_log_b[0],
                                                     dt_bias_f[0], dt_bias_b[0])
    conv8 = _conv_layout(conv_w[0])
    wg, bg = _gla_gate_weights(gla_w_f[0], gla_b_f[0], gla_w_b[0], gla_b_b[0])
    wl = jnp.asarray(_gla_level_weights(), BF16)
    wo = w_out[0].astype(BF16)
    wr_t = w_router[0].astype(F32).T
    wrh = wr_t.astype(BF16)
    wrl = (wr_t - wrh.astype(F32)).astype(BF16)
    wr_pad = jnp.pad(w_router[0].astype(F32), ((0, 0), (0, GATE_LANES - N_EXPERTS)))
    wrh2 = wr_pad.astype(BF16)
    wrl2 = (wr_pad - wrh2.astype(F32)).astype(BF16)
    w1b, w3b, w2b = w1[0].astype(BF16), w3[0].astype(BF16), w2[0].astype(BF16)
    row = lambda v: v.astype(F32).reshape(1, -1)
    outs = []
    for x in (x_prompt, x_sample):
        bsz, seq, _ = x.shape
        n = bsz * seq
        cap = EC_CAPACITY * n // N_EXPERTS
        x2d = x.reshape(n, D_MODEL)
        main, small = _in_proj(x2d, row(ln1[0]), w_main, w_small, small_params, tm=1024, tn=1024)
        main3 = main.reshape(bsz, seq, MAIN_WIDTH)
        small3 = small.reshape(bsz, seq, SMALL_WIDTH)
        oaf, oab = _gdn(_qkv_conv(main3, conv8, rows=SCAN_ROWS), _scal_layout(small3))
        obf, obb = _gla(main3, small3, wg, bg, wl)
        flat = lambda o: o.reshape(n, -1)
        h, hn, probs_t, p3 = _out_proj(flat(oaf), flat(oab), flat(obf), flat(obb), main, x2d, wo,
                                       row(norm_a[0]), row(norm_b[0]), row(ln2[0]),
                                       wrh, wrl, wrh2, wrl2, tm=256)
        slot, ts = _topk(probs_t, cap)
        dsp_steps, n_dsp = _route_steps(ts, cap, DSP_TOK)
        steps, n_steps = _route_steps(ts, cap, RT_TOK)
        xs = _dispatch(hn, p3, slot, dsp_steps, n_dsp, cap)
        ys = [_expert_mlp(xs[2 * e], xs[2 * e + 1], w1b, w3b, w2b, e, cap) for e in range(N_EXPERTS)]
        out = _combine2(ys, slot, steps, n_steps, h, row(ln_f))
        outs.append(out.reshape(bsz, seq, D_MODEL))
    return tuple(outs)
```

```python
import functools

import jax
import jax.numpy as jnp
import numpy as np
from jax import lax
from jax.experimental import pallas as pl
from jax.experimental.pallas import tpu as pltpu

F32 = jnp.float32
BF16 = jnp.bfloat16

D_MODEL = 2048
A_HEADS, A_DK, A_DV = 8, 128, 128
B_HEADS, B_DK, B_DV = 4, 128, 256
GLA_LOWRANK = 16
GLA_GATE_NORM = 16.0
CONV_K = 5
CHUNK = 64
N_EXPERTS = 16
EC_CAPACITY = 2
EXPERT_FF = D_MODEL // 2
EPS = 1e-6

A_QK = A_HEADS * A_DK
A_V = A_HEADS * A_DV
B_QK = B_HEADS * B_DK
B_V = B_HEADS * B_DV
MAIN_WIDTH = 2 * A_QK + 2 * A_V + 2 * B_QK + 2 * B_V
SMALL_WIDTH = 128
LR_OFF = 4 * A_HEADS
V7X_VMEM_LIMIT = 56 * 1024 * 1024
CONV_HALO = 16
SCAN_ROWS = 256
CPS = SCAN_ROWS // CHUNK
GLA_LEVELS = 6
GATE_LANES = 128
GDN_HEADS_PER_PASS = 8


def _cparams(sem):
    return pltpu.CompilerParams(dimension_semantics=sem, vmem_limit_bytes=V7X_VMEM_LIMIT)


def _dot(a, b):
    return jnp.dot(a.astype(BF16), b.astype(BF16), preferred_element_type=F32)


def _dot_nt(a, b):
    return lax.dot_general(a.astype(BF16), b.astype(BF16), (((1,), (1,)), ((), ())),
                           preferred_element_type=F32)


def _dot_tn(a, b):
    return lax.dot_general(a.astype(BF16), b.astype(BF16), (((0,), (0,)), ((), ())),
                           preferred_element_type=F32)


def _silu(x):
    return x * (1.0 / (1.0 + jnp.exp(-x)))


def _in_proj_kernel(x_ref, ln_ref, wm_ref, ws_ref, sp_ref, main_ref, small_ref, hn_ref):
    j = pl.program_id(1)

    @pl.when(j == 0)
    def _():
        xf = x_ref[...]
        y = xf * lax.rsqrt(jnp.mean(xf * xf, axis=-1, keepdims=True) + EPS) * ln_ref[...]
        hn = y.astype(BF16)
        hn_ref[...] = hn
        s = jnp.dot(hn, ws_ref[...], preferred_element_type=F32)
        lane = lax.broadcasted_iota(jnp.int32, s.shape, 1)
        neg_a = -jnp.exp(sp_ref[0:1, :])
        z = s + sp_ref[1:2, :]
        softplus = jnp.maximum(z, 0.0) + jnp.log(1.0 + jnp.exp(-jnp.abs(z)))
        sig = 1.0 / (1.0 + jnp.exp(-s))
        small_ref[...] = jnp.where(lane < 2 * A_HEADS, sig,
                                   jnp.where(lane < 4 * A_HEADS, neg_a * softplus, s))

    main_ref[...] = jnp.dot(hn_ref[...], wm_ref[...], preferred_element_type=F32).astype(BF16)


def _in_proj(x2d, ln1, w_main, w_small, small_params, tm, tn):
    t = x2d.shape[0]
    return pl.pallas_call(
        _in_proj_kernel,
        grid=(t // tm, MAIN_WIDTH // tn),
        in_specs=[
            pl.BlockSpec((tm, D_MODEL), lambda i, j: (i, 0)),
            pl.BlockSpec((1, D_MODEL), lambda i, j: (0, 0)),
            pl.BlockSpec((D_MODEL, tn), lambda i, j: (0, j)),
            pl.BlockSpec((D_MODEL, SMALL_WIDTH), lambda i, j: (0, 0)),
            pl.BlockSpec((8, SMALL_WIDTH), lambda i, j: (0, 0)),
        ],
        out_specs=[
            pl.BlockSpec((tm, tn), lambda i, j: (i, j)),
            pl.BlockSpec((tm, SMALL_WIDTH), lambda i, j: (i, 0)),
        ],
        out_shape=[
            jax.ShapeDtypeStruct((t, MAIN_WIDTH), BF16),
            jax.ShapeDtypeStruct((t, SMALL_WIDTH), F32),
        ],
        scratch_shapes=[pltpu.VMEM((tm, D_MODEL), BF16)],
        compiler_params=_cparams(("arbitrary", "arbitrary")),
        name="in_proj",
    )(x2d, ln1, w_main, w_small, small_params)


def _chunk_iotas():
    ii = lax.broadcasted_iota(jnp.int32, (CHUNK, CHUNK), 0)
    jj = lax.broadcasted_iota(jnp.int32, (CHUNK, CHUNK), 1)
    return ii, jj


def _row_to_col(row, eye):
    return jnp.sum(jnp.where(eye, row, 0.0), axis=1, keepdims=True)


def _col_to_row(col, eye):
    return jnp.sum(jnp.where(eye, col, 0.0), axis=0, keepdims=True)


def _l2norm(x):
    return x * lax.rsqrt(jnp.sum(x * x, axis=-1, keepdims=True) + EPS)


def _delta_chunks(qs, ks, vs, beta_rows, g_rows, states, revs):
    ii, jj = _chunk_iotas()
    eye = ii == jj
    nc = range(len(qs))
    incl = [(ii <= jj) if r else (ii >= jj) for r in revs]
    strict = [(ii < jj) if r else (ii > jj) for r in revs]
    gc_col = [jnp.sum(jnp.where(incl[c], g_rows[c], 0.0), axis=1, keepdims=True) for c in nc]
    gc_row = [_col_to_row(gc_col[c], eye) for c in nc]
    beta_col = [_row_to_col(beta_rows[c], eye) for c in nc]
    g_tot = [jnp.sum(g_rows[c], axis=1, keepdims=True) for c in nc]
    decay = [jnp.where(incl[c], jnp.exp(jnp.where(incl[c], gc_col[c] - gc_row[c], 0.0)), 0.0)
             for c in nc]
    kb = [ks[c] * beta_col[c] for c in nc]
    kk = [_dot_nt(kb[c], ks[c]) for c in nc]
    qk = [_dot_nt(qs[c], ks[c]) for c in nc]
    p = [jnp.where(strict[c], -kk[c] * decay[c], 0.0) for c in nc]
    toff = p
    for _ in range(5):
        p = [_dot(p[c], p[c]) for c in nc]
        tp = [_dot(toff[c], p[c]) for c in nc]
        toff = [toff[c] + p[c] + tp[c] for c in nc]
    e_gc = [jnp.exp(gc_col[c]) for c in nc]
    rhs = [jnp.concatenate([vs[c] * beta_col[c], kb[c] * e_gc[c]], axis=1) for c in nc]
    sol = [rhs[c] + _dot(toff[c], rhs[c]) for c in nc]
    attn = [jnp.where(incl[c], qk[c] * decay[c], 0.0) for c in nc]
    ws = [_dot(sol[c][:, A_DV:], states[c]) for c in nc]
    qs_state = [_dot(qs[c] * e_gc[c], states[c]) for c in nc]
    v_new = [sol[c][:, :A_DV] - ws[c] for c in nc]
    av = [_dot(attn[c], v_new[c]) for c in nc]
    kv = [_dot_tn(ks[c] * jnp.exp(g_tot[c] - gc_col[c]), v_new[c]) for c in nc]
    outs = [qs_state[c] + av[c] for c in nc]
    new_states = [states[c] * jnp.exp(g_tot[c]) + kv[c] for c in nc]
    return outs, new_states


def _qkv_conv_kernel(blk, prv, nxt, cw_ref, out_ref, pad_ref):
    n = pl.program_id(1)
    nb = pl.num_programs(1)
    rows = blk.shape[1]
    pad_ref[0:CONV_HALO, :] = jnp.where(n == 0, 0.0, prv[0].astype(F32))
    pad_ref[CONV_HALO:CONV_HALO + rows, :] = blk[0].astype(F32)
    pad_ref[CONV_HALO + rows:, :] = jnp.where(n == nb - 1, 0.0, nxt[0].astype(F32))
    lo = CONV_HALO - CONV_K // 2
    for g in range((2 * A_QK + A_V) // A_DK):
        cols = slice(g * A_DK, (g + 1) * A_DK)
        w = cw_ref[:, cols]
        acc = w[0:1, :] * pad_ref[lo:lo + rows, cols]
        for j in range(1, CONV_K):
            acc = acc + w[j:j + 1, :] * pad_ref[lo + j:lo + j + rows, cols]
        y = _silu(acc)
        if g < A_HEADS:
            y = _l2norm(y) * (A_DK ** -0.5)
        elif g < 2 * A_HEADS:
            y = _l2norm(y)
        out_ref[0, :, cols] = y.astype(out_ref.dtype)


def _qkv_conv(main3, conv8, rows):
    bsz, seq, _ = main3.shape
    width = 2 * A_QK + A_V
    hb = rows // CONV_HALO
    nhalo = seq // CONV_HALO
    return pl.pallas_call(
        _qkv_conv_kernel,
        grid=(bsz, seq // rows),
        in_specs=[
            pl.BlockSpec((1, rows, width), lambda b, n: (b, n, 0)),
            pl.BlockSpec((1, CONV_HALO, width), lambda b, n: (b, jnp.maximum(n * hb - 1, 0), 0)),
            pl.BlockSpec((1, CONV_HALO, width),
                         lambda b, n: (b, jnp.minimum((n + 1) * hb, nhalo - 1), 0)),
            pl.BlockSpec((8, width), lambda b, n: (0, 0)),
        ],
        out_specs=pl.BlockSpec((1, rows, width), lambda b, n: (b, n, 0)),
        out_shape=jax.ShapeDtypeStruct((bsz, seq, width), BF16),
        scratch_shapes=[pltpu.VMEM((rows + 2 * CONV_HALO, width), F32)],
        compiler_params=_cparams(("arbitrary", "arbitrary")),
        name="qkv_conv",
    )(main3, main3, main3, conv8)


def _gdn_kernel(blk_f, blk_b, sc_f, sc_b, of_ref, ob_ref, st_ref):
    n = pl.program_id(1)

    @pl.when(n == 0)
    def _():
        st_ref[...] = jnp.zeros_like(st_ref)

    def load(d, row0, a, h):
        blk = blk_f if d == 0 else blk_b
        col = a * A_QK + h * A_DK
        return blk[0, pl.ds(row0, CHUNK), col:col + A_DK].astype(F32)

    def body(s, carry):
        for h0 in range(0, A_HEADS, GDN_HEADS_PER_PASS):
            chains = []
            for d, sc_ref in enumerate((sc_f, sc_b)):
                sc = s if d == 0 else CPS - 1 - s
                row0 = pl.multiple_of(sc * CHUNK, CHUNK)
                for h in range(h0, h0 + GDN_HEADS_PER_PASS):
                    chains.append((d, h, sc, row0, sc_ref))
            qs = [load(d, row0, 0, h) for d, h, sc, row0, _ in chains]
            ks = [load(d, row0, 1, h) for d, h, sc, row0, _ in chains]
            vs = [load(d, row0, 2, h) for d, h, sc, row0, _ in chains]
            betas = [r[0, 0, d, h, pl.ds(sc, 1), :] for d, h, sc, row0, r in chains]
            gs = [r[0, 0, 2 + d, h, pl.ds(sc, 1), :] for d, h, sc, row0, r in chains]
            states = [st_ref[d, h] for d, h, sc, row0, _ in chains]
            outs, new_states = _delta_chunks(qs, ks, vs, betas, gs, states,
                                             [d == 1 for d, *_ in chains])
            for (d, h, sc, row0, _), o, st in zip(chains, outs, new_states):
                st_ref[d, h] = st
                out_ref = of_ref if d == 0 else ob_ref
                out_ref[0, pl.ds(row0, CHUNK), h * A_DV:(h + 1) * A_DV] = o.astype(out_ref.dtype)
        return carry

    lax.fori_loop(0, CPS, body, 0)


def _scal_layout(small):
    bsz, seq = small.shape[:2]
    s = small[..., :4 * A_HEADS].reshape(bsz, seq // SCAN_ROWS, CPS, CHUNK, 4, A_HEADS)
    return jnp.transpose(s, (0, 1, 4, 5, 2, 3))


def _conv_layout(conv_w):
    return jnp.pad(conv_w.astype(F32), ((0, 8 - CONV_K), (0, 0)))


def _gdn(qkv, scal):
    bsz, seq, width = qkv.shape
    nb = seq // SCAN_ROWS

    def specs(blk_of):
        return (pl.BlockSpec((1, SCAN_ROWS, width), lambda b, n: (b, blk_of(n), 0)),
                pl.BlockSpec((1, 1, 4, A_HEADS, CPS, CHUNK), lambda b, n: (b, blk_of(n), 0, 0, 0, 0)))

    qf, sf = specs(lambda n: n)
    qb, sb = specs(lambda n: nb - 1 - n)
    out_sds = jax.ShapeDtypeStruct((bsz, seq, A_V), BF16)
    return pl.pallas_call(
        _gdn_kernel,
        grid=(bsz, nb),
        in_specs=[qf, qb, sf, sb],
        out_specs=[
            pl.BlockSpec((1, SCAN_ROWS, A_V), lambda b, n: (b, n, 0)),
            pl.BlockSpec((1, SCAN_ROWS, A_V), lambda b, n: (b, nb - 1 - n, 0)),
        ],
        out_shape=[out_sds, out_sds],
        scratch_shapes=[pltpu.VMEM((2, A_HEADS, A_DK, A_DV), F32)],
        compiler_params=_cparams(("arbitrary", "arbitrary")),
        name="gdn",
    )(qkv, qkv, scal, scal)


GLA_MM_LEVELS = (4, 5)


def _gla_level_weights():
    c = CHUNK
    w = np.zeros((2, 3 * c, c), np.float32)
    for d in range(2):
        for blk, l in enumerate(GLA_MM_LEVELS):
            s = c >> (l + 1)
            for i in range(c):
                mid = (i // (2 * s)) * 2 * s + s
                if d == 0:
                    ts = range(mid, i + 1) if i >= mid else range(i + 1, mid)
                else:
                    ts = range(i, mid) if i < mid else range(mid, i)
                for t in ts:
                    w[d, blk * c + i, t] = 1.0
        for i in range(c):
            for t in range(c):
                before = t <= i if d == 0 else t >= i
                w[d, 2 * c + i, t] = 1.0 if before else 0.0
    return w


def _gla_kernel(qk_f, v_f, sm_f, qk_b, v_b, sm_b, wg_ref, bg_ref, wl_ref, of_ref, ob_ref, st_ref):
    n = pl.program_id(1)

    @pl.when(n == 0)
    def _():
        st_ref[...] = jnp.zeros_like(st_ref)

    ii, jj = _chunk_iotas()
    eye = ii == jj
    tok = lax.broadcasted_iota(jnp.int32, (CHUNK, 1), 0)

    streams = ((qk_f, v_f, sm_f, of_ref), (qk_b, v_b, sm_b, ob_ref))
    dirs = range(2)
    chains = [(d, h) for d in dirs for h in range(B_HEADS)]
    masks = [ii // (CHUNK >> l) == jj // (CHUNK >> l) for l in range(GLA_LEVELS)]

    def body(s, carry):
        rows = [pl.ds(pl.multiple_of((s if d == 0 else CPS - 1 - s) * CHUNK, CHUNK), CHUNK)
                for d in dirs]
        z = [_dot(streams[d][2][0, rows[d], :], wg_ref[d]) + bg_ref[d] for d in dirs]
        la = [(jnp.minimum(z[d], 0.0) - jnp.log(1.0 + jnp.exp(-jnp.abs(z[d]))))
              * (1.0 / GLA_GATE_NORM) for d in dirs]
        hi = [la[d].astype(BF16) for d in dirs]
        r1 = [la[d] - hi[d].astype(F32) for d in dirs]
        mid = [r1[d].astype(BF16) for d in dirs]
        low = [(r1[d] - mid[d].astype(F32)).astype(BF16) for d in dirs]
        ex = [jnp.dot(wl_ref[d], hi[d], preferred_element_type=F32)
              + jnp.dot(wl_ref[d], mid[d], preferred_element_type=F32)
              + jnp.dot(wl_ref[d], low[d], preferred_element_type=F32) for d in dirs]
        bcum = [ex[d][2 * CHUNK:] for d in dirs]
        b_tot = [bcum[0][CHUNK - 1:CHUNK], bcum[1][0:1]]
        q_all = [streams[d][0][0, rows[d], :B_QK].astype(F32) * (B_DK ** -0.5) for d in dirs]
        k_all = [streams[d][0][0, rows[d], B_QK:].astype(F32) for d in dirs]
        qs, ks = [], []
        for l in range(GLA_LEVELS):
            half = CHUNK >> (l + 1)
            right = (tok // half) % 2 == 1
            q_side = [right, jnp.logical_not(right)]
            if l in GLA_MM_LEVELS:
                blk = GLA_MM_LEVELS.index(l)
                expo = [ex[d][blk * CHUNK:(blk + 1) * CHUNK] for d in dirs]
            else:
                expo = []
                for d in dirs:
                    ref = jnp.concatenate(
                        [jnp.broadcast_to(bcum[d][g + half - 1 + d:g + half + d], (2 * half, B_QK))
                         for g in range(0, CHUNK, 2 * half)], axis=0)
                    expo.append(jnp.where(q_side[d], bcum[d] - ref, ref - bcum[d]))
            e_l = [jnp.exp(expo[d]) for d in dirs]
            qs.append([jnp.where(q_side[d], q_all[d] * e_l[d], 0.0).astype(BF16) for d in dirs])
            ks.append([jnp.where(q_side[d], 0.0, k_all[d] * e_l[d]).astype(BF16) for d in dirs])
        q_dec = [(q_all[d] * jnp.exp(bcum[d])).astype(BF16) for d in dirs]
        k_dec = [(k_all[d] * jnp.exp(b_tot[d] - bcum[d])).astype(BF16) for d in dirs]
        e_tot = [jnp.exp(b_tot[d]) for d in dirs]
        ck = [slice(h * B_DK, (h + 1) * B_DK) for h in range(B_HEADS)]
        cv = [slice(h * B_DV, (h + 1) * B_DV) for h in range(B_HEADS)]
        attn = [jnp.where(eye, _dot_nt(q_all[d][:, ck[h]], k_all[d][:, ck[h]]), 0.0)
                for d, h in chains]
        for l in range(GLA_LEVELS):
            part = [_dot_nt(qs[l][d][:, ck[h]], ks[l][d][:, ck[h]]) for d, h in chains]
            attn = [attn[c] + jnp.where(masks[l], part[c], 0.0) for c in range(len(chains))]
        v = [streams[d][1][0, rows[d], cv[h]] for d, h in chains]
        st = [st_ref[d, h] for d, h in chains]
        o_state = [_dot_nt(q_dec[d][:, ck[h]], st[c]) for c, (d, h) in enumerate(chains)]
        o_local = [_dot(attn[c], v[c]) for c in range(len(chains))]
        kv = [_dot_tn(v[c], k_dec[d][:, ck[h]]) for c, (d, h) in enumerate(chains)]
        for c, (d, h) in enumerate(chains):
            st_ref[d, h] = st[c] * e_tot[d][:, ck[h]] + kv[c]
            streams[d][3][0, rows[d], cv[h]] = (o_state[c] + o_local[c]).astype(of_ref.dtype)
        return carry

    lax.fori_loop(0, CPS, body, 0)


def _gla(main3, small3, wg, bg, wl):
    bsz, seq, _ = main3.shape
    nb = seq // SCAN_ROWS
    qk_blk = (2 * A_QK + 2 * A_V) // (2 * B_QK)
    v_blk = (2 * A_QK + 2 * A_V + 2 * B_QK) // B_V

    def stream(blk_of):
        return [
            pl.BlockSpec((1, SCAN_ROWS, 2 * B_QK), lambda b, n: (b, blk_of(n), qk_blk)),
            pl.BlockSpec((1, SCAN_ROWS, B_V), lambda b, n: (b, blk_of(n), v_blk)),
            pl.BlockSpec((1, SCAN_ROWS, SMALL_WIDTH), lambda b, n: (b, blk_of(n), 0)),
        ]

    out_sds = jax.ShapeDtypeStruct((bsz, seq, B_V), BF16)
    return pl.pallas_call(
        _gla_kernel,
        grid=(bsz, nb),
        in_specs=stream(lambda n: n) + stream(lambda n: nb - 1 - n) + [
            pl.BlockSpec((2, SMALL_WIDTH, B_QK), lambda b, n: (0, 0, 0)),
            pl.BlockSpec((2, 1, B_QK), lambda b, n: (0, 0, 0)),
            pl.BlockSpec((2, 3 * CHUNK, CHUNK), lambda b, n: (0, 0, 0)),
        ],
        out_specs=[
            pl.BlockSpec((1, SCAN_ROWS, B_V), lambda b, n: (b, n, 0)),
            pl.BlockSpec((1, SCAN_ROWS, B_V), lambda b, n: (b, nb - 1 - n, 0)),
        ],
        out_shape=[out_sds, out_sds],
        scratch_shapes=[pltpu.VMEM((2, B_HEADS, B_DV, B_DK), F32)],
        compiler_params=_cparams(("arbitrary", "arbitrary")),
        name="gla",
    )(main3, main3, small3, main3, main3, small3, wg, bg, wl)


def _gla_gate_weights(gla_w_f, gla_b_f, gla_w_b, gla_b_b):
    wg = jnp.zeros((2, SMALL_WIDTH, B_QK), F32)
    wg = wg.at[0, LR_OFF:LR_OFF + GLA_LOWRANK].set(gla_w_f.astype(F32))
    wg = wg.at[1, LR_OFF + GLA_LOWRANK:LR_OFF + 2 * GLA_LOWRANK].set(gla_w_b.astype(F32))
    bg = jnp.stack([gla_b_f, gla_b_b]).astype(F32).reshape(2, 1, B_QK)
    return wg.astype(BF16), bg


def _out_proj_kernel(oaf, oab, obf, obb, ga, gb, x_ref, wo_ref, na_ref, nb_ref, ln2_ref,
                     wrh_ref, wrl_ref, wrh2_ref, wrl2_ref, h_ref, hn_ref, pt_ref, p3_ref, mix_ref):
    def head_norm(rs, of_ref, ob_ref, g_ref, w_ref, width, heads, base):
        for h in range(heads):
            c = slice(h * width, (h + 1) * width)
            o = of_ref[rs, c].astype(F32) + ob_ref[rs, c].astype(F32)
            y = o * lax.rsqrt(jnp.mean(o * o, axis=-1, keepdims=True) + EPS) * w_ref[...]
            mix_ref[rs, base + h * width:base + (h + 1) * width] = (
                y * _silu(g_ref[rs, c].astype(F32))).astype(BF16)

    def mix(rs):
        head_norm(rs, oaf, oab, ga, na_ref, A_DV, A_HEADS, 0)
        head_norm(rs, obf, obb, gb, nb_ref, B_DV, B_HEADS, A_V)

    def project(rs):
        return x_ref[rs, :] + jnp.dot(mix_ref[rs, :], wo_ref[...], preferred_element_type=F32)

    def route(rs, hres):
        h_ref[rs, :] = hres
        hn = hres * lax.rsqrt(jnp.mean(hres * hres, axis=-1, keepdims=True) + EPS) * ln2_ref[...]
        hi = hn.astype(BF16)
        hn_ref[rs, :] = hi
        lo = (hn - hi.astype(F32)).astype(BF16)
        lt = _dot_nt(wrh_ref[...], hi) + _dot_nt(wrl_ref[...], hi) + _dot_nt(wrh_ref[...], lo)
        e = jnp.exp(lt - jnp.max(lt, axis=0, keepdims=True))
        pt_ref[:, rs] = e / jnp.sum(e, axis=0, keepdims=True)
        l2 = _dot(hi, wrh2_ref[...]) + _dot(hi, wrl2_ref[...]) + _dot(lo, wrh2_ref[...])
        lane = lax.broadcasted_iota(jnp.int32, l2.shape, 1)
        l2 = jnp.where(lane < N_EXPERTS, l2, -jnp.inf)
        e2 = jnp.exp(l2 - jnp.max(l2, axis=1, keepdims=True))
        p2 = e2 / jnp.sum(e2, axis=1, keepdims=True)
        g_hi = p2.astype(BF16)
        r1 = p2 - g_hi.astype(F32)
        g_mid = r1.astype(BF16)
        g_lo = (r1 - g_mid.astype(F32)).astype(BF16)
        p3 = (g_hi.astype(F32) + pltpu.roll(g_mid.astype(F32), N_EXPERTS, 1)
              + pltpu.roll(g_lo.astype(F32), 2 * N_EXPERTS, 1))
        p3_ref[rs, :] = p3.astype(BF16)

    half = x_ref.shape[0] // 2
    r0, r1_ = slice(0, half), slice(half, 2 * half)
    mix(r0)
    h0 = project(r0)
    mix(r1_)
    h1 = project(r1_)
    route(r0, h0)
    route(r1_, h1)


def _out_proj(oaf, oab, obf, obb, main, x2d, wo, na, nb, ln2, wrh, wrl, wrh2, wrl2, tm):
    t = x2d.shape[0]
    ga_blk = (2 * A_QK + A_V) // A_V
    gb_blk = (2 * A_QK + 2 * A_V + 2 * B_QK + B_V) // B_V
    row = lambda w: pl.BlockSpec((tm, w), lambda i: (i, 0))
    full = lambda a, b: pl.BlockSpec((a, b), lambda i: (0, 0))
    return pl.pallas_call(
        _out_proj_kernel,
        grid=(t // tm,),
        in_specs=[
            row(A_V), row(A_V), row(B_V), row(B_V),
            pl.BlockSpec((tm, A_V), lambda i: (i, ga_blk)),
            pl.BlockSpec((tm, B_V), lambda i: (i, gb_blk)),
            row(D_MODEL), full(D_MODEL, D_MODEL), full(1, A_DV), full(1, B_DV), full(1, D_MODEL),
            full(N_EXPERTS, D_MODEL), full(N_EXPERTS, D_MODEL),
            full(D_MODEL, GATE_LANES), full(D_MODEL, GATE_LANES),
        ],
        out_specs=[row(D_MODEL), row(D_MODEL), pl.BlockSpec((N_EXPERTS, tm), lambda i: (0, i)),
                   row(GATE_LANES)],
        out_shape=[
            jax.ShapeDtypeStruct((t, D_MODEL), F32),
            jax.ShapeDtypeStruct((t, D_MODEL), BF16),
            jax.ShapeDtypeStruct((N_EXPERTS, t), F32),
            jax.ShapeDtypeStruct((t, GATE_LANES), BF16),
        ],
        scratch_shapes=[pltpu.VMEM((tm, D_MODEL), BF16)],
        compiler_params=_cparams(("arbitrary",)),
        name="out_proj",
    )(oaf, oab, obf, obb, main, main, x2d, wo, na, nb, ln2, wrh, wrl, wrh2, wrl2)


TOPK_LANES = 256


def _topk_kernel(p_ref, slot_ref, ts_ref, *, cap):
    n = p_ref.shape[1]
    n_steps = n // TOPK_LANES
    bits = pltpu.bitcast(p_ref[...], jnp.int32)

    def bisect(i, thr):
        cand = thr | jnp.left_shift(jnp.int32(1), 30 - i)
        cnt = jnp.sum(jnp.where(bits >= cand, 1.0, 0.0), axis=1, keepdims=True)
        return jnp.where(cnt >= cap, cand, thr)

    thr = lax.fori_loop(0, 31, bisect, jnp.zeros((N_EXPERTS, 1), jnp.int32))
    need = cap - jnp.sum(jnp.where(bits > thr, 1.0, 0.0), axis=1, keepdims=True)
    r = lax.broadcasted_iota(jnp.int32, (TOPK_LANES, TOPK_LANES), 0)
    c = lax.broadcasted_iota(jnp.int32, (TOPK_LANES, TOPK_LANES), 1)
    tri = jnp.where(r <= c, 1.0, 0.0).astype(BF16)
    step_lane = lax.broadcasted_iota(jnp.int32, (N_EXPERTS, n_steps), 1)

    ts_ref[...] = jnp.zeros_like(ts_ref)

    def step(j, carry):
        ties_before, sel_before = carry
        lanes = pl.ds(pl.multiple_of(j * TOPK_LANES, TOPK_LANES), TOPK_LANES)
        pb = pltpu.bitcast(p_ref[:, lanes], jnp.int32)
        tie = jnp.where(pb == thr, 1.0, 0.0)
        tie_incl = jnp.dot(tie.astype(BF16), tri, preferred_element_type=F32)
        sel = jnp.logical_or(pb > thr, jnp.logical_and(pb == thr, ties_before + tie_incl - tie < need))
        self_f = jnp.where(sel, 1.0, 0.0)
        sel_incl = jnp.dot(self_f.astype(BF16), tri, preferred_element_type=F32)
        pos = sel_before + sel_incl - self_f
        slot_ref[:, lanes] = jnp.where(sel, pos, -1.0).astype(jnp.int32)
        ts_ref[...] = jnp.where(step_lane == j, sel_before.astype(jnp.int32), ts_ref[...])
        return (ties_before + tie_incl[:, TOPK_LANES - 1:],
                sel_before + sel_incl[:, TOPK_LANES - 1:])

    zero = jnp.zeros((N_EXPERTS, 1), F32)
    lax.fori_loop(0, n_steps, step, (zero, zero))


def _topk(probs_t, cap):
    n = probs_t.shape[1]
    return pl.pallas_call(
        functools.partial(_topk_kernel, cap=cap),
        out_shape=[
            jax.ShapeDtypeStruct((N_EXPERTS, n), jnp.int32),
            jax.ShapeDtypeStruct((N_EXPERTS, n // TOPK_LANES), jnp.int32),
        ],
        compiler_params=pltpu.CompilerParams(vmem_limit_bytes=V7X_VMEM_LIMIT),
        name="topk",
    )(probs_t)


MOE_TILE = 1024
MOE_SUB = TOPK_LANES
MOE_SB = 512
MOE_RB = 128
CMB_TILE = 512
CMB_YB = 256
FL_VALID, FL_FIRST, FL_LAST = 1, 2, 4


def _moe_kernel(it_e, it_tile, it_g, it_s0, it_fl, it_ts, x_ref, slot_ref, p_ref,
                w1_ref, w3_ref, w2_ref, y_ref, xs_ref, g_ref):
    i = pl.program_id(0)
    fl = it_fl[i]
    e = it_e[i]
    s0 = it_s0[i]

    @pl.when((fl & FL_FIRST) != 0)
    def _():
        xs_ref[...] = jnp.zeros_like(xs_ref)
        g_ref[...] = jnp.zeros_like(g_ref)

    @pl.when((fl & FL_VALID) != 0)
    def _():
        srow = slot_ref[pl.ds(e, 1), :]
        prow = p_ref[pl.ds(e, 1), :]
        iota_s = lax.broadcasted_iota(jnp.int32, (MOE_RB, MOE_SUB), 0)
        for q in range(MOE_TILE // MOE_SUB):
            lo = jnp.maximum(it_ts[i * 5 + q], s0)
            hi = jnp.minimum(it_ts[i * 5 + q + 1], s0 + MOE_SB)

            @pl.when(hi > lo)
            def _():
                sq = srow[:, q * MOE_SUB:(q + 1) * MOE_SUB]
                pq = prow[:, q * MOE_SUB:(q + 1) * MOE_SUB]
                xq = x_ref[q * MOE_SUB:(q + 1) * MOE_SUB, :]

                def rows(r, carry):
                    roff = pl.multiple_of(r * MOE_RB, MOE_RB)
                    oh = (iota_s + (s0 + roff)) == sq
                    xs_ref[pl.ds(roff, MOE_RB), :] += jnp.dot(
                        jnp.where(oh, 1.0, 0.0).astype(BF16), xq, preferred_element_type=F32)
                    g_ref[pl.ds(roff, MOE_RB), :] += jnp.sum(jnp.where(oh, pq, 0.0), axis=1,
                                                            keepdims=True)
                    return carry

                lax.fori_loop((lo - s0) // MOE_RB, (hi - 1 - s0) // MOE_RB + 1, rows, 0)

    @pl.when((fl & FL_LAST) != 0)
    def _():
        xs = xs_ref[...].astype(BF16)
        h1 = jnp.dot(xs, w1_ref[0], preferred_element_type=F32)
        h3 = jnp.dot(xs, w3_ref[0], preferred_element_type=F32)
        hid = (_silu(h1) * h3).astype(BF16)
        y = jnp.dot(hid, w2_ref[0], preferred_element_type=F32) * g_ref[...]
        y_ref[...] = y.astype(y_ref.dtype)


def _moe_items(ts, cap):
    n_grp = ts.shape[1]
    per = MOE_TILE // MOE_SUB
    n_tile = n_grp // per
    n_sb = cap // MOE_SB
    tse = jnp.concatenate([ts, jnp.full((N_EXPERTS, 1), cap, jnp.int32)], axis=1)
    a = tse[:, 0:n_grp:per]
    b = tse[:, per::per]
    kf = a // MOE_SB
    cnt = jnp.where(b > a, (b - 1) // MOE_SB - kf + 1, 0).reshape(-1)
    incl = jnp.cumsum(cnt)
    total = incl[-1]
    n_items = N_EXPERTS * (n_tile + n_sb)
    idx = jnp.arange(n_items, dtype=jnp.int32)
    valid = idx < total
    idc = jnp.minimum(idx, total - 1)
    pair = jnp.searchsorted(incl, idc, side="right").astype(jnp.int32)
    r = idc - (incl[pair] - cnt[pair])
    e = pair // n_tile
    j = pair % n_tile
    k = kf.reshape(-1)[pair] + r
    s0 = k * MOE_SB
    aa = a.reshape(-1)[pair]
    bb = b.reshape(-1)[pair]
    first = jnp.logical_and(aa <= s0, s0 < bb)
    last = jnp.logical_and(aa <= s0 + MOE_SB - 1, s0 + MOE_SB - 1 < bb)
    fl = jnp.where(valid, FL_VALID + FL_FIRST * first + FL_LAST * last, 0).astype(jnp.int32)
    tsi = jnp.stack([tse[e, j * per + q] for q in range(per + 1)], axis=1).reshape(-1)
    return (e.astype(jnp.int32), j.astype(jnp.int32), (e * n_sb + k).astype(jnp.int32),
            s0.astype(jnp.int32), fl, tsi.astype(jnp.int32))


def _moe(hn, slot, probs_t, ts, w1, w3, w2, cap):
    n = hn.shape[0]
    items = _moe_items(ts, cap)
    n_items = items[0].shape[0]
    grid_spec = pltpu.PrefetchScalarGridSpec(
        num_scalar_prefetch=6,
        grid=(n_items,),
        in_specs=[
            pl.BlockSpec((MOE_TILE, D_MODEL), lambda i, e, t, g, s, f, ts: (t[i], 0)),
            pl.BlockSpec((N_EXPERTS, MOE_TILE), lambda i, e, t, g, s, f, ts: (0, t[i])),
            pl.BlockSpec((N_EXPERTS, MOE_TILE), lambda i, e, t, g, s, f, ts: (0, t[i])),
            pl.BlockSpec((1, D_MODEL, EXPERT_FF), lambda i, e, t, g, s, f, ts: (e[i], 0, 0)),
            pl.BlockSpec((1, D_MODEL, EXPERT_FF), lambda i, e, t, g, s, f, ts: (e[i], 0, 0)),
            pl.BlockSpec((1, EXPERT_FF, D_MODEL), lambda i, e, t, g, s, f, ts: (e[i], 0, 0)),
        ],
        out_specs=pl.BlockSpec((MOE_SB, D_MODEL), lambda i, e, t, g, s, f, ts: (g[i], 0)),
        scratch_shapes=[pltpu.VMEM((MOE_SB, D_MODEL), F32), pltpu.VMEM((MOE_SB, 1), F32)],
    )
    return pl.pallas_call(
        _moe_kernel,
        grid_spec=grid_spec,
        out_shape=jax.ShapeDtypeStruct((N_EXPERTS * cap, D_MODEL), BF16),
        compiler_params=_cparams(("arbitrary",)),
        name="moe_dispatch_mlp",
    )(*items, hn, slot, probs_t, w1, w3, w2)


def _combine_kernel(it_e, it_tile, it_yb, it_rel, it_fl, slot_ref, y_ref, h_ref, lnf_ref,
                    out_ref, acc_ref):
    i = pl.program_id(0)
    fl = it_fl[i]

    @pl.when((fl & FL_FIRST) != 0)
    def _():
        acc_ref[...] = jnp.zeros_like(acc_ref)

    @pl.when((fl & FL_VALID) != 0)
    def _():
        srow = slot_ref[pl.ds(it_e[i], 1), :]
        iota_s = lax.broadcasted_iota(jnp.int32, (CMB_YB, CMB_TILE), 0)
        oh = jnp.where((iota_s + it_rel[i]) == srow, 1.0, 0.0).astype(BF16)
        acc_ref[...] += _dot_tn(oh, y_ref[...])

    @pl.when((fl & FL_LAST) != 0)
    def _():
        hh = h_ref[...] + acc_ref[...]
        out_ref[...] = hh * lax.rsqrt(jnp.mean(hh * hh, axis=-1, keepdims=True) + EPS) * lnf_ref[...]


def _combine_items(ts, cap):
    n_grp = ts.shape[1]
    per = CMB_TILE // TOPK_LANES
    n_tile = n_grp // per
    tse = jnp.concatenate([ts, jnp.full((N_EXPERTS, 1), cap, jnp.int32)], axis=1)
    a = tse[:, 0:n_grp:per].T.reshape(-1)
    b = tse[:, per::per].T.reshape(-1)
    kf = jnp.minimum(a, cap - 1) // CMB_YB
    cnt = jnp.where(b > a, (b - 1) // CMB_YB - kf + 1, 1)
    incl = jnp.cumsum(cnt)
    total = incl[-1]
    n_items = n_tile * N_EXPERTS + N_EXPERTS * cap // CMB_YB
    idx = jnp.arange(n_items, dtype=jnp.int32)
    valid = idx < total
    idc = jnp.minimum(idx, total - 1)
    pair = jnp.searchsorted(incl, idc, side="right").astype(jnp.int32)
    r = idc - (incl[pair] - cnt[pair])
    tile = pair // N_EXPERTS
    e = pair % N_EXPERTS
    k = kf[pair] + r
    first = jnp.logical_and(e == 0, r == 0)
    last = jnp.logical_and(e == N_EXPERTS - 1, r == cnt[pair] - 1)
    fl = jnp.where(valid, FL_VALID + FL_FIRST * first + FL_LAST * last, 0).astype(jnp.int32)
    yb = e * (cap // CMB_YB) + k
    return (e.astype(jnp.int32), tile.astype(jnp.int32), yb.astype(jnp.int32),
            (k * CMB_YB).astype(jnp.int32), fl)


def _combine(y, slot, ts, h, ln_f, cap):
    n = h.shape[0]
    items = _combine_items(ts, cap)
    n_items = items[0].shape[0]
    grid_spec = pltpu.PrefetchScalarGridSpec(
        num_scalar_prefetch=5,
        grid=(n_items,),
        in_specs=[
            pl.BlockSpec((N_EXPERTS, CMB_TILE), lambda i, e, t, yb, rel, f: (0, t[i])),
            pl.BlockSpec((CMB_YB, D_MODEL), lambda i, e, t, yb, rel, f: (yb[i], 0)),
            pl.BlockSpec((CMB_TILE, D_MODEL), lambda i, e, t, yb, rel, f: (t[i], 0)),
            pl.BlockSpec((1, D_MODEL), lambda i, e, t, yb, rel, f: (0, 0)),
        ],
        out_specs=pl.BlockSpec((CMB_TILE, D_MODEL), lambda i, e, t, yb, rel, f: (t[i], 0)),
        scratch_shapes=[pltpu.VMEM((CMB_TILE, D_MODEL), F32)],
    )
    return pl.pallas_call(
        _combine_kernel,
        grid_spec=grid_spec,
        out_shape=jax.ShapeDtypeStruct((n, D_MODEL), F32),
        compiler_params=_cparams(("arbitrary",)),
        name="moe_combine",
    )(*items, slot, y, h, ln_f)


RT_TOK = TOPK_LANES
DSP_TOK = 2 * TOPK_LANES
RT_ROWS = 128
RT_SPECS = 2 * N_EXPERTS
RT_GROUP = 4
XS_WIDTH = D_MODEL + GATE_LANES
SF_VALID, SF_FIRST, SF_LAST = 1, 2, 4
BF_ACTIVE, BF_FIRST = 1, 2


def _route_steps(ts, cap, tok):
    ts = ts[:, ::tok // TOPK_LANES]
    n_sub = ts.shape[1]
    tse = jnp.concatenate([ts, jnp.full((N_EXPERTS, 1), cap, jnp.int32)], axis=1)
    a, b = tse[:, :-1], tse[:, 1:]
    kf = jnp.minimum(a, cap - 1) // RT_ROWS
    kl = jnp.where(b > a, (b - 1) // RT_ROWS, kf)
    rounds = jnp.max((kl - kf + 2) // 2, axis=0)
    incl = jnp.cumsum(rounds)
    total = incl[-1]
    n_steps = ((tok // RT_ROWS + 2) // 2) * n_sub
    idx = jnp.arange(n_steps, dtype=jnp.int32)
    valid = idx < total
    idc = jnp.minimum(idx, total - 1)
    j = jnp.searchsorted(incl, idc, side="right").astype(jnp.int32)
    r = idc - (incl[j] - rounds[j])
    k0 = (kf[:, j] + 2 * r)[:, :, None]
    par = jnp.arange(2, dtype=jnp.int32)[None, None, :]
    cand = k0 + (par - k0) % 2
    a_s, b_s, kl_s = a[:, j][:, :, None], b[:, j][:, :, None], kl[:, j][:, :, None]
    active = valid[None, :, None] & (cand <= kl_s) & (b_s > a_s)
    first = active & (a_s <= cand * RT_ROWS) & (cand * RT_ROWS < b_s)
    held = lax.cummax(jnp.where(active, cand, -1), axis=1)
    held = jnp.where(held < 0, par, held)
    order = lambda v: jnp.transpose(v, (1, 0, 2)).reshape(-1).astype(jnp.int32)
    bflag = BF_ACTIVE * active + BF_FIRST * first
    sflag = jnp.where(valid, SF_VALID + SF_FIRST * (r == 0) + SF_LAST * (r == rounds[j] - 1), 0)
    return (j, order(held), order(bflag), sflag.astype(jnp.int32)), total.astype(jnp.int32)


def _route_groups():
    return [[(RT_GROUP // 2 * g + q // 2, q % 2) for q in range(RT_GROUP)]
            for g in range(RT_SPECS // RT_GROUP)]


def _one_hots(i, blk, flg, slot_ref, specs):
    iota = lax.broadcasted_iota(jnp.int32, (RT_ROWS, slot_ref.shape[1]), 0)
    ohs = []
    for e, par in specs:
        k = i * RT_SPECS + e * 2 + par
        s0 = jnp.where((flg[k] & BF_ACTIVE) != 0, blk[k] * RT_ROWS, -(1 << 30))
        ohs.append(jnp.where((iota + s0) == slot_ref[e:e + 1, :], 1.0, 0.0).astype(BF16))
    return jnp.concatenate(ohs, axis=0)


def _dispatch_kernel(tile, blk, flg, sflg, x_ref, p3_ref, slot_ref, *outs):
    i = pl.program_id(0)

    @pl.when(i == 0)
    def _():
        for out in outs:
            out[...] = jnp.zeros_like(out)

    @pl.when((sflg[i] & SF_VALID) != 0)
    def _():
        xa = jnp.concatenate([x_ref[...], p3_ref[...]], axis=1)
        for specs in _route_groups():
            res = jnp.dot(_one_hots(i, blk, flg, slot_ref, specs), xa, preferred_element_type=F32)
            for q, (e, par) in enumerate(specs):
                out = outs[e * 2 + par]
                first = (flg[i * RT_SPECS + e * 2 + par] & BF_FIRST) != 0
                prev = out[...]
                prev = jnp.where(first, jnp.zeros_like(prev), prev)
                out[...] = prev + res[q * RT_ROWS:(q + 1) * RT_ROWS].astype(out.dtype)


def _dispatch(hn, p3, slot, steps, n_steps, cap):
    def out_spec(e, par):
        return pl.BlockSpec((RT_ROWS, XS_WIDTH),
                            lambda i, t, b, f, s: (b[i * RT_SPECS + e * 2 + par] // 2, 0))

    grid_spec = pltpu.PrefetchScalarGridSpec(
        num_scalar_prefetch=4,
        grid=(n_steps,),
        in_specs=[
            pl.BlockSpec((DSP_TOK, D_MODEL), lambda i, t, b, f, s: (t[i], 0)),
            pl.BlockSpec((DSP_TOK, GATE_LANES), lambda i, t, b, f, s: (t[i], 0)),
            pl.BlockSpec((N_EXPERTS, DSP_TOK), lambda i, t, b, f, s: (0, t[i])),
        ],
        out_specs=[out_spec(e, par) for e in range(N_EXPERTS) for par in range(2)],
    )
    sds = jax.ShapeDtypeStruct((cap // 2, XS_WIDTH), BF16)
    return pl.pallas_call(
        _dispatch_kernel,
        grid_spec=grid_spec,
        out_shape=[sds] * RT_SPECS,
        compiler_params=_cparams(("arbitrary",)),
        name="moe_dispatch",
    )(*steps, hn, p3, slot)


MLP_ROWS = 4 * RT_ROWS


def _expert_mlp_kernel(xe_ref, xo_ref, w1_ref, w3_ref, w2_ref, y_ref, *, e):
    r = RT_ROWS
    xa = jnp.concatenate([xe_ref[0:r], xo_ref[0:r], xe_ref[r:2 * r], xo_ref[r:2 * r]], axis=0)
    xs = xa[:, :D_MODEL]
    g3 = xa[:, D_MODEL:].astype(F32)
    lane = lax.broadcasted_iota(jnp.int32, g3.shape, 1)
    mine = jnp.logical_and((lane & (N_EXPERTS - 1)) == e, lane < 3 * N_EXPERTS)
    gate = jnp.sum(jnp.where(mine, g3, 0.0), axis=1, keepdims=True)
    h1 = jnp.dot(xs, w1_ref[0], preferred_element_type=F32)
    h3 = jnp.dot(xs, w3_ref[0], preferred_element_type=F32)
    hid = (_silu(h1) * h3).astype(BF16)
    y_ref[...] = (jnp.dot(hid, w2_ref[0], preferred_element_type=F32) * gate).astype(y_ref.dtype)


def _expert_mlp(xe, xo, w1, w3, w2, e, cap):
    half = pl.BlockSpec((MLP_ROWS // 2, XS_WIDTH), lambda m: (m, 0))
    return pl.pallas_call(
        functools.partial(_expert_mlp_kernel, e=e),
        grid=(cap // MLP_ROWS,),
        in_specs=[
            half, half,
            pl.BlockSpec((1, D_MODEL, EXPERT_FF), lambda m: (e, 0, 0)),
            pl.BlockSpec((1, D_MODEL, EXPERT_FF), lambda m: (e, 0, 0)),
            pl.BlockSpec((1, EXPERT_FF, D_MODEL), lambda m: (e, 0, 0)),
        ],
        out_specs=pl.BlockSpec((MLP_ROWS, D_MODEL), lambda m: (m, 0)),
        out_shape=jax.ShapeDtypeStruct((cap, D_MODEL), BF16),
        compiler_params=_cparams(("arbitrary",)),
        name="expert_mlp",
    )(xe, xo, w1, w3, w2)


CMB_WIN = 64
CMB_ALIGN = 16


def _combine_steps(ts, cap):
    n_sub = ts.shape[1]
    tse = jnp.concatenate([ts, jnp.full((N_EXPERTS, 1), cap, jnp.int32)], axis=1)
    a, b = tse[:, :-1], tse[:, 1:]
    a16 = (a // CMB_ALIGN) * CMB_ALIGN
    rounds = jnp.maximum(jnp.max((b - a16 + CMB_WIN - 1) // CMB_WIN, axis=0), 1)
    incl = jnp.cumsum(rounds)
    total = incl[-1]
    n_steps = ((RT_TOK + CMB_ALIGN) // CMB_WIN + 1) * n_sub
    idx = jnp.arange(n_steps, dtype=jnp.int32)
    valid = idx < total
    idc = jnp.minimum(idx, total - 1)
    j = jnp.searchsorted(incl, idc, side="right").astype(jnp.int32)
    r = idc - (incl[j] - rounds[j])
    base = a16[:, j] + CMB_WIN * r
    wstart = jnp.clip(base, 0, cap - CMB_WIN) // CMB_ALIGN
    order = lambda v: v.T.reshape(-1).astype(jnp.int32)
    sflag = jnp.where(valid, SF_VALID + SF_FIRST * (r == 0) + SF_LAST * (r == rounds[j] - 1), 0)
    return (j, order(wstart), order(base), sflag.astype(jnp.int32)), total.astype(jnp.int32)


def _combine3_kernel(tile, ws, base, sflg, slot_ref, h_ref, lnf_ref, *rest):
    ys = rest[:N_EXPERTS]
    out_ref, acc_ref = rest[N_EXPERTS], rest[N_EXPERTS + 1]
    i = pl.program_id(0)
    sf = sflg[i]

    @pl.when((sf & SF_FIRST) != 0)
    def _():
        acc_ref[...] = jnp.zeros_like(acc_ref)

    @pl.when((sf & SF_VALID) != 0)
    def _():
        iota = lax.broadcasted_iota(jnp.int32, (CMB_WIN, RT_TOK), 0)
        ohs = []
        for e in range(N_EXPERTS):
            k = i * N_EXPERTS + e
            rel = slot_ref[e:e + 1, :] - base[k]
            rel = jnp.where(rel >= 0, rel, -(1 << 30))
            shift = base[k] - ws[k] * CMB_ALIGN
            ohs.append(jnp.where((iota - shift) == rel, 1.0, 0.0).astype(BF16))
        ycat = jnp.concatenate([y[...] for y in ys], axis=0)
        acc_ref[...] += _dot_tn(jnp.concatenate(ohs, axis=0), ycat)

    @pl.when((sf & SF_LAST) != 0)
    def _():
        hh = h_ref[...] + acc_ref[...]
        out_ref[...] = hh * lax.rsqrt(jnp.mean(hh * hh, axis=-1, keepdims=True) + EPS) * lnf_ref[...]


def _combine3(ys, slot, steps, n_steps, h, ln_f):
    n = h.shape[0]

    def y_spec(e):
        return pl.BlockSpec((pl.Element(CMB_WIN), pl.Element(D_MODEL)),
                            lambda i, t, w, b, s: (w[i * N_EXPERTS + e] * CMB_ALIGN, 0))

    grid_spec = pltpu.PrefetchScalarGridSpec(
        num_scalar_prefetch=4,
        grid=(n_steps,),
        in_specs=[
            pl.BlockSpec((N_EXPERTS, RT_TOK), lambda i, t, w, b, s: (0, t[i])),
            pl.BlockSpec((RT_TOK, D_MODEL), lambda i, t, w, b, s: (t[i], 0)),
            pl.BlockSpec((1, D_MODEL), lambda i, t, w, b, s: (0, 0)),
        ] + [y_spec(e) for e in range(N_EXPERTS)],
        out_specs=pl.BlockSpec((RT_TOK, D_MODEL), lambda i, t, w, b, s: (t[i], 0)),
        scratch_shapes=[pltpu.VMEM((RT_TOK, D_MODEL), F32)],
    )
    return pl.pallas_call(
        _combine3_kernel,
        grid_spec=grid_spec,
        out_shape=jax.ShapeDtypeStruct((n, D_MODEL), F32),
        compiler_params=_cparams(("arbitrary",)),
        name="moe_combine",
    )(*steps, slot, h, ln_f, *ys)


def _combine2_kernel(tile, blk, flg, sflg, slot_ref, h_ref, lnf_ref, *rest):
    ys = rest[:RT_SPECS]
    out_ref, acc_ref = rest[RT_SPECS], rest[RT_SPECS + 1]
    i = pl.program_id(0)
    sf = sflg[i]

    @pl.when((sf & SF_FIRST) != 0)
    def _():
        acc_ref[...] = jnp.zeros_like(acc_ref)

    @pl.when((sf & SF_VALID) != 0)
    def _():
        total = None
        for specs in _route_groups():
            ycat = jnp.concatenate([ys[e * 2 + par][...] for e, par in specs], axis=0)
            t = _dot_tn(_one_hots(i, blk, flg, slot_ref, specs), ycat)
            total = t if total is None else total + t
        acc_ref[...] += total

    @pl.when((sf & SF_LAST) != 0)
    def _():
        hh = h_ref[...] + acc_ref[...]
        out_ref[...] = hh * lax.rsqrt(jnp.mean(hh * hh, axis=-1, keepdims=True) + EPS) * lnf_ref[...]


def _combine2(ys, slot, steps, n_steps, h, ln_f):
    n = h.shape[0]

    def y_spec(e, par):
        return pl.BlockSpec((RT_ROWS, D_MODEL),
                            lambda i, t, b, f, s: (b[i * RT_SPECS + e * 2 + par], 0))

    grid_spec = pltpu.PrefetchScalarGridSpec(
        num_scalar_prefetch=4,
        grid=(n_steps,),
        in_specs=[
            pl.BlockSpec((N_EXPERTS, RT_TOK), lambda i, t, b, f, s: (0, t[i])),
            pl.BlockSpec((RT_TOK, D_MODEL), lambda i, t, b, f, s: (t[i], 0)),
            pl.BlockSpec((1, D_MODEL), lambda i, t, b, f, s: (0, 0)),
        ] + [y_spec(e, par) for e in range(N_EXPERTS) for par in range(2)],
        out_specs=pl.BlockSpec((RT_TOK, D_MODEL), lambda i, t, b, f, s: (t[i], 0)),
        scratch_shapes=[pltpu.VMEM((RT_TOK, D_MODEL), F32)],
    )
    return pl.pallas_call(
        _combine2_kernel,
        grid_spec=grid_spec,
        out_shape=jax.ShapeDtypeStruct((n, D_MODEL), F32),
        compiler_params=_cparams(("arbitrary",)),
        name="moe_combine",
    )(*steps, slot, h, ln_f, *[ys[e] for e in range(N_EXPERTS) for _ in range(2)])

def _prep_in_weights(w_in, a_log_f, a_log_b, dt_bias_f, dt_bias_b):
    offs = np.cumsum([0, A_QK, A_QK, A_V, A_V, A_HEADS, A_HEADS, A_HEADS, A_HEADS,
                      B_QK, B_QK, B_V, B_V, GLA_LOWRANK, GLA_LOWRANK])
    seg = [w_in[:, offs[i]:offs[i + 1]] for i in range(14)]
    w_main = jnp.concatenate(seg[0:4] + seg[8:12], axis=1).astype(BF16)
    w_small = jnp.concatenate(seg[4:8] + seg[12:14], axis=1)
    w_small = jnp.pad(w_small, ((0, 0), (0, SMALL_WIDTH - w_small.shape[1]))).astype(BF16)
    pad = SMALL_WIDTH - 4 * A_HEADS
    z = jnp.zeros((2 * A_HEADS,), F32)
    a_row = jnp.concatenate([z, a_log_f.astype(F32), a_log_b.astype(F32), jnp.zeros((pad,), F32)])
    dt_row = jnp.concatenate([z, dt_bias_f.astype(F32), dt_bias_b.astype(F32), jnp.zeros((pad,), F32)])
    small_params = jnp.zeros((8, SMALL_WIDTH), F32).at[0].set(a_row).at[1].set(dt_row)
    return w_main, w_small, small_params


def kernel(x_prompt, x_sample, ln1, w_in, conv_w, a_log_f, a_log_b, dt_bias_f, dt_bias_b, norm_a, gla_w_f, gla_b_f, gla_w_b, gla_b_b, norm_b, w_out, ln2, w_router, w1, w3, w2, ln_f):
    w_main, w_small, small_params = _prep_in_weights(w_in[0], a_log_f[0], a_log_b[0],
                                                     dt_bias_f[0], dt_bias_b[0])
    conv8 = _conv_layout(conv_w[0])
    wg, bg = _gla_gate_weights(gla_w_f[0], gla_b_f[0], gla_w_b[0], gla_b_b[0])
    wl = jnp.asarray(_gla_level_weights(), BF16)
    wo = w_out[0].astype(BF16)
    wr_t = w_router[0].astype(F32).T
    wrh = wr_t.astype(BF16)
    wrl = (wr_t - wrh.astype(F32)).astype(BF16)
    wr_pad = jnp.pad(w_router[0].astype(F32), ((0, 0), (0, GATE_LANES - N_EXPERTS)))
    wrh2 = wr_pad.astype(BF16)
    wrl2 = (wr_pad - wrh2.astype(F32)).astype(BF16)
    w1b, w3b, w2b = w1[0].astype(BF16), w3[0].astype(BF16), w2[0].astype(BF16)
    row = lambda v: v.astype(F32).reshape(1, -1)
    outs = []
    for x in (x_prompt, x_sample):
        bsz, seq, _ = x.shape
        n = bsz * seq
        cap = EC_CAPACITY * n // N_EXPERTS
        x2d = x.reshape(n, D_MODEL)
        main, small = _in_proj(x2d, row(ln1[0]), w_main, w_small, small_params, tm=1024, tn=1792)
        main3 = main.reshape(bsz, seq, MAIN_WIDTH)
        small3 = small.reshape(bsz, seq, SMALL_WIDTH)
        oaf, oab = _gdn(_qkv_conv(main3, conv8, rows=SCAN_ROWS), _scal_layout(small3))
        obf, obb = _gla(main3, small3, wg, bg, wl)
        flat = lambda o: o.reshape(n, -1)
        h, hn, probs_t, p3 = _out_proj(flat(oaf), flat(oab), flat(obf), flat(obb), main, x2d, wo,
                                       row(norm_a[0]), row(norm_b[0]), row(ln2[0]),
                                       wrh, wrl, wrh2, wrl2, tm=256)
        slot, ts = _topk(probs_t, cap)
        dsp_steps, n_dsp = _route_steps(ts, cap, DSP_TOK)
        cmb_steps, n_cmb = _combine_steps(ts, cap)
        xs = _dispatch(hn, p3, slot, dsp_steps, n_dsp, cap)
        ys = [_expert_mlp(xs[2 * e], xs[2 * e + 1], w1b, w3b, w2b, e, cap) for e in range(N_EXPERTS)]
        out = _combine3(ys, slot, cmb_steps, n_cmb, h, row(ln_f))
        outs.append(out.reshape(bsz, seq, D_MODEL))
    return tuple(outs)
```

```python
import functools

import jax
import jax.numpy as jnp
import numpy as np
from jax import lax
from jax.experimental import pallas as pl
from jax.experimental.pallas import tpu as pltpu

F32 = jnp.float32
BF16 = jnp.bfloat16

D_MODEL = 2048
A_HEADS, A_DK, A_DV = 8, 128, 128
B_HEADS, B_DK, B_DV = 4, 128, 256
GLA_LOWRANK = 16
GLA_GATE_NORM = 16.0
CONV_K = 5
CHUNK = 64
N_EXPERTS = 16
EC_CAPACITY = 2
EXPERT_FF = D_MODEL // 2
EPS = 1e-6

A_QK = A_HEADS * A_DK
A_V = A_HEADS * A_DV
B_QK = B_HEADS * B_DK
B_V = B_HEADS * B_DV
MAIN_WIDTH = 2 * A_QK + 2 * A_V + 2 * B_QK + 2 * B_V
SMALL_WIDTH = 128
LR_OFF = 4 * A_HEADS
V7X_VMEM_LIMIT = 56 * 1024 * 1024
CONV_HALO = 16
SCAN_ROWS = 256
CPS = SCAN_ROWS // CHUNK
GLA_LEVELS = 6
GATE_LANES = 128
GDN_HEADS_PER_PASS = 8


def _cparams(sem):
    return pltpu.CompilerParams(dimension_semantics=sem, vmem_limit_bytes=V7X_VMEM_LIMIT)


def _dot(a, b):
    return jnp.dot(a.astype(BF16), b.astype(BF16), preferred_element_type=F32)


def _dot_nt(a, b):
    return lax.dot_general(a.astype(BF16), b.astype(BF16), (((1,), (1,)), ((), ())),
                           preferred_element_type=F32)


def _dot_tn(a, b):
    return lax.dot_general(a.astype(BF16), b.astype(BF16), (((0,), (0,)), ((), ())),
                           preferred_element_type=F32)


def _silu(x):
    return x * (1.0 / (1.0 + jnp.exp(-x)))


def _in_proj_kernel(x_ref, ln_ref, wm_ref, ws_ref, sp_ref, main_ref, small_ref, hn_ref):
    j = pl.program_id(1)

    @pl.when(j == 0)
    def _():
        xf = x_ref[...]
        y = xf * lax.rsqrt(jnp.mean(xf * xf, axis=-1, keepdims=True) + EPS) * ln_ref[...]
        hn = y.astype(BF16)
        hn_ref[...] = hn
        s = jnp.dot(hn, ws_ref[...], preferred_element_type=F32)
        lane = lax.broadcasted_iota(jnp.int32, s.shape, 1)
        neg_a = -jnp.exp(sp_ref[0:1, :])
        z = s + sp_ref[1:2, :]
        softplus = jnp.maximum(z, 0.0) + jnp.log(1.0 + jnp.exp(-jnp.abs(z)))
        sig = 1.0 / (1.0 + jnp.exp(-s))
        small_ref[...] = jnp.where(lane < 2 * A_HEADS, sig,
                                   jnp.where(lane < 4 * A_HEADS, neg_a * softplus, s))

    main_ref[...] = jnp.dot(hn_ref[...], wm_ref[...], preferred_element_type=F32).astype(BF16)


def _in_proj(x2d, ln1, w_main, w_small, small_params, tm, tn):
    t = x2d.shape[0]
    return pl.pallas_call(
        _in_proj_kernel,
        grid=(t // tm, MAIN_WIDTH // tn),
        in_specs=[
            pl.BlockSpec((tm, D_MODEL), lambda i, j: (i, 0)),
            pl.BlockSpec((1, D_MODEL), lambda i, j: (0, 0)),
            pl.BlockSpec((D_MODEL, tn), lambda i, j: (0, j)),
            pl.BlockSpec((D_MODEL, SMALL_WIDTH), lambda i, j: (0, 0)),
            pl.BlockSpec((8, SMALL_WIDTH), lambda i, j: (0, 0)),
        ],
        out_specs=[
            pl.BlockSpec((tm, tn), lambda i, j: (i, j)),
            pl.BlockSpec((tm, SMALL_WIDTH), lambda i, j: (i, 0)),
        ],
        out_shape=[
            jax.ShapeDtypeStruct((t, MAIN_WIDTH), BF16),
            jax.ShapeDtypeStruct((t, SMALL_WIDTH), F32),
        ],
        scratch_shapes=[pltpu.VMEM((tm, D_MODEL), BF16)],
        compiler_params=_cparams(("arbitrary", "arbitrary")),
        name="in_proj",
    )(x2d, ln1, w_main, w_small, small_params)


def _chunk_iotas():
    ii = lax.broadcasted_iota(jnp.int32, (CHUNK, CHUNK), 0)
    jj = lax.broadcasted_iota(jnp.int32, (CHUNK, CHUNK), 1)
    return ii, jj


def _row_to_col(row, eye):
    return jnp.sum(jnp.where(eye, row, 0.0), axis=1, keepdims=True)


def _col_to_row(col, eye):
    return jnp.sum(jnp.where(eye, col, 0.0), axis=0, keepdims=True)


def _l2norm(x):
    return x * lax.rsqrt(jnp.sum(x * x, axis=-1, keepdims=True) + EPS)


def _delta_chunks(qs, ks, vs, beta_rows, g_rows, states, revs):
    ii, jj = _chunk_iotas()
    eye = ii == jj
    nc = range(len(qs))
    incl = [(ii <= jj) if r else (ii >= jj) for r in revs]
    strict = [(ii < jj) if r else (ii > jj) for r in revs]
    gc_col = [jnp.sum(jnp.where(incl[c], g_rows[c], 0.0), axis=1, keepdims=True) for c in nc]
    gc_row = [_col_to_row(gc_col[c], eye) for c in nc]
    beta_col = [_row_to_col(beta_rows[c], eye) for c in nc]
    g_tot = [jnp.sum(g_rows[c], axis=1, keepdims=True) for c in nc]
    decay = [jnp.where(incl[c], jnp.exp(jnp.where(incl[c], gc_col[c] - gc_row[c], 0.0)), 0.0)
             for c in nc]
    kb = [ks[c] * beta_col[c] for c in nc]
    kq = [_dot_nt(jnp.concatenate([kb[c], qs[c]], axis=0), ks[c]) for c in nc]
    kk = [kq[c][:CHUNK] for c in nc]
    qk = [kq[c][CHUNK:] for c in nc]
    p = [jnp.where(strict[c], -kk[c] * decay[c], 0.0) for c in nc]
    toff = p
    p = [_dot(p[c], p[c]) for c in nc]
    for _ in range(4):
        r = [_dot(jnp.concatenate([p[c], toff[c]], axis=0), p[c]) for c in nc]
        toff = [toff[c] + p[c] + r[c][CHUNK:] for c in nc]
        p = [r[c][:CHUNK] for c in nc]
    tp = [_dot(toff[c], p[c]) for c in nc]
    toff = [toff[c] + p[c] + tp[c] for c in nc]
    e_gc = [jnp.exp(gc_col[c]) for c in nc]
    rhs = [jnp.concatenate([vs[c] * beta_col[c], kb[c] * e_gc[c]], axis=1) for c in nc]
    sol = [rhs[c] + _dot(toff[c], rhs[c]) for c in nc]
    attn = [jnp.where(incl[c], qk[c] * decay[c], 0.0) for c in nc]
    wq = [_dot(jnp.concatenate([sol[c][:, A_DV:], qs[c] * e_gc[c]], axis=0), states[c]) for c in nc]
    ws = [wq[c][:CHUNK] for c in nc]
    qs_state = [wq[c][CHUNK:] for c in nc]
    v_new = [sol[c][:, :A_DV] - ws[c] for c in nc]
    av = [_dot(attn[c], v_new[c]) for c in nc]
    kv = [_dot_tn(ks[c] * jnp.exp(g_tot[c] - gc_col[c]), v_new[c]) for c in nc]
    outs = [qs_state[c] + av[c] for c in nc]
    new_states = [states[c] * jnp.exp(g_tot[c]) + kv[c] for c in nc]
    return outs, new_states


def _qkv_conv_kernel(blk, prv, nxt, cw_ref, out_ref, pad_ref):
    n = pl.program_id(1)
    nb = pl.num_programs(1)
    rows = blk.shape[1]
    pad_ref[0:CONV_HALO, :] = jnp.where(n == 0, 0.0, prv[0].astype(F32))
    pad_ref[CONV_HALO:CONV_HALO + rows, :] = blk[0].astype(F32)
    pad_ref[CONV_HALO + rows:, :] = jnp.where(n == nb - 1, 0.0, nxt[0].astype(F32))
    lo = CONV_HALO - CONV_K // 2
    for g in range((2 * A_QK + A_V) // A_DK):
        cols = slice(g * A_DK, (g + 1) * A_DK)
        w = cw_ref[:, cols]
        acc = w[0:1, :] * pad_ref[lo:lo + rows, cols]
        for j in range(1, CONV_K):
            acc = acc + w[j:j + 1, :] * pad_ref[lo + j:lo + j + rows, cols]
        y = _silu(acc)
        if g < A_HEADS:
            y = _l2norm(y) * (A_DK ** -0.5)
        elif g < 2 * A_HEADS:
            y = _l2norm(y)
        out_ref[0, :, cols] = y.astype(out_ref.dtype)


def _qkv_conv(main3, conv8, rows):
    bsz, seq, _ = main3.shape
    width = 2 * A_QK + A_V
    hb = rows // CONV_HALO
    nhalo = seq // CONV_HALO
    return pl.pallas_call(
        _qkv_conv_kernel,
        grid=(bsz, seq // rows),
        in_specs=[
            pl.BlockSpec((1, rows, width), lambda b, n: (b, n, 0)),
            pl.BlockSpec((1, CONV_HALO, width), lambda b, n: (b, jnp.maximum(n * hb - 1, 0), 0)),
            pl.BlockSpec((1, CONV_HALO, width),
                         lambda b, n: (b, jnp.minimum((n + 1) * hb, nhalo - 1), 0)),
            pl.BlockSpec((8, width), lambda b, n: (0, 0)),
        ],
        out_specs=pl.BlockSpec((1, rows, width), lambda b, n: (b, n, 0)),
        out_shape=jax.ShapeDtypeStruct((bsz, seq, width), BF16),
        scratch_shapes=[pltpu.VMEM((rows + 2 * CONV_HALO, width), F32)],
        compiler_params=_cparams(("arbitrary", "arbitrary")),
        name="qkv_conv",
    )(main3, main3, main3, conv8)


def _gdn_kernel(blk_f, blk_b, sc_f, sc_b, of_ref, ob_ref, st_ref):
    n = pl.program_id(1)

    @pl.when(n == 0)
    def _():
        st_ref[...] = jnp.zeros_like(st_ref)

    def load(d, row0, a, h):
        blk = blk_f if d == 0 else blk_b
        col = a * A_QK + h * A_DK
        return blk[0, pl.ds(row0, CHUNK), col:col + A_DK].astype(F32)

    def body(s, carry):
        for h0 in range(0, A_HEADS, GDN_HEADS_PER_PASS):
            chains = []
            for d, sc_ref in enumerate((sc_f, sc_b)):
                sc = s if d == 0 else CPS - 1 - s
                row0 = pl.multiple_of(sc * CHUNK, CHUNK)
                for h in range(h0, h0 + GDN_HEADS_PER_PASS):
                    chains.append((d, h, sc, row0, sc_ref))
            qs = [load(d, row0, 0, h) for d, h, sc, row0, _ in chains]
            ks = [load(d, row0, 1, h) for d, h, sc, row0, _ in chains]
            vs = [load(d, row0, 2, h) for d, h, sc, row0, _ in chains]
            betas = [r[0, 0, d, h, pl.ds(sc, 1), :] for d, h, sc, row0, r in chains]
            gs = [r[0, 0, 2 + d, h, pl.ds(sc, 1), :] for d, h, sc, row0, r in chains]
            states = [st_ref[d, h] for d, h, sc, row0, _ in chains]
            outs, new_states = _delta_chunks(qs, ks, vs, betas, gs, states,
                                             [d == 1 for d, *_ in chains])
            for (d, h, sc, row0, _), o, st in zip(chains, outs, new_states):
                st_ref[d, h] = st
                out_ref = of_ref if d == 0 else ob_ref
                out_ref[0, pl.ds(row0, CHUNK), h * A_DV:(h + 1) * A_DV] = o.astype(out_ref.dtype)
        return carry

    lax.fori_loop(0, CPS, body, 0)


def _scal_layout(small):
    bsz, seq = small.shape[:2]
    s = small[..., :4 * A_HEADS].reshape(bsz, seq // SCAN_ROWS, CPS, CHUNK, 4, A_HEADS)
    return jnp.transpose(s, (0, 1, 4, 5, 2, 3))


def _conv_layout(conv_w):
    return jnp.pad(conv_w.astype(F32), ((0, 8 - CONV_K), (0, 0)))


def _gdn(qkv, scal):
    bsz, seq, width = qkv.shape
    nb = seq // SCAN_ROWS

    def specs(blk_of):
        return (pl.BlockSpec((1, SCAN_ROWS, width), lambda b, n: (b, blk_of(n), 0)),
                pl.BlockSpec((1, 1, 4, A_HEADS, CPS, CHUNK), lambda b, n: (b, blk_of(n), 0, 0, 0, 0)))

    qf, sf = specs(lambda n: n)
    qb, sb = specs(lambda n: nb - 1 - n)
    out_sds = jax.ShapeDtypeStruct((bsz, seq, A_V), BF16)
    return pl.pallas_call(
        _gdn_kernel,
        grid=(bsz, nb),
        in_specs=[qf, qb, sf, sb],
        out_specs=[
            pl.BlockSpec((1, SCAN_ROWS, A_V), lambda b, n: (b, n, 0)),
            pl.BlockSpec((1, SCAN_ROWS, A_V), lambda b, n: (b, nb - 1 - n, 0)),
        ],
        out_shape=[out_sds, out_sds],
        scratch_shapes=[pltpu.VMEM((2, A_HEADS, A_DK, A_DV), F32)],
        compiler_params=_cparams(("arbitrary", "arbitrary")),
        name="gdn",
    )(qkv, qkv, scal, scal)


GLA_MM_LEVELS = (4, 5)


def _gla_level_weights():
    c = CHUNK
    w = np.zeros((2, 3 * c, c), np.float32)
    for d in range(2):
        for blk, l in enumerate(GLA_MM_LEVELS):
            s = c >> (l + 1)
            for i in range(c):
                mid = (i // (2 * s)) * 2 * s + s
                if d == 0:
                    ts = range(mid, i + 1) if i >= mid else range(i + 1, mid)
                else:
                    ts = range(i, mid) if i < mid else range(mid, i)
                for t in ts:
                    w[d, blk * c + i, t] = 1.0
        for i in range(c):
            for t in range(c):
                before = t <= i if d == 0 else t >= i
                w[d, 2 * c + i, t] = 1.0 if before else 0.0
    return w


def _gla_kernel(qk_f, v_f, sm_f, qk_b, v_b, sm_b, wg_ref, bg_ref, wl_ref, of_ref, ob_ref, st_ref):
    n = pl.program_id(1)

    @pl.when(n == 0)
    def _():
        st_ref[...] = jnp.zeros_like(st_ref)

    ii, jj = _chunk_iotas()
    eye = ii == jj
    tok = lax.broadcasted_iota(jnp.int32, (CHUNK, 1), 0)

    streams = ((qk_f, v_f, sm_f, of_ref), (qk_b, v_b, sm_b, ob_ref))
    dirs = range(2)
    chains = [(d, h) for d in dirs for h in range(B_HEADS)]
    masks = [ii // (CHUNK >> l) == jj // (CHUNK >> l) for l in range(GLA_LEVELS)]

    def body(s, carry):
        rows = [pl.ds(pl.multiple_of((s if d == 0 else CPS - 1 - s) * CHUNK, CHUNK), CHUNK)
                for d in dirs]
        z = [_dot(streams[d][2][0, rows[d], :], wg_ref[d]) + bg_ref[d] for d in dirs]
        la = [(jnp.minimum(z[d], 0.0) - jnp.log(1.0 + jnp.exp(-jnp.abs(z[d]))))
              * (1.0 / GLA_GATE_NORM) for d in dirs]
        hi = [la[d].astype(BF16) for d in dirs]
        r1 = [la[d] - hi[d].astype(F32) for d in dirs]
        mid = [r1[d].astype(BF16) for d in dirs]
        low = [(r1[d] - mid[d].astype(F32)).astype(BF16) for d in dirs]
        ex = [jnp.dot(wl_ref[d], hi[d], preferred_element_type=F32)
              + jnp.dot(wl_ref[d], mid[d], preferred_element_type=F32)
              + jnp.dot(wl_ref[d], low[d], preferred_element_type=F32) for d in dirs]
        bcum = [ex[d][2 * CHUNK:] for d in dirs]
        b_tot = [bcum[0][CHUNK - 1:CHUNK], bcum[1][0:1]]
        q_all = [streams[d][0][0, rows[d], :B_QK].astype(F32) * (B_DK ** -0.5) for d in dirs]
        k_all = [streams[d][0][0, rows[d], B_QK:].astype(F32) for d in dirs]
        qs, ks = [], []
        for l in range(GLA_LEVELS):
            half = CHUNK >> (l + 1)
            right = (tok // half) % 2 == 1
            q_side = [right, jnp.logical_not(right)]
            if l in GLA_MM_LEVELS:
                blk = GLA_MM_LEVELS.index(l)
                expo = [ex[d][blk * CHUNK:(blk + 1) * CHUNK] for d in dirs]
            else:
                expo = []
                for d in dirs:
                    ref = jnp.concatenate(
                        [jnp.broadcast_to(bcum[d][g + half - 1 + d:g + half + d], (2 * half, B_QK))
                         for g in range(0, CHUNK, 2 * half)], axis=0)
                    expo.append(jnp.where(q_side[d], bcum[d] - ref, ref - bcum[d]))
            e_l = [jnp.exp(expo[d]) for d in dirs]
            qs.append([jnp.where(q_side[d], q_all[d] * e_l[d], 0.0).astype(BF16) for d in dirs])
            ks.append([jnp.where(q_side[d], 0.0, k_all[d] * e_l[d]).astype(BF16) for d in dirs])
        q_dec = [(q_all[d] * jnp.exp(bcum[d])).astype(BF16) for d in dirs]
        k_dec = [(k_all[d] * jnp.exp(b_tot[d] - bcum[d])).astype(BF16) for d in dirs]
        e_tot = [jnp.exp(b_tot[d]) for d in dirs]
        ck = [slice(h * B_DK, (h + 1) * B_DK) for h in range(B_HEADS)]
        cv = [slice(h * B_DV, (h + 1) * B_DV) for h in range(B_HEADS)]
        attn = [jnp.where(eye, _dot_nt(q_all[d][:, ck[h]], k_all[d][:, ck[h]]), 0.0)
                for d, h in chains]
        for l in range(GLA_LEVELS):
            part = [_dot_nt(qs[l][d][:, ck[h]], ks[l][d][:, ck[h]]) for d, h in chains]
            attn = [attn[c] + jnp.where(masks[l], part[c], 0.0) for c in range(len(chains))]
        v = [streams[d][1][0, rows[d], cv[h]] for d, h in chains]
        st = [st_ref[d, h] for d, h in chains]
        o_state = [_dot_nt(q_dec[d][:, ck[h]], st[c]) for c, (d, h) in enumerate(chains)]
        o_local = [_dot(attn[c], v[c]) for c in range(len(chains))]
        kv = [_dot_tn(v[c], k_dec[d][:, ck[h]]) for c, (d, h) in enumerate(chains)]
        for c, (d, h) in enumerate(chains):
            st_ref[d, h] = st[c] * e_tot[d][:, ck[h]] + kv[c]
            streams[d][3][0, rows[d], cv[h]] = (o_state[c] + o_local[c]).astype(of_ref.dtype)
        return carry

    lax.fori_loop(0, CPS, body, 0)


def _gla(main3, small3, wg, bg, wl):
    bsz, seq, _ = main3.shape
    nb = seq // SCAN_ROWS
    qk_blk = (2 * A_QK + 2 * A_V) // (2 * B_QK)
    v_blk = (2 * A_QK + 2 * A_V + 2 * B_QK) // B_V

    def stream(blk_of):
        return [
            pl.BlockSpec((1, SCAN_ROWS, 2 * B_QK), lambda b, n: (b, blk_of(n), qk_blk)),
            pl.BlockSpec((1, SCAN_ROWS, B_V), lambda b, n: (b, blk_of(n), v_blk)),
            pl.BlockSpec((1, SCAN_ROWS, SMALL_WIDTH), lambda b, n: (b, blk_of(n), 0)),
        ]

    out_sds = jax.ShapeDtypeStruct((bsz, seq, B_V), BF16)
    return pl.pallas_call(
        _gla_kernel,
        grid=(bsz, nb),
        in_specs=stream(lambda n: n) + stream(lambda n: nb - 1 - n) + [
            pl.BlockSpec((2, SMALL_WIDTH, B_QK), lambda b, n: (0, 0, 0)),
            pl.BlockSpec((2, 1, B_QK), lambda b, n: (0, 0, 0)),
            pl.BlockSpec((2, 3 * CHUNK, CHUNK), lambda b, n: (0, 0, 0)),
        ],
        out_specs=[
            pl.BlockSpec((1, SCAN_ROWS, B_V), lambda b, n: (b, n, 0)),
            pl.BlockSpec((1, SCAN_ROWS, B_V), lambda b, n: (b, nb - 1 - n, 0)),
        ],
        out_shape=[out_sds, out_sds],
        scratch_shapes=[pltpu.VMEM((2, B_HEADS, B_DV, B_DK), F32)],
        compiler_params=_cparams(("arbitrary", "arbitrary")),
        name="gla",
    )(main3, main3, small3, main3, main3, small3, wg, bg, wl)


def _gla_gate_weights(gla_w_f, gla_b_f, gla_w_b, gla_b_b):
    wg = jnp.zeros((2, SMALL_WIDTH, B_QK), F32)
    wg = wg.at[0, LR_OFF:LR_OFF + GLA_LOWRANK].set(gla_w_f.astype(F32))
    wg = wg.at[1, LR_OFF + GLA_LOWRANK:LR_OFF + 2 * GLA_LOWRANK].set(gla_w_b.astype(F32))
    bg = jnp.stack([gla_b_f, gla_b_b]).astype(F32).reshape(2, 1, B_QK)
    return wg.astype(BF16), bg


def _out_proj_kernel(oaf, oab, obf, obb, ga, gb, x_ref, wo_ref, na_ref, nb_ref, ln2_ref,
                     wrh_ref, wrl_ref, wrh2_ref, wrl2_ref, h_ref, hn_ref, pt_ref, p3_ref, mix_ref):
    def head_norm(rs, of_ref, ob_ref, g_ref, w_ref, width, heads, base):
        for h in range(heads):
            c = slice(h * width, (h + 1) * width)
            o = of_ref[rs, c].astype(F32) + ob_ref[rs, c].astype(F32)
            y = o * lax.rsqrt(jnp.mean(o * o, axis=-1, keepdims=True) + EPS) * w_ref[...]
            mix_ref[rs, base + h * width:base + (h + 1) * width] = (
                y * _silu(g_ref[rs, c].astype(F32))).astype(BF16)

    def mix(rs):
        head_norm(rs, oaf, oab, ga, na_ref, A_DV, A_HEADS, 0)
        head_norm(rs, obf, obb, gb, nb_ref, B_DV, B_HEADS, A_V)

    def project(rs):
        return x_ref[rs, :] + jnp.dot(mix_ref[rs, :], wo_ref[...], preferred_element_type=F32)

    def route(rs, hres):
        h_ref[rs, :] = hres
        hn = hres * lax.rsqrt(jnp.mean(hres * hres, axis=-1, keepdims=True) + EPS) * ln2_ref[...]
        hi = hn.astype(BF16)
        hn_ref[rs, :] = hi
        lo = (hn - hi.astype(F32)).astype(BF16)
        lt = _dot_nt(wrh_ref[...], hi) + _dot_nt(wrl_ref[...], hi) + _dot_nt(wrh_ref[...], lo)
        e = jnp.exp(lt - jnp.max(lt, axis=0, keepdims=True))
        pt_ref[:, rs] = e / jnp.sum(e, axis=0, keepdims=True)
        l2 = _dot(hi, wrh2_ref[...]) + _dot(hi, wrl2_ref[...]) + _dot(lo, wrh2_ref[...])
        lane = lax.broadcasted_iota(jnp.int32, l2.shape, 1)
        l2 = jnp.where(lane < N_EXPERTS, l2, -jnp.inf)
        e2 = jnp.exp(l2 - jnp.max(l2, axis=1, keepdims=True))
        p2 = e2 / jnp.sum(e2, axis=1, keepdims=True)
        g_hi = p2.astype(BF16)
        r1 = p2 - g_hi.astype(F32)
        g_mid = r1.astype(BF16)
        g_lo = (r1 - g_mid.astype(F32)).astype(BF16)
        p3 = (g_hi.astype(F32) + pltpu.roll(g_mid.astype(F32), N_EXPERTS, 1)
              + pltpu.roll(g_lo.astype(F32), 2 * N_EXPERTS, 1))
        p3_ref[rs, :] = p3.astype(BF16)

    half = x_ref.shape[0] // 2
    r0, r1_ = slice(0, half), slice(half, 2 * half)
    mix(r0)
    h0 = project(r0)
    mix(r1_)
    h1 = project(r1_)
    route(r0, h0)
    route(r1_, h1)


def _out_proj(oaf, oab, obf, obb, main, x2d, wo, na, nb, ln2, wrh, wrl, wrh2, wrl2, tm):
    t = x2d.shape[0]
    ga_blk = (2 * A_QK + A_V) // A_V
    gb_blk = (2 * A_QK + 2 * A_V + 2 * B_QK + B_V) // B_V
    row = lambda w: pl.BlockSpec((tm, w), lambda i: (i, 0))
    full = lambda a, b: pl.BlockSpec((a, b), lambda i: (0, 0))
    return pl.pallas_call(
        _out_proj_kernel,
        grid=(t // tm,),
        in_specs=[
            row(A_V), row(A_V), row(B_V), row(B_V),
            pl.BlockSpec((tm, A_V), lambda i: (i, ga_blk)),
            pl.BlockSpec((tm, B_V), lambda i: (i, gb_blk)),
            row(D_MODEL), full(D_MODEL, D_MODEL), full(1, A_DV), full(1, B_DV), full(1, D_MODEL),
            full(N_EXPERTS, D_MODEL), full(N_EXPERTS, D_MODEL),
            full(D_MODEL, GATE_LANES), full(D_MODEL, GATE_LANES),
        ],
        out_specs=[row(D_MODEL), row(D_MODEL), pl.BlockSpec((N_EXPERTS, tm), lambda i: (0, i)),
                   row(GATE_LANES)],
        out_shape=[
            jax.ShapeDtypeStruct((t, D_MODEL), F32),
            jax.ShapeDtypeStruct((t, D_MODEL), BF16),
            jax.ShapeDtypeStruct((N_EXPERTS, t), F32),
            jax.ShapeDtypeStruct((t, GATE_LANES), BF16),
        ],
        scratch_shapes=[pltpu.VMEM((tm, D_MODEL), BF16)],
        compiler_params=_cparams(("arbitrary",)),
        name="out_proj",
    )(oaf, oab, obf, obb, main, main, x2d, wo, na, nb, ln2, wrh, wrl, wrh2, wrl2)


TOPK_LANES = 256


def _topk_kernel(p_ref, slot_ref, ts_ref, *, cap):
    n = p_ref.shape[1]
    n_steps = n // TOPK_LANES
    bits = pltpu.bitcast(p_ref[...], jnp.int32)

    def bisect(i, thr):
        cand = thr | jnp.left_shift(jnp.int32(1), 30 - i)
        cnt = jnp.sum(jnp.where(bits >= cand, 1.0, 0.0), axis=1, keepdims=True)
        return jnp.where(cnt >= cap, cand, thr)

    thr = lax.fori_loop(0, 31, bisect, jnp.zeros((N_EXPERTS, 1), jnp.int32))
    need = cap - jnp.sum(jnp.where(bits > thr, 1.0, 0.0), axis=1, keepdims=True)
    r = lax.broadcasted_iota(jnp.int32, (TOPK_LANES, TOPK_LANES), 0)
    c = lax.broadcasted_iota(jnp.int32, (TOPK_LANES, TOPK_LANES), 1)
    tri = jnp.where(r <= c, 1.0, 0.0).astype(BF16)
    step_lane = lax.broadcasted_iota(jnp.int32, (N_EXPERTS, n_steps), 1)

    ts_ref[...] = jnp.zeros_like(ts_ref)

    def step(j, carry):
        ties_before, sel_before = carry
        lanes = pl.ds(pl.multiple_of(j * TOPK_LANES, TOPK_LANES), TOPK_LANES)
        pb = pltpu.bitcast(p_ref[:, lanes], jnp.int32)
        tie = jnp.where(pb == thr, 1.0, 0.0)
        tie_incl = jnp.dot(tie.astype(BF16), tri, preferred_element_type=F32)
        sel = jnp.logical_or(pb > thr, jnp.logical_and(pb == thr, ties_before + tie_incl - tie < need))
        self_f = jnp.where(sel, 1.0, 0.0)
        sel_incl = jnp.dot(self_f.astype(BF16), tri, preferred_element_type=F32)
        pos = sel_before + sel_incl - self_f
        slot_ref[:, lanes] = jnp.where(sel, pos, -1.0).astype(jnp.int32)
        ts_ref[...] = jnp.where(step_lane == j, sel_before.astype(jnp.int32), ts_ref[...])
        return (ties_before + tie_incl[:, TOPK_LANES - 1:],
                sel_before + sel_incl[:, TOPK_LANES - 1:])

    zero = jnp.zeros((N_EXPERTS, 1), F32)
    lax.fori_loop(0, n_steps, step, (zero, zero))


def _topk(probs_t, cap):
    n = probs_t.shape[1]
    return pl.pallas_call(
        functools.partial(_topk_kernel, cap=cap),
        out_shape=[
            jax.ShapeDtypeStruct((N_EXPERTS, n), jnp.int32),
            jax.ShapeDtypeStruct((N_EXPERTS, n // TOPK_LANES), jnp.int32),
        ],
        compiler_params=pltpu.CompilerParams(vmem_limit_bytes=V7X_VMEM_LIMIT),
        name="topk",
    )(probs_t)


MOE_TILE = 1024
MOE_SUB = TOPK_LANES
MOE_SB = 512
MOE_RB = 128
CMB_TILE = 512
CMB_YB = 256
FL_VALID, FL_FIRST, FL_LAST = 1, 2, 4


def _moe_kernel(it_e, it_tile, it_g, it_s0, it_fl, it_ts, x_ref, slot_ref, p_ref,
                w1_ref, w3_ref, w2_ref, y_ref, xs_ref, g_ref):
    i = pl.program_id(0)
    fl = it_fl[i]
    e = it_e[i]
    s0 = it_s0[i]

    @pl.when((fl & FL_FIRST) != 0)
    def _():
        xs_ref[...] = jnp.zeros_like(xs_ref)
        g_ref[...] = jnp.zeros_like(g_ref)

    @pl.when((fl & FL_VALID) != 0)
    def _():
        srow = slot_ref[pl.ds(e, 1), :]
        prow = p_ref[pl.ds(e, 1), :]
        iota_s = lax.broadcasted_iota(jnp.int32, (MOE_RB, MOE_SUB), 0)
        for q in range(MOE_TILE // MOE_SUB):
            lo = jnp.maximum(it_ts[i * 5 + q], s0)
            hi = jnp.minimum(it_ts[i * 5 + q + 1], s0 + MOE_SB)

            @pl.when(hi > lo)
            def _():
                sq = srow[:, q * MOE_SUB:(q + 1) * MOE_SUB]
                pq = prow[:, q * MOE_SUB:(q + 1) * MOE_SUB]
                xq = x_ref[q * MOE_SUB:(q + 1) * MOE_SUB, :]

                def rows(r, carry):
                    roff = pl.multiple_of(r * MOE_RB, MOE_RB)
                    oh = (iota_s + (s0 + roff)) == sq
                    xs_ref[pl.ds(roff, MOE_RB), :] += jnp.dot(
                        jnp.where(oh, 1.0, 0.0).astype(BF16), xq, preferred_element_type=F32)
                    g_ref[pl.ds(roff, MOE_RB), :] += jnp.sum(jnp.where(oh, pq, 0.0), axis=1,
                                                            keepdims=True)
                    return carry

                lax.fori_loop((lo - s0) // MOE_RB, (hi - 1 - s0) // MOE_RB + 1, rows, 0)

    @pl.when((fl & FL_LAST) != 0)
    def _():
        xs = xs_ref[...].astype(BF16)
        h1 = jnp.dot(xs, w1_ref[0], preferred_element_type=F32)
        h3 = jnp.dot(xs, w3_ref[0], preferred_element_type=F32)
        hid = (_silu(h1) * h3).astype(BF16)
        y = jnp.dot(hid, w2_ref[0], preferred_element_type=F32) * g_ref[...]
        y_ref[...] = y.astype(y_ref.dtype)


def _moe_items(ts, cap):
    n_grp = ts.shape[1]
    per = MOE_TILE // MOE_SUB
    n_tile = n_grp // per
    n_sb = cap // MOE_SB
    tse = jnp.concatenate([ts, jnp.full((N_EXPERTS, 1), cap, jnp.int32)], axis=1)
    a = tse[:, 0:n_grp:per]
    b = tse[:, per::per]
    kf = a // MOE_SB
    cnt = jnp.where(b > a, (b - 1) // MOE_SB - kf + 1, 0).reshape(-1)
    incl = jnp.cumsum(cnt)
    total = incl[-1]
    n_items = N_EXPERTS * (n_tile + n_sb)
    idx = jnp.arange(n_items, dtype=jnp.int32)
    valid = idx < total
    idc = jnp.minimum(idx, total - 1)
    pair = jnp.searchsorted(incl, idc, side="right").astype(jnp.int32)
    r = idc - (incl[pair] - cnt[pair])
    e = pair // n_tile
    j = pair % n_tile
    k = kf.reshape(-1)[pair] + r
    s0 = k * MOE_SB
    aa = a.reshape(-1)[pair]
    bb = b.reshape(-1)[pair]
    first = jnp.logical_and(aa <= s0, s0 < bb)
    last = jnp.logical_and(aa <= s0 + MOE_SB - 1, s0 + MOE_SB - 1 < bb)
    fl = jnp.where(valid, FL_VALID + FL_FIRST * first + FL_LAST * last, 0).astype(jnp.int32)
    tsi = jnp.stack([tse[e, j * per + q] for q in range(per + 1)], axis=1).reshape(-1)
    return (e.astype(jnp.int32), j.astype(jnp.int32), (e * n_sb + k).astype(jnp.int32),
            s0.astype(jnp.int32), fl, tsi.astype(jnp.int32))


def _moe(hn, slot, probs_t, ts, w1, w3, w2, cap):
    n = hn.shape[0]
    items = _moe_items(ts, cap)
    n_items = items[0].shape[0]
    grid_spec = pltpu.PrefetchScalarGridSpec(
        num_scalar_prefetch=6,
        grid=(n_items,),
        in_specs=[
            pl.BlockSpec((MOE_TILE, D_MODEL), lambda i, e, t, g, s, f, ts: (t[i], 0)),
            pl.BlockSpec((N_EXPERTS, MOE_TILE), lambda i, e, t, g, s, f, ts: (0, t[i])),
            pl.BlockSpec((N_EXPERTS, MOE_TILE), lambda i, e, t, g, s, f, ts: (0, t[i])),
            pl.BlockSpec((1, D_MODEL, EXPERT_FF), lambda i, e, t, g, s, f, ts: (e[i], 0, 0)),
            pl.BlockSpec((1, D_MODEL, EXPERT_FF), lambda i, e, t, g, s, f, ts: (e[i], 0, 0)),
            pl.BlockSpec((1, EXPERT_FF, D_MODEL), lambda i, e, t, g, s, f, ts: (e[i], 0, 0)),
        ],
        out_specs=pl.BlockSpec((MOE_SB, D_MODEL), lambda i, e, t, g, s, f, ts: (g[i], 0)),
        scratch_shapes=[pltpu.VMEM((MOE_SB, D_MODEL), F32), pltpu.VMEM((MOE_SB, 1), F32)],
    )
    return pl.pallas_call(
        _moe_kernel,
        grid_spec=grid_spec,
        out_shape=jax.ShapeDtypeStruct((N_EXPERTS * cap, D_MODEL), BF16),
        compiler_params=_cparams(("arbitrary",)),
        name="moe_dispatch_mlp",
    )(*items, hn, slot, probs_t, w1, w3, w2)


def _combine_kernel(it_e, it_tile, it_yb, it_rel, it_fl, slot_ref, y_ref, h_ref, lnf_ref,
                    out_ref, acc_ref):
    i = pl.program_id(0)
    fl = it_fl[i]

    @pl.when((fl & FL_FIRST) != 0)
    def _():
        acc_ref[...] = jnp.zeros_like(acc_ref)

    @pl.when((fl & FL_VALID) != 0)
    def _():
        srow = slot_ref[pl.ds(it_e[i], 1), :]
        iota_s = lax.broadcasted_iota(jnp.int32, (CMB_YB, CMB_TILE), 0)
        oh = jnp.where((iota_s + it_rel[i]) == srow, 1.0, 0.0).astype(BF16)
        acc_ref[...] += _dot_tn(oh, y_ref[...])

    @pl.when((fl & FL_LAST) != 0)
    def _():
        hh = h_ref[...] + acc_ref[...]
        out_ref[...] = hh * lax.rsqrt(jnp.mean(hh * hh, axis=-1, keepdims=True) + EPS) * lnf_ref[...]


def _combine_items(ts, cap):
    n_grp = ts.shape[1]
    per = CMB_TILE // TOPK_LANES
    n_tile = n_grp // per
    tse = jnp.concatenate([ts, jnp.full((N_EXPERTS, 1), cap, jnp.int32)], axis=1)
    a = tse[:, 0:n_grp:per].T.reshape(-1)
    b = tse[:, per::per].T.reshape(-1)
    kf = jnp.minimum(a, cap - 1) // CMB_YB
    cnt = jnp.where(b > a, (b - 1) // CMB_YB - kf + 1, 1)
    incl = jnp.cumsum(cnt)
    total = incl[-1]
    n_items = n_tile * N_EXPERTS + N_EXPERTS * cap // CMB_YB
    idx = jnp.arange(n_items, dtype=jnp.int32)
    valid = idx < total
    idc = jnp.minimum(idx, total - 1)
    pair = jnp.searchsorted(incl, idc, side="right").astype(jnp.int32)
    r = idc - (incl[pair] - cnt[pair])
    tile = pair // N_EXPERTS
    e = pair % N_EXPERTS
    k = kf[pair] + r
    first = jnp.logical_and(e == 0, r == 0)
    last = jnp.logical_and(e == N_EXPERTS - 1, r == cnt[pair] - 1)
    fl = jnp.where(valid, FL_VALID + FL_FIRST * first + FL_LAST * last, 0).astype(jnp.int32)
    yb = e * (cap // CMB_YB) + k
    return (e.astype(jnp.int32), tile.astype(jnp.int32), yb.astype(jnp.int32),
            (k * CMB_YB).astype(jnp.int32), fl)


def _combine(y, slot, ts, h, ln_f, cap):
    n = h.shape[0]
    items = _combine_items(ts, cap)
    n_items = items[0].shape[0]
    grid_spec = pltpu.PrefetchScalarGridSpec(
        num_scalar_prefetch=5,
        grid=(n_items,),
        in_specs=[
            pl.BlockSpec((N_EXPERTS, CMB_TILE), lambda i, e, t, yb, rel, f: (0, t[i])),
            pl.BlockSpec((CMB_YB, D_MODEL), lambda i, e, t, yb, rel, f: (yb[i], 0)),
            pl.BlockSpec((CMB_TILE, D_MODEL), lambda i, e, t, yb, rel, f: (t[i], 0)),
            pl.BlockSpec((1, D_MODEL), lambda i, e, t, yb, rel, f: (0, 0)),
        ],
        out_specs=pl.BlockSpec((CMB_TILE, D_MODEL), lambda i, e, t, yb, rel, f: (t[i], 0)),
        scratch_shapes=[pltpu.VMEM((CMB_TILE, D_MODEL), F32)],
    )
    return pl.pallas_call(
        _combine_kernel,
        grid_spec=grid_spec,
        out_shape=jax.ShapeDtypeStruct((n, D_MODEL), F32),
        compiler_params=_cparams(("arbitrary",)),
        name="moe_combine",
    )(*items, slot, y, h, ln_f)


RT_TOK = TOPK_LANES
DSP_TOK = 2 * TOPK_LANES
RT_ROWS = 128
RT_SPECS = 2 * N_EXPERTS
RT_GROUP = 4
XS_WIDTH = D_MODEL + GATE_LANES
SF_VALID, SF_FIRST, SF_LAST = 1, 2, 4
BF_ACTIVE, BF_FIRST = 1, 2


def _route_steps(ts, cap, tok):
    ts = ts[:, ::tok // TOPK_LANES]
    n_sub = ts.shape[1]
    tse = jnp.concatenate([ts, jnp.full((N_EXPERTS, 1), cap, jnp.int32)], axis=1)
    a, b = tse[:, :-1], tse[:, 1:]
    kf = jnp.minimum(a, cap - 1) // RT_ROWS
    kl = jnp.where(b > a, (b - 1) // RT_ROWS, kf)
    rounds = jnp.max((kl - kf + 2) // 2, axis=0)
    incl = jnp.cumsum(rounds)
    total = incl[-1]
    n_steps = ((tok // RT_ROWS + 2) // 2) * n_sub
    idx = jnp.arange(n_steps, dtype=jnp.int32)
    valid = idx < total
    idc = jnp.minimum(idx, total - 1)
    j = jnp.sum(incl[None, :] <= idc[:, None], axis=1).astype(jnp.int32)
    r = idc - (incl[j] - rounds[j])
    rep = lambda v: jnp.repeat(jnp.take(v, j, axis=1), 2, axis=0)
    par = jnp.tile(jnp.arange(2, dtype=jnp.int32), N_EXPERTS)[:, None]
    k0 = rep(kf) + 2 * r[None, :]
    cand = k0 + (par - k0) % 2
    a_s, b_s, kl_s = rep(a), rep(b), rep(kl)
    active = valid[None, :] & (cand <= kl_s) & (b_s > a_s)
    first = active & (a_s <= cand * RT_ROWS) & (cand * RT_ROWS < b_s)
    held = lax.cummax(jnp.where(active, cand, -1), axis=1)
    held = jnp.where(held < 0, par, held)
    order = lambda v: v.reshape(-1).astype(jnp.int32)
    bflag = BF_ACTIVE * active + BF_FIRST * first
    sflag = jnp.where(valid, SF_VALID + SF_FIRST * (r == 0) + SF_LAST * (r == rounds[j] - 1), 0)
    return (j, order(held), order(bflag), sflag.astype(jnp.int32)), total.astype(jnp.int32)


def _route_groups():
    return [[(RT_GROUP // 2 * g + q // 2, q % 2) for q in range(RT_GROUP)]
            for g in range(RT_SPECS // RT_GROUP)]


def _spec_at(table, spec, i):
    return spec * (table.shape[0] // RT_SPECS) + i


def _one_hots(i, blk, flg, slot_ref, specs):
    iota = lax.broadcasted_iota(jnp.int32, (RT_ROWS, slot_ref.shape[1]), 0)
    ohs = []
    for e, par in specs:
        k = _spec_at(blk, e * 2 + par, i)
        s0 = jnp.where((flg[k] & BF_ACTIVE) != 0, blk[k] * RT_ROWS, -(1 << 30))
        ohs.append(jnp.where((iota + s0) == slot_ref[e:e + 1, :], 1.0, 0.0).astype(BF16))
    return jnp.concatenate(ohs, axis=0)


def _dispatch_kernel(tile, blk, flg, sflg, x_ref, p3_ref, slot_ref, *outs):
    i = pl.program_id(0)

    @pl.when(i == 0)
    def _():
        for out in outs:
            out[...] = jnp.zeros_like(out)

    @pl.when((sflg[i] & SF_VALID) != 0)
    def _():
        xa = jnp.concatenate([x_ref[...], p3_ref[...]], axis=1)
        for specs in _route_groups():
            res = jnp.dot(_one_hots(i, blk, flg, slot_ref, specs), xa, preferred_element_type=F32)
            for q, (e, par) in enumerate(specs):
                out = outs[e * 2 + par]
                first = (flg[_spec_at(flg, e * 2 + par, i)] & BF_FIRST) != 0
                prev = out[...]
                prev = jnp.where(first, jnp.zeros_like(prev), prev)
                out[...] = prev + res[q * RT_ROWS:(q + 1) * RT_ROWS].astype(out.dtype)


def _dispatch(hn, p3, slot, steps, n_steps, cap):
    def out_spec(e, par):
        return pl.BlockSpec((RT_ROWS, XS_WIDTH),
                            lambda i, t, b, f, s: (b[_spec_at(b, e * 2 + par, i)] // 2, 0))

    grid_spec = pltpu.PrefetchScalarGridSpec(
        num_scalar_prefetch=4,
        grid=(n_steps,),
        in_specs=[
            pl.BlockSpec((DSP_TOK, D_MODEL), lambda i, t, b, f, s: (t[i], 0)),
            pl.BlockSpec((DSP_TOK, GATE_LANES), lambda i, t, b, f, s: (t[i], 0)),
            pl.BlockSpec((N_EXPERTS, DSP_TOK), lambda i, t, b, f, s: (0, t[i])),
        ],
        out_specs=[out_spec(e, par) for e in range(N_EXPERTS) for par in range(2)],
    )
    sds = jax.ShapeDtypeStruct((cap // 2, XS_WIDTH), BF16)
    return pl.pallas_call(
        _dispatch_kernel,
        grid_spec=grid_spec,
        out_shape=[sds] * RT_SPECS,
        compiler_params=_cparams(("arbitrary",)),
        name="moe_dispatch",
    )(*steps, hn, p3, slot)


MLP_ROWS = 4 * RT_ROWS


def _expert_mlp_kernel(xe_p, xo_p, xe_s, xo_s, w1_ref, w3_ref, w2_ref, yp_ref, ys_ref, *, e, steps_p):
    def run(xe_ref, xo_ref, y_ref):
        r = RT_ROWS
        xa = jnp.concatenate([xe_ref[0:r], xo_ref[0:r], xe_ref[r:2 * r], xo_ref[r:2 * r]], axis=0)
        xs = xa[:, :D_MODEL]
        g3 = xa[:, D_MODEL:].astype(F32)
        lane = lax.broadcasted_iota(jnp.int32, g3.shape, 1)
        mine = jnp.logical_and((lane & (N_EXPERTS - 1)) == e, lane < 3 * N_EXPERTS)
        gate = jnp.sum(jnp.where(mine, g3, 0.0), axis=1, keepdims=True)
        h1 = jnp.dot(xs, w1_ref[0], preferred_element_type=F32)
        h3 = jnp.dot(xs, w3_ref[0], preferred_element_type=F32)
        hid = (_silu(h1) * h3).astype(BF16)
        y_ref[...] = (jnp.dot(hid, w2_ref[0], preferred_element_type=F32) * gate).astype(y_ref.dtype)

    m = pl.program_id(0)

    @pl.when(m < steps_p)
    def _():
        run(xe_p, xo_p, yp_ref)

    @pl.when(m >= steps_p)
    def _():
        run(xe_s, xo_s, ys_ref)


def _expert_mlp(xs_p, xs_s, w1, w3, w2, e, cap_p, cap_s):
    steps_p, steps_s = cap_p // MLP_ROWS, cap_s // MLP_ROWS
    first = lambda m: (jnp.minimum(m, steps_p - 1), 0)
    second = lambda m: (jnp.maximum(m - steps_p, 0), 0)
    half = lambda idx: pl.BlockSpec((MLP_ROWS // 2, XS_WIDTH), idx)
    return pl.pallas_call(
        functools.partial(_expert_mlp_kernel, e=e, steps_p=steps_p),
        grid=(steps_p + steps_s,),
        in_specs=[
            half(first), half(first), half(second), half(second),
            pl.BlockSpec((1, D_MODEL, EXPERT_FF), lambda m: (e, 0, 0)),
            pl.BlockSpec((1, D_MODEL, EXPERT_FF), lambda m: (e, 0, 0)),
            pl.BlockSpec((1, EXPERT_FF, D_MODEL), lambda m: (e, 0, 0)),
        ],
        out_specs=[pl.BlockSpec((MLP_ROWS, D_MODEL), first),
                   pl.BlockSpec((MLP_ROWS, D_MODEL), second)],
        out_shape=[jax.ShapeDtypeStruct((cap_p, D_MODEL), BF16),
                   jax.ShapeDtypeStruct((cap_s, D_MODEL), BF16)],
        compiler_params=_cparams(("arbitrary",)),
        name="expert_mlp",
    )(xs_p[2 * e], xs_p[2 * e + 1], xs_s[2 * e], xs_s[2 * e + 1], w1, w3, w2)


CMB_WIN = 64
CMB_ALIGN = 16


def _combine_steps(ts, cap):
    n_sub = ts.shape[1]
    tse = jnp.concatenate([ts, jnp.full((N_EXPERTS, 1), cap, jnp.int32)], axis=1)
    a, b = tse[:, :-1], tse[:, 1:]
    a16 = (a // CMB_ALIGN) * CMB_ALIGN
    rounds = jnp.maximum(jnp.max((b - a16 + CMB_WIN - 1) // CMB_WIN, axis=0), 1)
    incl = jnp.cumsum(rounds)
    total = incl[-1]
    n_steps = ((RT_TOK + CMB_ALIGN) // CMB_WIN + 1) * n_sub
    idx = jnp.arange(n_steps, dtype=jnp.int32)
    valid = idx < total
    idc = jnp.minimum(idx, total - 1)
    j = jnp.sum(incl[None, :] <= idc[:, None], axis=1).astype(jnp.int32)
    r = idc - (incl[j] - rounds[j])
    base = jnp.take(a16, j, axis=1) + CMB_WIN * r[None, :]
    wstart = jnp.clip(base, 0, cap - CMB_WIN) // CMB_ALIGN
    order = lambda v: v.reshape(-1).astype(jnp.int32)
    sflag = jnp.where(valid, SF_VALID + SF_FIRST * (r == 0) + SF_LAST * (r == rounds[j] - 1), 0)
    return (j, order(wstart), order(base), sflag.astype(jnp.int32)), total.astype(jnp.int32)


def _combine3_kernel(tile, ws, base, sflg, slot_ref, h_ref, lnf_ref, *rest):
    ys = rest[:N_EXPERTS]
    out_ref, acc_ref = rest[N_EXPERTS], rest[N_EXPERTS + 1]
    i = pl.program_id(0)
    sf = sflg[i]

    @pl.when((sf & SF_FIRST) != 0)
    def _():
        acc_ref[...] = jnp.zeros_like(acc_ref)

    @pl.when((sf & SF_VALID) != 0)
    def _():
        iota = lax.broadcasted_iota(jnp.int32, (CMB_WIN, RT_TOK), 0)
        ohs = []
        for e in range(N_EXPERTS):
            k = e * (base.shape[0] // N_EXPERTS) + i
            rel = slot_ref[e:e + 1, :] - base[k]
            rel = jnp.where(rel >= 0, rel, -(1 << 30))
            shift = base[k] - ws[k] * CMB_ALIGN
            ohs.append(jnp.where((iota - shift) == rel, 1.0, 0.0).astype(BF16))
        ycat = jnp.concatenate([y[...] for y in ys], axis=0)
        acc_ref[...] += _dot_tn(jnp.concatenate(ohs, axis=0), ycat)

    @pl.when((sf & SF_LAST) != 0)
    def _():
        hh = h_ref[...] + acc_ref[...]
        out_ref[...] = hh * lax.rsqrt(jnp.mean(hh * hh, axis=-1, keepdims=True) + EPS) * lnf_ref[...]


def _combine3(ys, slot, steps, n_steps, h, ln_f):
    n = h.shape[0]

    def y_spec(e):
        return pl.BlockSpec((pl.Element(CMB_WIN), pl.Element(D_MODEL)),
                            lambda i, t, w, b, s: (w[e * (w.shape[0] // N_EXPERTS) + i] * CMB_ALIGN, 0))

    grid_spec = pltpu.PrefetchScalarGridSpec(
        num_scalar_prefetch=4,
        grid=(n_steps,),
        in_specs=[
            pl.BlockSpec((N_EXPERTS, RT_TOK), lambda i, t, w, b, s: (0, t[i])),
            pl.BlockSpec((RT_TOK, D_MODEL), lambda i, t, w, b, s: (t[i], 0)),
            pl.BlockSpec((1, D_MODEL), lambda i, t, w, b, s: (0, 0)),
        ] + [y_spec(e) for e in range(N_EXPERTS)],
        out_specs=pl.BlockSpec((RT_TOK, D_MODEL), lambda i, t, w, b, s: (t[i], 0)),
        scratch_shapes=[pltpu.VMEM((RT_TOK, D_MODEL), F32)],
    )
    return pl.pallas_call(
        _combine3_kernel,
        grid_spec=grid_spec,
        out_shape=jax.ShapeDtypeStruct((n, D_MODEL), F32),
        compiler_params=_cparams(("arbitrary",)),
        name="moe_combine",
    )(*steps, slot, h, ln_f, *ys)


def _combine2_kernel(tile, blk, flg, sflg, slot_ref, h_ref, lnf_ref, *rest):
    ys = rest[:RT_SPECS]
    out_ref, acc_ref = rest[RT_SPECS], rest[RT_SPECS + 1]
    i = pl.program_id(0)
    sf = sflg[i]

    @pl.when((sf & SF_FIRST) != 0)
    def _():
        acc_ref[...] = jnp.zeros_like(acc_ref)

    @pl.when((sf & SF_VALID) != 0)
    def _():
        total = None
        for specs in _route_groups():
            ycat = jnp.concatenate([ys[e * 2 + par][...] for e, par in specs], axis=0)
            t = _dot_tn(_one_hots(i, blk, flg, slot_ref, specs), ycat)
            total = t if total is None else total + t
        acc_ref[...] += total

    @pl.when((sf & SF_LAST) != 0)
    def _():
        hh = h_ref[...] + acc_ref[...]
        out_ref[...] = hh * lax.rsqrt(jnp.mean(hh * hh, axis=-1, keepdims=True) + EPS) * lnf_ref[...]


def _combine2(ys, slot, steps, n_steps, h, ln_f):
    n = h.shape[0]

    def y_spec(e, par):
        return pl.BlockSpec((RT_ROWS, D_MODEL),
                            lambda i, t, b, f, s: (b[i * RT_SPECS + e * 2 + par], 0))

    grid_spec = pltpu.PrefetchScalarGridSpec(
        num_scalar_prefetch=4,
        grid=(n_steps,),
        in_specs=[
            pl.BlockSpec((N_EXPERTS, RT_TOK), lambda i, t, b, f, s: (0, t[i])),
            pl.BlockSpec((RT_TOK, D_MODEL), lambda i, t, b, f, s: (t[i], 0)),
            pl.BlockSpec((1, D_MODEL), lambda i, t, b, f, s: (0, 0)),
        ] + [y_spec(e, par) for e in range(N_EXPERTS) for par in range(2)],
        out_specs=pl.BlockSpec((RT_TOK, D_MODEL), lambda i, t, b, f, s: (t[i], 0)),
        scratch_shapes=[pltpu.VMEM((RT_TOK, D_MODEL), F32)],
    )
    return pl.pallas_call(
        _combine2_kernel,
        grid_spec=grid_spec,
        out_shape=jax.ShapeDtypeStruct((n, D_MODEL), F32),
        compiler_params=_cparams(("arbitrary",)),
        name="moe_combine",
    )(*steps, slot, h, ln_f, *[ys[e] for e in range(N_EXPERTS) for _ in range(2)])

def _prep_in_weights(w_in, a_log_f, a_log_b, dt_bias_f, dt_bias_b):
    offs = np.cumsum([0, A_QK, A_QK, A_V, A_V, A_HEADS, A_HEADS, A_HEADS, A_HEADS,
                      B_QK, B_QK, B_V, B_V, GLA_LOWRANK, GLA_LOWRANK])
    seg = [w_in[:, offs[i]:offs[i + 1]] for i in range(14)]
    w_main = jnp.concatenate(seg[0:4] + seg[8:12], axis=1).astype(BF16)
    w_small = jnp.concatenate(seg[4:8] + seg[12:14], axis=1)
    w_small = jnp.pad(w_small, ((0, 0), (0, SMALL_WIDTH - w_small.shape[1]))).astype(BF16)
    pad = SMALL_WIDTH - 4 * A_HEADS
    z = jnp.zeros((2 * A_HEADS,), F32)
    a_row = jnp.concatenate([z, a_log_f.astype(F32), a_log_b.astype(F32), jnp.zeros((pad,), F32)])
    dt_row = jnp.concatenate([z, dt_bias_f.astype(F32), dt_bias_b.astype(F32), jnp.zeros((pad,), F32)])
    small_params = jnp.zeros((8, SMALL_WIDTH), F32).at[0].set(a_row).at[1].set(dt_row)
    return w_main, w_small, small_params


def kernel(x_prompt, x_sample, ln1, w_in, conv_w, a_log_f, a_log_b, dt_bias_f, dt_bias_b, norm_a, gla_w_f, gla_b_f, gla_w_b, gla_b_b, norm_b, w_out, ln2, w_router, w1, w3, w2, ln_f):
    w_main, w_small, small_params = _prep_in_weights(w_in[0], a_log_f[0], a_log_b[0],
                                                     dt_bias_f[0], dt_bias_b[0])
    conv8 = _conv_layout(conv_w[0])
    wg, bg = _gla_gate_weights(gla_w_f[0], gla_b_f[0], gla_w_b[0], gla_b_b[0])
    wl = jnp.asarray(_gla_level_weights(), BF16)
    wo = w_out[0].astype(BF16)
    wr_t = w_router[0].astype(F32).T
    wrh = wr_t.astype(BF16)
    wrl = (wr_t - wrh.astype(F32)).astype(BF16)
    wr_pad = jnp.pad(w_router[0].astype(F32), ((0, 0), (0, GATE_LANES - N_EXPERTS)))
    wrh2 = wr_pad.astype(BF16)
    wrl2 = (wr_pad - wrh2.astype(F32)).astype(BF16)
    w1b, w3b, w2b = w1[0].astype(BF16), w3[0].astype(BF16), w2[0].astype(BF16)
    row = lambda v: v.astype(F32).reshape(1, -1)
    routed = []
    for x in (x_prompt, x_sample):
        bsz, seq, _ = x.shape
        n = bsz * seq
        cap = EC_CAPACITY * n // N_EXPERTS
        x2d = x.reshape(n, D_MODEL)
        main, small = _in_proj(x2d, row(ln1[0]), w_main, w_small, small_params, tm=1024, tn=1792)
        main3 = main.reshape(bsz, seq, MAIN_WIDTH)
        small3 = small.reshape(bsz, seq, SMALL_WIDTH)
        oaf, oab = _gdn(_qkv_conv(main3, conv8, rows=SCAN_ROWS), _scal_layout(small3))
        obf, obb = _gla(main3, small3, wg, bg, wl)
        flat = lambda o: o.reshape(n, -1)
        h, hn, probs_t, p3 = _out_proj(flat(oaf), flat(oab), flat(obf), flat(obb), main, x2d, wo,
                                       row(norm_a[0]), row(norm_b[0]), row(ln2[0]),
                                       wrh, wrl, wrh2, wrl2, tm=256)
        slot, ts = _topk(probs_t, cap)
        dsp_steps, n_dsp = _route_steps(ts, cap, DSP_TOK)
        cmb_steps, n_cmb = _combine_steps(ts, cap)
        xs = _dispatch(hn, p3, slot, dsp_steps, n_dsp, cap)
        routed.append((x.shape, cap, xs, slot, cmb_steps, n_cmb, h))
    (_, cap_p, xs_p, *_), (_, cap_s, xs_s, *_) = routed
    ys = [_expert_mlp(xs_p, xs_s, w1b, w3b, w2b, e, cap_p, cap_s) for e in range(N_EXPERTS)]
    outs = []
    for k, (shape, cap, xs, slot, cmb_steps, n_cmb, h) in enumerate(routed):
        out = _combine3([y[k] for y in ys], slot, cmb_steps, n_cmb, h, row(ln_f))
        outs.append(out.reshape(shape))
    return tuple(outs)
```

```python
import functools

import jax
import jax.numpy as jnp
import numpy as np
from jax import lax
from jax.experimental import pallas as pl
from jax.experimental.pallas import tpu as pltpu

F32 = jnp.float32
BF16 = jnp.bfloat16

D_MODEL = 2048
A_HEADS, A_DK, A_DV = 8, 128, 128
B_HEADS, B_DK, B_DV = 4, 128, 256
GLA_LOWRANK = 16
GLA_GATE_NORM = 16.0
CONV_K = 5
CHUNK = 64
N_EXPERTS = 16
EC_CAPACITY = 2
EXPERT_FF = D_MODEL // 2
EPS = 1e-6

A_QK = A_HEADS * A_DK
A_V = A_HEADS * A_DV
B_QK = B_HEADS * B_DK
B_V = B_HEADS * B_DV
MAIN_WIDTH = 2 * A_QK + 2 * A_V + 2 * B_QK + 2 * B_V
SMALL_WIDTH = 128
LR_OFF = 4 * A_HEADS
V7X_VMEM_LIMIT = 56 * 1024 * 1024
CONV_HALO = 16
SCAN_ROWS = 256
CPS = SCAN_ROWS // CHUNK
GLA_LEVELS = 6
GATE_LANES = 128
GDN_HEADS_PER_PASS = 8


def _cparams(sem):
    return pltpu.CompilerParams(dimension_semantics=sem, vmem_limit_bytes=V7X_VMEM_LIMIT)


def _dot(a, b):
    return jnp.dot(a.astype(BF16), b.astype(BF16), preferred_element_type=F32)


def _dot_nt(a, b):
    return lax.dot_general(a.astype(BF16), b.astype(BF16), (((1,), (1,)), ((), ())),
                           preferred_element_type=F32)


def _dot_tn(a, b):
    return lax.dot_general(a.astype(BF16), b.astype(BF16), (((0,), (0,)), ((), ())),
                           preferred_element_type=F32)


def _silu(x):
    return x * (1.0 / (1.0 + jnp.exp(-x)))


def _in_proj_kernel(x_ref, ln_ref, wm_ref, ws_ref, sp_ref, main_ref, small_ref, hn_ref):
    j = pl.program_id(1)

    @pl.when(j == 0)
    def _():
        xf = x_ref[...]
        y = xf * lax.rsqrt(jnp.mean(xf * xf, axis=-1, keepdims=True) + EPS) * ln_ref[...]
        hn = y.astype(BF16)
        hn_ref[...] = hn
        s = jnp.dot(hn, ws_ref[...], preferred_element_type=F32)
        lane = lax.broadcasted_iota(jnp.int32, s.shape, 1)
        neg_a = -jnp.exp(sp_ref[0:1, :])
        z = s + sp_ref[1:2, :]
        softplus = jnp.maximum(z, 0.0) + jnp.log(1.0 + jnp.exp(-jnp.abs(z)))
        sig = 1.0 / (1.0 + jnp.exp(-s))
        small_ref[...] = jnp.where(lane < 2 * A_HEADS, sig,
                                   jnp.where(lane < 4 * A_HEADS, neg_a * softplus, s))

    main_ref[...] = jnp.dot(hn_ref[...], wm_ref[...], preferred_element_type=F32).astype(BF16)


def _in_proj(x2d, ln1, w_main, w_small, small_params, tm, tn):
    t = x2d.shape[0]
    return pl.pallas_call(
        _in_proj_kernel,
        grid=(t // tm, MAIN_WIDTH // tn),
        in_specs=[
            pl.BlockSpec((tm, D_MODEL), lambda i, j: (i, 0)),
            pl.BlockSpec((1, D_MODEL), lambda i, j: (0, 0)),
            pl.BlockSpec((D_MODEL, tn), lambda i, j: (0, j)),
            pl.BlockSpec((D_MODEL, SMALL_WIDTH), lambda i, j: (0, 0)),
            pl.BlockSpec((8, SMALL_WIDTH), lambda i, j: (0, 0)),
        ],
        out_specs=[
            pl.BlockSpec((tm, tn), lambda i, j: (i, j)),
            pl.BlockSpec((tm, SMALL_WIDTH), lambda i, j: (i, 0)),
        ],
        out_shape=[
            jax.ShapeDtypeStruct((t, MAIN_WIDTH), BF16),
            jax.ShapeDtypeStruct((t, SMALL_WIDTH), F32),
        ],
        scratch_shapes=[pltpu.VMEM((tm, D_MODEL), BF16)],
        compiler_params=_cparams(("arbitrary", "arbitrary")),
        name="in_proj",
    )(x2d, ln1, w_main, w_small, small_params)


def _chunk_iotas():
    ii = lax.broadcasted_iota(jnp.int32, (CHUNK, CHUNK), 0)
    jj = lax.broadcasted_iota(jnp.int32, (CHUNK, CHUNK), 1)
    return ii, jj


def _row_to_col(row, eye):
    return jnp.sum(jnp.where(eye, row, 0.0), axis=1, keepdims=True)


def _col_to_row(col, eye):
    return jnp.sum(jnp.where(eye, col, 0.0), axis=0, keepdims=True)


def _l2norm(x):
    return x * lax.rsqrt(jnp.sum(x * x, axis=-1, keepdims=True) + EPS)


def _delta_chunks(qs, ks, vs, beta_rows, g_rows, states, revs):
    ii, jj = _chunk_iotas()
    eye = ii == jj
    nc = range(len(qs))
    incl = [(ii <= jj) if r else (ii >= jj) for r in revs]
    strict = [(ii < jj) if r else (ii > jj) for r in revs]
    gc_col = [jnp.sum(jnp.where(incl[c], g_rows[c], 0.0), axis=1, keepdims=True) for c in nc]
    gc_row = [_col_to_row(gc_col[c], eye) for c in nc]
    beta_col = [_row_to_col(beta_rows[c], eye) for c in nc]
    g_tot = [jnp.sum(g_rows[c], axis=1, keepdims=True) for c in nc]
    decay = [jnp.where(incl[c], jnp.exp(jnp.where(incl[c], gc_col[c] - gc_row[c], 0.0)), 0.0)
             for c in nc]
    kb = [ks[c] * beta_col[c] for c in nc]
    kq = [_dot_nt(jnp.concatenate([kb[c], qs[c]], axis=0), ks[c]) for c in nc]
    kk = [kq[c][:CHUNK] for c in nc]
    qk = [kq[c][CHUNK:] for c in nc]
    p = [jnp.where(strict[c], -kk[c] * decay[c], 0.0) for c in nc]
    toff = p
    p = [_dot(p[c], p[c]) for c in nc]
    for _ in range(4):
        r = [_dot(jnp.concatenate([p[c], toff[c]], axis=0), p[c]) for c in nc]
        toff = [toff[c] + p[c] + r[c][CHUNK:] for c in nc]
        p = [r[c][:CHUNK] for c in nc]
    tp = [_dot(toff[c], p[c]) for c in nc]
    toff = [toff[c] + p[c] + tp[c] for c in nc]
    e_gc = [jnp.exp(gc_col[c]) for c in nc]
    rhs = [jnp.concatenate([vs[c] * beta_col[c], kb[c] * e_gc[c]], axis=1) for c in nc]
    sol = [rhs[c] + _dot(toff[c], rhs[c]) for c in nc]
    attn = [jnp.where(incl[c], qk[c] * decay[c], 0.0) for c in nc]
    wq = [_dot(jnp.concatenate([sol[c][:, A_DV:], qs[c] * e_gc[c]], axis=0), states[c]) for c in nc]
    ws = [wq[c][:CHUNK] for c in nc]
    qs_state = [wq[c][CHUNK:] for c in nc]
    v_new = [sol[c][:, :A_DV] - ws[c] for c in nc]
    av = [_dot(attn[c], v_new[c]) for c in nc]
    kv = [_dot_tn(ks[c] * jnp.exp(g_tot[c] - gc_col[c]), v_new[c]) for c in nc]
    outs = [qs_state[c] + av[c] for c in nc]
    new_states = [states[c] * jnp.exp(g_tot[c]) + kv[c] for c in nc]
    return outs, new_states


def _qkv_conv_kernel(blk, prv, nxt, cw_ref, out_ref, pad_ref):
    n = pl.program_id(1)
    nb = pl.num_programs(1)
    rows = blk.shape[1]
    pad_ref[0:CONV_HALO, :] = jnp.where(n == 0, 0.0, prv[0].astype(F32))
    pad_ref[CONV_HALO:CONV_HALO + rows, :] = blk[0].astype(F32)
    pad_ref[CONV_HALO + rows:, :] = jnp.where(n == nb - 1, 0.0, nxt[0].astype(F32))
    lo = CONV_HALO - CONV_K // 2
    for g in range((2 * A_QK + A_V) // A_DK):
        cols = slice(g * A_DK, (g + 1) * A_DK)
        w = cw_ref[:, cols]
        win = pad_ref[:, cols]
        total = win.shape[0]
        acc = None
        for j in range(CONV_K):
            shift = (CONV_HALO - lo - j) % total
            tap = win if shift == 0 else pltpu.roll(win, shift, 0)
            term = w[j:j + 1, :] * tap[CONV_HALO:CONV_HALO + rows, :]
            acc = term if acc is None else acc + term
        y = _silu(acc)
        if g < A_HEADS:
            y = _l2norm(y) * (A_DK ** -0.5)
        elif g < 2 * A_HEADS:
            y = _l2norm(y)
        out_ref[0, :, cols] = y.astype(out_ref.dtype)


def _qkv_conv(main3, conv8, rows):
    bsz, seq, _ = main3.shape
    width = 2 * A_QK + A_V
    hb = rows // CONV_HALO
    nhalo = seq // CONV_HALO
    return pl.pallas_call(
        _qkv_conv_kernel,
        grid=(bsz, seq // rows),
        in_specs=[
            pl.BlockSpec((1, rows, width), lambda b, n: (b, n, 0)),
            pl.BlockSpec((1, CONV_HALO, width), lambda b, n: (b, jnp.maximum(n * hb - 1, 0), 0)),
            pl.BlockSpec((1, CONV_HALO, width),
                         lambda b, n: (b, jnp.minimum((n + 1) * hb, nhalo - 1), 0)),
            pl.BlockSpec((8, width), lambda b, n: (0, 0)),
        ],
        out_specs=pl.BlockSpec((1, rows, width), lambda b, n: (b, n, 0)),
        out_shape=jax.ShapeDtypeStruct((bsz, seq, width), BF16),
        scratch_shapes=[pltpu.VMEM((rows + 2 * CONV_HALO, width), F32)],
        compiler_params=_cparams(("arbitrary", "arbitrary")),
        name="qkv_conv",
    )(main3, main3, main3, conv8)


def _gdn_kernel(blk_f, blk_b, sc_f, sc_b, of_ref, ob_ref, st_ref):
    n = pl.program_id(1)

    @pl.when(n == 0)
    def _():
        st_ref[...] = jnp.zeros_like(st_ref)

    def load(d, row0, a, h):
        blk = blk_f if d == 0 else blk_b
        col = a * A_QK + h * A_DK
        return blk[0, pl.ds(row0, CHUNK), col:col + A_DK].astype(F32)

    def body(s, carry):
        for h0 in range(0, A_HEADS, GDN_HEADS_PER_PASS):
            chains = []
            for d, sc_ref in enumerate((sc_f, sc_b)):
                sc = s if d == 0 else CPS - 1 - s
                row0 = pl.multiple_of(sc * CHUNK, CHUNK)
                for h in range(h0, h0 + GDN_HEADS_PER_PASS):
                    chains.append((d, h, sc, row0, sc_ref))
            qs = [load(d, row0, 0, h) for d, h, sc, row0, _ in chains]
            ks = [load(d, row0, 1, h) for d, h, sc, row0, _ in chains]
            vs = [load(d, row0, 2, h) for d, h, sc, row0, _ in chains]
            betas = [r[0, 0, d, h, pl.ds(sc, 1), :] for d, h, sc, row0, r in chains]
            gs = [r[0, 0, 2 + d, h, pl.ds(sc, 1), :] for d, h, sc, row0, r in chains]
            states = [st_ref[d, h] for d, h, sc, row0, _ in chains]
            outs, new_states = _delta_chunks(qs, ks, vs, betas, gs, states,
                                             [d == 1 for d, *_ in chains])
            for (d, h, sc, row0, _), o, st in zip(chains, outs, new_states):
                st_ref[d, h] = st
                out_ref = of_ref if d == 0 else ob_ref
                out_ref[0, pl.ds(row0, CHUNK), h * A_DV:(h + 1) * A_DV] = o.astype(out_ref.dtype)
        return carry

    lax.fori_loop(0, CPS, body, 0)


def _scal_layout(small):
    bsz, seq = small.shape[:2]
    s = small[..., :4 * A_HEADS].reshape(bsz, seq // SCAN_ROWS, CPS, CHUNK, 4, A_HEADS)
    return jnp.transpose(s, (0, 1, 4, 5, 2, 3))


def _conv_layout(conv_w):
    return jnp.pad(conv_w.astype(F32), ((0, 8 - CONV_K), (0, 0)))


def _gdn(qkv, scal):
    bsz, seq, width = qkv.shape
    nb = seq // SCAN_ROWS

    def specs(blk_of):
        return (pl.BlockSpec((1, SCAN_ROWS, width), lambda b, n: (b, blk_of(n), 0)),
                pl.BlockSpec((1, 1, 4, A_HEADS, CPS, CHUNK), lambda b, n: (b, blk_of(n), 0, 0, 0, 0)))

    qf, sf = specs(lambda n: n)
    qb, sb = specs(lambda n: nb - 1 - n)
    out_sds = jax.ShapeDtypeStruct((bsz, seq, A_V), BF16)
    return pl.pallas_call(
        _gdn_kernel,
        grid=(bsz, nb),
        in_specs=[qf, qb, sf, sb],
        out_specs=[
            pl.BlockSpec((1, SCAN_ROWS, A_V), lambda b, n: (b, n, 0)),
            pl.BlockSpec((1, SCAN_ROWS, A_V), lambda b, n: (b, nb - 1 - n, 0)),
        ],
        out_shape=[out_sds, out_sds],
        scratch_shapes=[pltpu.VMEM((2, A_HEADS, A_DK, A_DV), F32)],
        compiler_params=_cparams(("arbitrary", "arbitrary")),
        name="gdn",
    )(qkv, qkv, scal, scal)


GLA_BATCH = 2
GLA_MM_LEVELS = (4, 5)


def _gla_level_weights():
    c = CHUNK
    w = np.zeros((2, 3 * c, c), np.float32)
    for d in range(2):
        for blk, l in enumerate(GLA_MM_LEVELS):
            s = c >> (l + 1)
            for i in range(c):
                mid = (i // (2 * s)) * 2 * s + s
                if d == 0:
                    ts = range(mid, i + 1) if i >= mid else range(i + 1, mid)
                else:
                    ts = range(i, mid) if i < mid else range(mid, i)
                for t in ts:
                    w[d, blk * c + i, t] = 1.0
        for i in range(c):
            for t in range(c):
                before = t <= i if d == 0 else t >= i
                w[d, 2 * c + i, t] = 1.0 if before else 0.0
    return w


def _gla_kernel(qk_f, v_f, sm_f, qk_b, v_b, sm_b, wg_ref, bg_ref, wl_ref, of_ref, ob_ref, st_ref):
    n = pl.program_id(1)

    @pl.when(n == 0)
    def _():
        st_ref[...] = jnp.zeros_like(st_ref)

    ii, jj = _chunk_iotas()
    eye = ii == jj
    tok = lax.broadcasted_iota(jnp.int32, (CHUNK, 1), 0)

    streams = ((qk_f, v_f, sm_f, of_ref), (qk_b, v_b, sm_b, ob_ref))
    lanes = [(bi, d) for bi in range(qk_f.shape[0]) for d in range(2)]
    dirs = range(len(lanes))
    chains = [(p, h) for p in dirs for h in range(B_HEADS)]
    masks = [ii // (CHUNK >> l) == jj // (CHUNK >> l) for l in range(GLA_LEVELS)]

    def body(s, carry):
        rows = [pl.ds(pl.multiple_of((s if d == 0 else CPS - 1 - s) * CHUNK, CHUNK), CHUNK)
                for _, d in lanes]

        def blk(p, which, cols):
            bi, d = lanes[p]
            return streams[d][which][bi, rows[p], cols]

        z = [_dot(blk(p, 2, slice(None)), wg_ref[lanes[p][1]]) + bg_ref[lanes[p][1]]
             for p in dirs]
        la = [(jnp.minimum(z[d], 0.0) - jnp.log(1.0 + jnp.exp(-jnp.abs(z[d]))))
              * (1.0 / GLA_GATE_NORM) for d in dirs]
        hi = [la[d].astype(BF16) for d in dirs]
        r1 = [la[d] - hi[d].astype(F32) for d in dirs]
        mid = [r1[d].astype(BF16) for d in dirs]
        low = [(r1[d] - mid[d].astype(F32)).astype(BF16) for d in dirs]
        ex = [jnp.dot(wl_ref[lanes[d][1]], hi[d], preferred_element_type=F32)
              + jnp.dot(wl_ref[lanes[d][1]], mid[d], preferred_element_type=F32)
              + jnp.dot(wl_ref[lanes[d][1]], low[d], preferred_element_type=F32)
              for d in dirs]
        bcum = [ex[d][2 * CHUNK:] for d in dirs]
        b_tot = [bcum[p][CHUNK - 1:CHUNK] if lanes[p][1] == 0 else bcum[p][0:1] for p in dirs]
        q_all = [blk(p, 0, slice(0, B_QK)).astype(F32) * (B_DK ** -0.5) for p in dirs]
        k_all = [blk(p, 0, slice(B_QK, 2 * B_QK)).astype(F32) for p in dirs]
        qs, ks = [], []
        for l in range(GLA_LEVELS):
            half = CHUNK >> (l + 1)
            right = (tok // half) % 2 == 1
            q_side = [right if d == 0 else jnp.logical_not(right) for _, d in lanes]
            if l in GLA_MM_LEVELS:
                wrow = GLA_MM_LEVELS.index(l) * CHUNK
                expo = [ex[d][wrow:wrow + CHUNK] for d in dirs]
            else:
                expo = []
                for d in dirs:
                    ref = jnp.concatenate(
                        [jnp.broadcast_to(bcum[d][g + half - 1 + lanes[d][1]:g + half + lanes[d][1]],
                                          (2 * half, B_QK))
                         for g in range(0, CHUNK, 2 * half)], axis=0)
                    expo.append(jnp.where(q_side[d], bcum[d] - ref, ref - bcum[d]))
            e_l = [jnp.exp(expo[d]) for d in dirs]
            qs.append([jnp.where(q_side[d], q_all[d] * e_l[d], 0.0).astype(BF16) for d in dirs])
            ks.append([jnp.where(q_side[d], 0.0, k_all[d] * e_l[d]).astype(BF16) for d in dirs])
        q_dec = [(q_all[d] * jnp.exp(bcum[d])).astype(BF16) for d in dirs]
        k_dec = [(k_all[d] * jnp.exp(b_tot[d] - bcum[d])).astype(BF16) for d in dirs]
        e_tot = [jnp.exp(b_tot[d]) for d in dirs]
        ck = [slice(h * B_DK, (h + 1) * B_DK) for h in range(B_HEADS)]
        cv = [slice(h * B_DV, (h + 1) * B_DV) for h in range(B_HEADS)]
        attn = [jnp.where(eye, _dot_nt(q_all[d][:, ck[h]], k_all[d][:, ck[h]]), 0.0)
                for d, h in chains]
        for l in range(GLA_LEVELS):
            part = [_dot_nt(qs[l][d][:, ck[h]], ks[l][d][:, ck[h]]) for d, h in chains]
            attn = [attn[c] + jnp.where(masks[l], part[c], 0.0) for c in range(len(chains))]
        v = [blk(p, 1, cv[h]) for p, h in chains]
        st = [st_ref[lanes[p][0], lanes[p][1], h] for p, h in chains]
        o_state = [_dot_nt(q_dec[d][:, ck[h]], st[c]) for c, (d, h) in enumerate(chains)]
        o_local = [_dot(attn[c], v[c]) for c in range(len(chains))]
        kv = [_dot_tn(v[c], k_dec[d][:, ck[h]]) for c, (d, h) in enumerate(chains)]
        for c, (p, h) in enumerate(chains):
            bi, d = lanes[p]
            st_ref[bi, d, h] = st[c] * e_tot[p][:, ck[h]] + kv[c]
            streams[d][3][bi, rows[p], cv[h]] = (o_state[c] + o_local[c]).astype(of_ref.dtype)
        return carry

    lax.fori_loop(0, CPS, body, 0)


def _gla(main3, small3, wg, bg, wl):
    bsz, seq, _ = main3.shape
    nb = seq // SCAN_ROWS
    qk_blk = (2 * A_QK + 2 * A_V) // (2 * B_QK)
    v_blk = (2 * A_QK + 2 * A_V + 2 * B_QK) // B_V

    gb = GLA_BATCH

    def stream(blk_of):
        return [
            pl.BlockSpec((gb, SCAN_ROWS, 2 * B_QK), lambda b, n: (b, blk_of(n), qk_blk)),
            pl.BlockSpec((gb, SCAN_ROWS, B_V), lambda b, n: (b, blk_of(n), v_blk)),
            pl.BlockSpec((gb, SCAN_ROWS, SMALL_WIDTH), lambda b, n: (b, blk_of(n), 0)),
        ]

    out_sds = jax.ShapeDtypeStruct((bsz, seq, B_V), BF16)
    return pl.pallas_call(
        _gla_kernel,
        grid=(bsz // gb, nb),
        in_specs=stream(lambda n: n) + stream(lambda n: nb - 1 - n) + [
            pl.BlockSpec((2, SMALL_WIDTH, B_QK), lambda b, n: (0, 0, 0)),
            pl.BlockSpec((2, 1, B_QK), lambda b, n: (0, 0, 0)),
            pl.BlockSpec((2, 3 * CHUNK, CHUNK), lambda b, n: (0, 0, 0)),
        ],
        out_specs=[
            pl.BlockSpec((gb, SCAN_ROWS, B_V), lambda b, n: (b, n, 0)),
            pl.BlockSpec((gb, SCAN_ROWS, B_V), lambda b, n: (b, nb - 1 - n, 0)),
        ],
        out_shape=[out_sds, out_sds],
        scratch_shapes=[pltpu.VMEM((gb, 2, B_HEADS, B_DV, B_DK), F32)],
        compiler_params=_cparams(("arbitrary", "arbitrary")),
        name="gla",
    )(main3, main3, small3, main3, main3, small3, wg, bg, wl)


def _gla_gate_weights(gla_w_f, gla_b_f, gla_w_b, gla_b_b):
    wg = jnp.zeros((2, SMALL_WIDTH, B_QK), F32)
    wg = wg.at[0, LR_OFF:LR_OFF + GLA_LOWRANK].set(gla_w_f.astype(F32))
    wg = wg.at[1, LR_OFF + GLA_LOWRANK:LR_OFF + 2 * GLA_LOWRANK].set(gla_w_b.astype(F32))
    bg = jnp.stack([gla_b_f, gla_b_b]).astype(F32).reshape(2, 1, B_QK)
    return wg.astype(BF16), bg


def _out_proj_kernel(oaf, oab, obf, obb, ga, gb, x_ref, wo_ref, na_ref, nb_ref, ln2_ref,
                     wrh_ref, wrl_ref, wrh2_ref, wrl2_ref, h_ref, hn_ref, pt_ref, p3_ref, mix_ref):
    def head_norm(rs, of_ref, ob_ref, g_ref, w_ref, width, heads, base):
        for h in range(heads):
            c = slice(h * width, (h + 1) * width)
            o = of_ref[rs, c].astype(F32) + ob_ref[rs, c].astype(F32)
            y = o * lax.rsqrt(jnp.mean(o * o, axis=-1, keepdims=True) + EPS) * w_ref[...]
            mix_ref[rs, base + h * width:base + (h + 1) * width] = (
                y * _silu(g_ref[rs, c].astype(F32))).astype(BF16)

    def mix(rs):
        head_norm(rs, oaf, oab, ga, na_ref, A_DV, A_HEADS, 0)
        head_norm(rs, obf, obb, gb, nb_ref, B_DV, B_HEADS, A_V)

    def project(rs):
        return x_ref[rs, :] + jnp.dot(mix_ref[rs, :], wo_ref[...], preferred_element_type=F32)

    def route(rs, hres):
        h_ref[rs, :] = hres
        hn = hres * lax.rsqrt(jnp.mean(hres * hres, axis=-1, keepdims=True) + EPS) * ln2_ref[...]
        hi = hn.astype(BF16)
        hn_ref[rs, :] = hi
        lo = (hn - hi.astype(F32)).astype(BF16)
        lt = _dot_nt(wrh_ref[...], hi) + _dot_nt(wrl_ref[...], hi) + _dot_nt(wrh_ref[...], lo)
        e = jnp.exp(lt - jnp.max(lt, axis=0, keepdims=True))
        pt_ref[:, rs] = e / jnp.sum(e, axis=0, keepdims=True)
        l2 = _dot(hi, wrh2_ref[...]) + _dot(hi, wrl2_ref[...]) + _dot(lo, wrh2_ref[...])
        lane = lax.broadcasted_iota(jnp.int32, l2.shape, 1)
        l2 = jnp.where(lane < N_EXPERTS, l2, -jnp.inf)
        e2 = jnp.exp(l2 - jnp.max(l2, axis=1, keepdims=True))
        p2 = e2 / jnp.sum(e2, axis=1, keepdims=True)
        g_hi = p2.astype(BF16)
        r1 = p2 - g_hi.astype(F32)
        g_mid = r1.astype(BF16)
        g_lo = (r1 - g_mid.astype(F32)).astype(BF16)
        p3 = (g_hi.astype(F32) + pltpu.roll(g_mid.astype(F32), N_EXPERTS, 1)
              + pltpu.roll(g_lo.astype(F32), 2 * N_EXPERTS, 1))
        p3_ref[rs, :] = p3.astype(BF16)

    half = x_ref.shape[0] // 2
    r0, r1_ = slice(0, half), slice(half, 2 * half)
    mix(r0)
    h0 = project(r0)
    mix(r1_)
    h1 = project(r1_)
    route(r0, h0)
    route(r1_, h1)


def _out_proj(oaf, oab, obf, obb, main, x2d, wo, na, nb, ln2, wrh, wrl, wrh2, wrl2, tm):
    t = x2d.shape[0]
    ga_blk = (2 * A_QK + A_V) // A_V
    gb_blk = (2 * A_QK + 2 * A_V + 2 * B_QK + B_V) // B_V
    row = lambda w: pl.BlockSpec((tm, w), lambda i: (i, 0))
    full = lambda a, b: pl.BlockSpec((a, b), lambda i: (0, 0))
    return pl.pallas_call(
        _out_proj_kernel,
        grid=(t // tm,),
        in_specs=[
            row(A_V), row(A_V), row(B_V), row(B_V),
            pl.BlockSpec((tm, A_V), lambda i: (i, ga_blk)),
            pl.BlockSpec((tm, B_V), lambda i: (i, gb_blk)),
            row(D_MODEL), full(D_MODEL, D_MODEL), full(1, A_DV), full(1, B_DV), full(1, D_MODEL),
            full(N_EXPERTS, D_MODEL), full(N_EXPERTS, D_MODEL),
            full(D_MODEL, GATE_LANES), full(D_MODEL, GATE_LANES),
        ],
        out_specs=[row(D_MODEL), row(D_MODEL), pl.BlockSpec((N_EXPERTS, tm), lambda i: (0, i)),
                   row(GATE_LANES)],
        out_shape=[
            jax.ShapeDtypeStruct((t, D_MODEL), F32),
            jax.ShapeDtypeStruct((t, D_MODEL), BF16),
            jax.ShapeDtypeStruct((N_EXPERTS, t), F32),
            jax.ShapeDtypeStruct((t, GATE_LANES), BF16),
        ],
        scratch_shapes=[pltpu.VMEM((tm, D_MODEL), BF16)],
        compiler_params=_cparams(("arbitrary",)),
        name="out_proj",
    )(oaf, oab, obf, obb, main, main, x2d, wo, na, nb, ln2, wrh, wrl, wrh2, wrl2)


TOPK_LANES = 256


def _topk_kernel(p_ref, slot_ref, ts_ref, *, cap):
    n = p_ref.shape[1]
    n_steps = n // TOPK_LANES
    bits = pltpu.bitcast(p_ref[...], jnp.int32)

    def bisect(i, thr):
        cand = thr | jnp.left_shift(jnp.int32(1), 30 - i)
        cnt = jnp.sum(jnp.where(bits >= cand, 1.0, 0.0), axis=1, keepdims=True)
        return jnp.where(cnt >= cap, cand, thr)

    thr = lax.fori_loop(0, 31, bisect, jnp.zeros((N_EXPERTS, 1), jnp.int32))
    need = cap - jnp.sum(jnp.where(bits > thr, 1.0, 0.0), axis=1, keepdims=True)
    r = lax.broadcasted_iota(jnp.int32, (TOPK_LANES, TOPK_LANES), 0)
    c = lax.broadcasted_iota(jnp.int32, (TOPK_LANES, TOPK_LANES), 1)
    tri = jnp.where(r <= c, 1.0, 0.0).astype(BF16)
    step_lane = lax.broadcasted_iota(jnp.int32, (N_EXPERTS, n_steps), 1)

    ts_ref[...] = jnp.zeros_like(ts_ref)

    def step(j, carry):
        ties_before, sel_before = carry
        lanes = pl.ds(pl.multiple_of(j * TOPK_LANES, TOPK_LANES), TOPK_LANES)
        pb = pltpu.bitcast(p_ref[:, lanes], jnp.int32)
        tie = jnp.where(pb == thr, 1.0, 0.0)
        tie_incl = jnp.dot(tie.astype(BF16), tri, preferred_element_type=F32)
        sel = jnp.logical_or(pb > thr, jnp.logical_and(pb == thr, ties_before + tie_incl - tie < need))
        self_f = jnp.where(sel, 1.0, 0.0)
        sel_incl = jnp.dot(self_f.astype(BF16), tri, preferred_element_type=F32)
        pos = sel_before + sel_incl - self_f
        slot_ref[:, lanes] = jnp.where(sel, pos, -1.0).astype(jnp.int32)
        ts_ref[...] = jnp.where(step_lane == j, sel_before.astype(jnp.int32), ts_ref[...])
        return (ties_before + tie_incl[:, TOPK_LANES - 1:],
                sel_before + sel_incl[:, TOPK_LANES - 1:])

    zero = jnp.zeros((N_EXPERTS, 1), F32)
    lax.fori_loop(0, n_steps, step, (zero, zero))


def _topk(probs_t, cap):
    n = probs_t.shape[1]
    return pl.pallas_call(
        functools.partial(_topk_kernel, cap=cap),
        out_shape=[
            jax.ShapeDtypeStruct((N_EXPERTS, n), jnp.int32),
            jax.ShapeDtypeStruct((N_EXPERTS, n // TOPK_LANES), jnp.int32),
        ],
        compiler_params=pltpu.CompilerParams(vmem_limit_bytes=V7X_VMEM_LIMIT),
        name="topk",
    )(probs_t)


MOE_TILE = 1024
MOE_SUB = TOPK_LANES
MOE_SB = 512
MOE_RB = 128
CMB_TILE = 512
CMB_YB = 256
FL_VALID, FL_FIRST, FL_LAST = 1, 2, 4


def _moe_kernel(it_e, it_tile, it_g, it_s0, it_fl, it_ts, x_ref, slot_ref, p_ref,
                w1_ref, w3_ref, w2_ref, y_ref, xs_ref, g_ref):
    i = pl.program_id(0)
    fl = it_fl[i]
    e = it_e[i]
    s0 = it_s0[i]

    @pl.when((fl & FL_FIRST) != 0)
    def _():
        xs_ref[...] = jnp.zeros_like(xs_ref)
        g_ref[...] = jnp.zeros_like(g_ref)

    @pl.when((fl & FL_VALID) != 0)
    def _():
        srow = slot_ref[pl.ds(e, 1), :]
        prow = p_ref[pl.ds(e, 1), :]
        iota_s = lax.broadcasted_iota(jnp.int32, (MOE_RB, MOE_SUB), 0)
        for q in range(MOE_TILE // MOE_SUB):
            lo = jnp.maximum(it_ts[i * 5 + q], s0)
            hi = jnp.minimum(it_ts[i * 5 + q + 1], s0 + MOE_SB)

            @pl.when(hi > lo)
            def _():
                sq = srow[:, q * MOE_SUB:(q + 1) * MOE_SUB]
                pq = prow[:, q * MOE_SUB:(q + 1) * MOE_SUB]
                xq = x_ref[q * MOE_SUB:(q + 1) * MOE_SUB, :]

                def rows(r, carry):
                    roff = pl.multiple_of(r * MOE_RB, MOE_RB)
                    oh = (iota_s + (s0 + roff)) == sq
                    xs_ref[pl.ds(roff, MOE_RB), :] += jnp.dot(
                        jnp.where(oh, 1.0, 0.0).astype(BF16), xq, preferred_element_type=F32)
                    g_ref[pl.ds(roff, MOE_RB), :] += jnp.sum(jnp.where(oh, pq, 0.0), axis=1,
                                                            keepdims=True)
                    return carry

                lax.fori_loop((lo - s0) // MOE_RB, (hi - 1 - s0) // MOE_RB + 1, rows, 0)

    @pl.when((fl & FL_LAST) != 0)
    def _():
        xs = xs_ref[...].astype(BF16)
        h1 = jnp.dot(xs, w1_ref[0], preferred_element_type=F32)
        h3 = jnp.dot(xs, w3_ref[0], preferred_element_type=F32)
        hid = (_silu(h1) * h3).astype(BF16)
        y = jnp.dot(hid, w2_ref[0], preferred_element_type=F32) * g_ref[...]
        y_ref[...] = y.astype(y_ref.dtype)


def _moe_items(ts, cap):
    n_grp = ts.shape[1]
    per = MOE_TILE // MOE_SUB
    n_tile = n_grp // per
    n_sb = cap // MOE_SB
    tse = jnp.concatenate([ts, jnp.full((N_EXPERTS, 1), cap, jnp.int32)], axis=1)
    a = tse[:, 0:n_grp:per]
    b = tse[:, per::per]
    kf = a // MOE_SB
    cnt = jnp.where(b > a, (b - 1) // MOE_SB - kf + 1, 0).reshape(-1)
    incl = jnp.cumsum(cnt)
    total = incl[-1]
    n_items = N_EXPERTS * (n_tile + n_sb)
    idx = jnp.arange(n_items, dtype=jnp.int32)
    valid = idx < total
    idc = jnp.minimum(idx, total - 1)
    pair = jnp.searchsorted(incl, idc, side="right").astype(jnp.int32)
    r = idc - (incl[pair] - cnt[pair])
    e = pair // n_tile
    j = pair % n_tile
    k = kf.reshape(-1)[pair] + r
    s0 = k * MOE_SB
    aa = a.reshape(-1)[pair]
    bb = b.reshape(-1)[pair]
    first = jnp.logical_and(aa <= s0, s0 < bb)
    last = jnp.logical_and(aa <= s0 + MOE_SB - 1, s0 + MOE_SB - 1 < bb)
    fl = jnp.where(valid, FL_VALID + FL_FIRST * first + FL_LAST * last, 0).astype(jnp.int32)
    tsi = jnp.stack([tse[e, j * per + q] for q in range(per + 1)], axis=1).reshape(-1)
    return (e.astype(jnp.int32), j.astype(jnp.int32), (e * n_sb + k).astype(jnp.int32),
            s0.astype(jnp.int32), fl, tsi.astype(jnp.int32))


def _moe(hn, slot, probs_t, ts, w1, w3, w2, cap):
    n = hn.shape[0]
    items = _moe_items(ts, cap)
    n_items = items[0].shape[0]
    grid_spec = pltpu.PrefetchScalarGridSpec(
        num_scalar_prefetch=6,
        grid=(n_items,),
        in_specs=[
            pl.BlockSpec((MOE_TILE, D_MODEL), lambda i, e, t, g, s, f, ts: (t[i], 0)),
            pl.BlockSpec((N_EXPERTS, MOE_TILE), lambda i, e, t, g, s, f, ts: (0, t[i])),
            pl.BlockSpec((N_EXPERTS, MOE_TILE), lambda i, e, t, g, s, f, ts: (0, t[i])),
            pl.BlockSpec((1, D_MODEL, EXPERT_FF), lambda i, e, t, g, s, f, ts: (e[i], 0, 0)),
            pl.BlockSpec((1, D_MODEL, EXPERT_FF), lambda i, e, t, g, s, f, ts: (e[i], 0, 0)),
            pl.BlockSpec((1, EXPERT_FF, D_MODEL), lambda i, e, t, g, s, f, ts: (e[i], 0, 0)),
        ],
        out_specs=pl.BlockSpec((MOE_SB, D_MODEL), lambda i, e, t, g, s, f, ts: (g[i], 0)),
        scratch_shapes=[pltpu.VMEM((MOE_SB, D_MODEL), F32), pltpu.VMEM((MOE_SB, 1), F32)],
    )
    return pl.pallas_call(
        _moe_kernel,
        grid_spec=grid_spec,
        out_shape=jax.ShapeDtypeStruct((N_EXPERTS * cap, D_MODEL), BF16),
        compiler_params=_cparams(("arbitrary",)),
        name="moe_dispatch_mlp",
    )(*items, hn, slot, probs_t, w1, w3, w2)


def _combine_kernel(it_e, it_tile, it_yb, it_rel, it_fl, slot_ref, y_ref, h_ref, lnf_ref,
                    out_ref, acc_ref):
    i = pl.program_id(0)
    fl = it_fl[i]

    @pl.when((fl & FL_FIRST) != 0)
    def _():
        acc_ref[...] = jnp.zeros_like(acc_ref)

    @pl.when((fl & FL_VALID) != 0)
    def _():
        srow = slot_ref[pl.ds(it_e[i], 1), :]
        iota_s = lax.broadcasted_iota(jnp.int32, (CMB_YB, CMB_TILE), 0)
        oh = jnp.where((iota_s + it_rel[i]) == srow, 1.0, 0.0).astype(BF16)
        acc_ref[...] += _dot_tn(oh, y_ref[...])

    @pl.when((fl & FL_LAST) != 0)
    def _():
        hh = h_ref[...] + acc_ref[...]
        out_ref[...] = hh * lax.rsqrt(jnp.mean(hh * hh, axis=-1, keepdims=True) + EPS) * lnf_ref[...]


def _combine_items(ts, cap):
    n_grp = ts.shape[1]
    per = CMB_TILE // TOPK_LANES
    n_tile = n_grp // per
    tse = jnp.concatenate([ts, jnp.full((N_EXPERTS, 1), cap, jnp.int32)], axis=1)
    a = tse[:, 0:n_grp:per].T.reshape(-1)
    b = tse[:, per::per].T.reshape(-1)
    kf = jnp.minimum(a, cap - 1) // CMB_YB
    cnt = jnp.where(b > a, (b - 1) // CMB_YB - kf + 1, 1)
    incl = jnp.cumsum(cnt)
    total = incl[-1]
    n_items = n_tile * N_EXPERTS + N_EXPERTS * cap // CMB_YB
    idx = jnp.arange(n_items, dtype=jnp.int32)
    valid = idx < total
    idc = jnp.minimum(idx, total - 1)
    pair = jnp.searchsorted(incl, idc, side="right").astype(jnp.int32)
    r = idc - (incl[pair] - cnt[pair])
    tile = pair // N_EXPERTS
    e = pair % N_EXPERTS
    k = kf[pair] + r
    first = jnp.logical_and(e == 0, r == 0)
    last = jnp.logical_and(e == N_EXPERTS - 1, r == cnt[pair] - 1)
    fl = jnp.where(valid, FL_VALID + FL_FIRST * first + FL_LAST * last, 0).astype(jnp.int32)
    yb = e * (cap // CMB_YB) + k
    return (e.astype(jnp.int32), tile.astype(jnp.int32), yb.astype(jnp.int32),
            (k * CMB_YB).astype(jnp.int32), fl)


def _combine(y, slot, ts, h, ln_f, cap):
    n = h.shape[0]
    items = _combine_items(ts, cap)
    n_items = items[0].shape[0]
    grid_spec = pltpu.PrefetchScalarGridSpec(
        num_scalar_prefetch=5,
        grid=(n_items,),
        in_specs=[
            pl.BlockSpec((N_EXPERTS, CMB_TILE), lambda i, e, t, yb, rel, f: (0, t[i])),
            pl.BlockSpec((CMB_YB, D_MODEL), lambda i, e, t, yb, rel, f: (yb[i], 0)),
            pl.BlockSpec((CMB_TILE, D_MODEL), lambda i, e, t, yb, rel, f: (t[i], 0)),
            pl.BlockSpec((1, D_MODEL), lambda i, e, t, yb, rel, f: (0, 0)),
        ],
        out_specs=pl.BlockSpec((CMB_TILE, D_MODEL), lambda i, e, t, yb, rel, f: (t[i], 0)),
        scratch_shapes=[pltpu.VMEM((CMB_TILE, D_MODEL), F32)],
    )
    return pl.pallas_call(
        _combine_kernel,
        grid_spec=grid_spec,
        out_shape=jax.ShapeDtypeStruct((n, D_MODEL), F32),
        compiler_params=_cparams(("arbitrary",)),
        name="moe_combine",
    )(*items, slot, y, h, ln_f)


RT_TOK = TOPK_LANES
DSP_TOK = 2 * TOPK_LANES
RT_ROWS = 128
RT_SPECS = 2 * N_EXPERTS
RT_GROUP = 4
XS_WIDTH = D_MODEL + GATE_LANES
SF_VALID, SF_FIRST, SF_LAST = 1, 2, 4
BF_ACTIVE, BF_FIRST = 1, 2


def _route_steps(ts, cap, tok):
    ts = ts[:, ::tok // TOPK_LANES]
    n_sub = ts.shape[1]
    tse = jnp.concatenate([ts, jnp.full((N_EXPERTS, 1), cap, jnp.int32)], axis=1)
    a, b = tse[:, :-1], tse[:, 1:]
    kf = jnp.minimum(a, cap - 1) // RT_ROWS
    kl = jnp.where(b > a, (b - 1) // RT_ROWS, kf)
    rounds = jnp.max((kl - kf + 2) // 2, axis=0)
    incl = jnp.cumsum(rounds)
    total = incl[-1]
    n_steps = ((tok // RT_ROWS + 2) // 2) * n_sub
    idx = jnp.arange(n_steps, dtype=jnp.int32)
    valid = idx < total
    idc = jnp.minimum(idx, total - 1)
    j = jnp.sum(incl[None, :] <= idc[:, None], axis=1).astype(jnp.int32)
    r = idc - (incl[j] - rounds[j])
    rep = lambda v: jnp.repeat(jnp.take(v, j, axis=1), 2, axis=0)
    par = jnp.tile(jnp.arange(2, dtype=jnp.int32), N_EXPERTS)[:, None]
    k0 = rep(kf) + 2 * r[None, :]
    cand = k0 + (par - k0) % 2
    a_s, b_s, kl_s = rep(a), rep(b), rep(kl)
    active = valid[None, :] & (cand <= kl_s) & (b_s > a_s)
    first = active & (a_s <= cand * RT_ROWS) & (cand * RT_ROWS < b_s)
    held = lax.cummax(jnp.where(active, cand, -1), axis=1)
    held = jnp.where(held < 0, par, held)
    order = lambda v: v.reshape(-1).astype(jnp.int32)
    bflag = BF_ACTIVE * active + BF_FIRST * first
    sflag = jnp.where(valid, SF_VALID + SF_FIRST * (r == 0) + SF_LAST * (r == rounds[j] - 1), 0)
    return (j, order(held), order(bflag), sflag.astype(jnp.int32)), total.astype(jnp.int32)


def _route_groups():
    return [[(RT_GROUP // 2 * g + q // 2, q % 2) for q in range(RT_GROUP)]
            for g in range(RT_SPECS // RT_GROUP)]


def _spec_at(table, spec, i):
    return spec * (table.shape[0] // RT_SPECS) + i


def _one_hots(i, blk, flg, slot_ref, specs):
    iota = lax.broadcasted_iota(jnp.int32, (RT_ROWS, slot_ref.shape[1]), 0)
    ohs = []
    for e, par in specs:
        k = _spec_at(blk, e * 2 + par, i)
        s0 = jnp.where((flg[k] & BF_ACTIVE) != 0, blk[k] * RT_ROWS, -(1 << 30))
        ohs.append(jnp.where((iota + s0) == slot_ref[e:e + 1, :], 1.0, 0.0).astype(BF16))
    return jnp.concatenate(ohs, axis=0)


def _dispatch_kernel(tile, blk, flg, sflg, x_ref, p3_ref, slot_ref, *outs):
    i = pl.program_id(0)

    @pl.when(i == 0)
    def _():
        for out in outs:
            out[...] = jnp.zeros_like(out)

    @pl.when((sflg[i] & SF_VALID) != 0)
    def _():
        xa = jnp.concatenate([x_ref[...], p3_ref[...]], axis=1)
        for specs in _route_groups():
            res = jnp.dot(_one_hots(i, blk, flg, slot_ref, specs), xa, preferred_element_type=F32)
            for q, (e, par) in enumerate(specs):
                out = outs[e * 2 + par]
                first = (flg[_spec_at(flg, e * 2 + par, i)] & BF_FIRST) != 0
                prev = out[...]
                prev = jnp.where(first, jnp.zeros_like(prev), prev)
                out[...] = prev + res[q * RT_ROWS:(q + 1) * RT_ROWS].astype(out.dtype)


def _dispatch(hn, p3, slot, steps, n_steps, cap):
    def out_spec(e, par):
        return pl.BlockSpec((RT_ROWS, XS_WIDTH),
                            lambda i, t, b, f, s: (b[_spec_at(b, e * 2 + par, i)] // 2, 0))

    grid_spec = pltpu.PrefetchScalarGridSpec(
        num_scalar_prefetch=4,
        grid=(n_steps,),
        in_specs=[
            pl.BlockSpec((DSP_TOK, D_MODEL), lambda i, t, b, f, s: (t[i], 0)),
            pl.BlockSpec((DSP_TOK, GATE_LANES), lambda i, t, b, f, s: (t[i], 0)),
            pl.BlockSpec((N_EXPERTS, DSP_TOK), lambda i, t, b, f, s: (0, t[i])),
        ],
        out_specs=[out_spec(e, par) for e in range(N_EXPERTS) for par in range(2)],
    )
    sds = jax.ShapeDtypeStruct((cap // 2, XS_WIDTH), BF16)
    return pl.pallas_call(
        _dispatch_kernel,
        grid_spec=grid_spec,
        out_shape=[sds] * RT_SPECS,
        compiler_params=_cparams(("arbitrary",)),
        name="moe_dispatch",
    )(*steps, hn, p3, slot)


MLP_ROWS = 4 * RT_ROWS


def _expert_mlp_kernel(xe_p, xo_p, xe_s, xo_s, w1_ref, w3_ref, w2_ref, yp_ref, ys_ref, *, e, steps_p):
    def run(xe_ref, xo_ref, y_ref):
        r = RT_ROWS
        xa = jnp.concatenate([xe_ref[0:r], xo_ref[0:r], xe_ref[r:2 * r], xo_ref[r:2 * r]], axis=0)
        xs = xa[:, :D_MODEL]
        g3 = xa[:, D_MODEL:].astype(F32)
        lane = lax.broadcasted_iota(jnp.int32, g3.shape, 1)
        mine = jnp.logical_and((lane & (N_EXPERTS - 1)) == e, lane < 3 * N_EXPERTS)
        gate = jnp.sum(jnp.where(mine, g3, 0.0), axis=1, keepdims=True)
        h1 = jnp.dot(xs, w1_ref[0], preferred_element_type=F32)
        h3 = jnp.dot(xs, w3_ref[0], preferred_element_type=F32)
        hid = (_silu(h1) * h3).astype(BF16)
        y_ref[...] = (jnp.dot(hid, w2_ref[0], preferred_element_type=F32) * gate).astype(y_ref.dtype)

    m = pl.program_id(0)

    @pl.when(m < steps_p)
    def _():
        run(xe_p, xo_p, yp_ref)

    @pl.when(m >= steps_p)
    def _():
        run(xe_s, xo_s, ys_ref)


def _expert_mlp(xs_p, xs_s, w1, w3, w2, e, cap_p, cap_s):
    steps_p, steps_s = cap_p // MLP_ROWS, cap_s // MLP_ROWS
    first = lambda m: (jnp.minimum(m, steps_p - 1), 0)
    second = lambda m: (jnp.maximum(m - steps_p, 0), 0)
    half = lambda idx: pl.BlockSpec((MLP_ROWS // 2, XS_WIDTH), idx)
    return pl.pallas_call(
        functools.partial(_expert_mlp_kernel, e=e, steps_p=steps_p),
        grid=(steps_p + steps_s,),
        in_specs=[
            half(first), half(first), half(second), half(second),
            pl.BlockSpec((1, D_MODEL, EXPERT_FF), lambda m: (e, 0, 0)),
            pl.BlockSpec((1, D_MODEL, EXPERT_FF), lambda m: (e, 0, 0)),
            pl.BlockSpec((1, EXPERT_FF, D_MODEL), lambda m: (e, 0, 0)),
        ],
        out_specs=[pl.BlockSpec((MLP_ROWS, D_MODEL), first),
                   pl.BlockSpec((MLP_ROWS, D_MODEL), second)],
        out_shape=[jax.ShapeDtypeStruct((cap_p, D_MODEL), BF16),
                   jax.ShapeDtypeStruct((cap_s, D_MODEL), BF16)],
        compiler_params=_cparams(("arbitrary",)),
        name="expert_mlp",
    )(xs_p[2 * e], xs_p[2 * e + 1], xs_s[2 * e], xs_s[2 * e + 1], w1, w3, w2)


CMB_WIN = 64
CMB_ALIGN = 16


def _combine_steps(ts, cap):
    n_sub = ts.shape[1]
    tse = jnp.concatenate([ts, jnp.full((N_EXPERTS, 1), cap, jnp.int32)], axis=1)
    a, b = tse[:, :-1], tse[:, 1:]
    a16 = (a // CMB_ALIGN) * CMB_ALIGN
    rounds = jnp.maximum(jnp.max((b - a16 + CMB_WIN - 1) // CMB_WIN, axis=0), 1)
    incl = jnp.cumsum(rounds)
    total = incl[-1]
    n_steps = ((RT_TOK + CMB_ALIGN) // CMB_WIN + 1) * n_sub
    idx = jnp.arange(n_steps, dtype=jnp.int32)
    valid = idx < total
    idc = jnp.minimum(idx, total - 1)
    j = jnp.sum(incl[None, :] <= idc[:, None], axis=1).astype(jnp.int32)
    r = idc - (incl[j] - rounds[j])
    base = jnp.take(a16, j, axis=1) + CMB_WIN * r[None, :]
    wstart = jnp.clip(base, 0, cap - CMB_WIN) // CMB_ALIGN
    order = lambda v: v.reshape(-1).astype(jnp.int32)
    sflag = jnp.where(valid, SF_VALID + SF_FIRST * (r == 0) + SF_LAST * (r == rounds[j] - 1), 0)
    return (j, order(wstart), order(base), sflag.astype(jnp.int32)), total.astype(jnp.int32)


def _combine3_kernel(tile, ws, base, sflg, slot_ref, h_ref, lnf_ref, *rest):
    ys = rest[:N_EXPERTS]
    out_ref, acc_ref = rest[N_EXPERTS], rest[N_EXPERTS + 1]
    i = pl.program_id(0)
    sf = sflg[i]

    @pl.when((sf & SF_FIRST) != 0)
    def _():
        acc_ref[...] = jnp.zeros_like(acc_ref)

    @pl.when((sf & SF_VALID) != 0)
    def _():
        iota = lax.broadcasted_iota(jnp.int32, (CMB_WIN, RT_TOK), 0)
        ohs = []
        for e in range(N_EXPERTS):
            k = e * (base.shape[0] // N_EXPERTS) + i
            rel = slot_ref[e:e + 1, :] - base[k]
            rel = jnp.where(rel >= 0, rel, -(1 << 30))
            shift = base[k] - ws[k] * CMB_ALIGN
            ohs.append(jnp.where((iota - shift) == rel, 1.0, 0.0).astype(BF16))
        ycat = jnp.concatenate([y[...] for y in ys], axis=0)
        acc_ref[...] += _dot_tn(jnp.concatenate(ohs, axis=0), ycat)

    @pl.when((sf & SF_LAST) != 0)
    def _():
        hh = h_ref[...] + acc_ref[...]
        out_ref[...] = hh * lax.rsqrt(jnp.mean(hh * hh, axis=-1, keepdims=True) + EPS) * lnf_ref[...]


def _combine3(ys, slot, steps, n_steps, h, ln_f):
    n = h.shape[0]

    def y_spec(e):
        return pl.BlockSpec((pl.Element(CMB_WIN), pl.Element(D_MODEL)),
                            lambda i, t, w, b, s: (w[e * (w.shape[0] // N_EXPERTS) + i] * CMB_ALIGN, 0))

    grid_spec = pltpu.PrefetchScalarGridSpec(
        num_scalar_prefetch=4,
        grid=(n_steps,),
        in_specs=[
            pl.BlockSpec((N_EXPERTS, RT_TOK), lambda i, t, w, b, s: (0, t[i])),
            pl.BlockSpec((RT_TOK, D_MODEL), lambda i, t, w, b, s: (t[i], 0)),
            pl.BlockSpec((1, D_MODEL), lambda i, t, w, b, s: (0, 0)),
        ] + [y_spec(e) for e in range(N_EXPERTS)],
        out_specs=pl.BlockSpec((RT_TOK, D_MODEL), lambda i, t, w, b, s: (t[i], 0)),
        scratch_shapes=[pltpu.VMEM((RT_TOK, D_MODEL), F32)],
    )
    return pl.pallas_call(
        _combine3_kernel,
        grid_spec=grid_spec,
        out_shape=jax.ShapeDtypeStruct((n, D_MODEL), F32),
        compiler_params=_cparams(("arbitrary",)),
        name="moe_combine",
    )(*steps, slot, h, ln_f, *ys)


def _combine2_kernel(tile, blk, flg, sflg, slot_ref, h_ref, lnf_ref, *rest):
    ys = rest[:RT_SPECS]
    out_ref, acc_ref = rest[RT_SPECS], rest[RT_SPECS + 1]
    i = pl.program_id(0)
    sf = sflg[i]

    @pl.when((sf & SF_FIRST) != 0)
    def _():
        acc_ref[...] = jnp.zeros_like(acc_ref)

    @pl.when((sf & SF_VALID) != 0)
    def _():
        total = None
        for specs in _route_groups():
            ycat = jnp.concatenate([ys[e * 2 + par][...] for e, par in specs], axis=0)
            t = _dot_tn(_one_hots(i, blk, flg, slot_ref, specs), ycat)
            total = t if total is None else total + t
        acc_ref[...] += total

    @pl.when((sf & SF_LAST) != 0)
    def _():
        hh = h_ref[...] + acc_ref[...]
        out_ref[...] = hh * lax.rsqrt(jnp.mean(hh * hh, axis=-1, keepdims=True) + EPS) * lnf_ref[...]


def _combine2(ys, slot, steps, n_steps, h, ln_f):
    n = h.shape[0]

    def y_spec(e, par):
        return pl.BlockSpec((RT_ROWS, D_MODEL),
                            lambda i, t, b, f, s: (b[i * RT_SPECS + e * 2 + par], 0))

    grid_spec = pltpu.PrefetchScalarGridSpec(
        num_scalar_prefetch=4,
        grid=(n_steps,),
        in_specs=[
            pl.BlockSpec((N_EXPERTS, RT_TOK), lambda i, t, b, f, s: (0, t[i])),
            pl.BlockSpec((RT_TOK, D_MODEL), lambda i, t, b, f, s: (t[i], 0)),
            pl.BlockSpec((1, D_MODEL), lambda i, t, b, f, s: (0, 0)),
        ] + [y_spec(e, par) for e in range(N_EXPERTS) for par in range(2)],
        out_specs=pl.BlockSpec((RT_TOK, D_MODEL), lambda i, t, b, f, s: (t[i], 0)),
        scratch_shapes=[pltpu.VMEM((RT_TOK, D_MODEL), F32)],
    )
    return pl.pallas_call(
        _combine2_kernel,
        grid_spec=grid_spec,
        out_shape=jax.ShapeDtypeStruct((n, D_MODEL), F32),
        compiler_params=_cparams(("arbitrary",)),
        name="moe_combine",
    )(*steps, slot, h, ln_f, *[ys[e] for e in range(N_EXPERTS) for _ in range(2)])

def _prep_in_weights(w_in, a_log_f, a_log_b, dt_bias_f, dt_bias_b):
    offs = np.cumsum([0, A_QK, A_QK, A_V, A_V, A_HEADS, A_HEADS, A_HEADS, A_HEADS,
                      B_QK, B_QK, B_V, B_V, GLA_LOWRANK, GLA_LOWRANK])
    seg = [w_in[:, offs[i]:offs[i + 1]] for i in range(14)]
    w_main = jnp.concatenate(seg[0:4] + seg[8:12], axis=1).astype(BF16)
    w_small = jnp.concatenate(seg[4:8] + seg[12:14], axis=1)
    w_small = jnp.pad(w_small, ((0, 0), (0, SMALL_WIDTH - w_small.shape[1]))).astype(BF16)
    pad = SMALL_WIDTH - 4 * A_HEADS
    z = jnp.zeros((2 * A_HEADS,), F32)
    a_row = jnp.concatenate([z, a_log_f.astype(F32), a_log_b.astype(F32), jnp.zeros((pad,), F32)])
    dt_row = jnp.concatenate([z, dt_bias_f.astype(F32), dt_bias_b.astype(F32), jnp.zeros((pad,), F32)])
    small_params = jnp.zeros((8, SMALL_WIDTH), F32).at[0].set(a_row).at[1].set(dt_row)
    return w_main, w_small, small_params


def kernel(x_prompt, x_sample, ln1, w_in, conv_w, a_log_f, a_log_b, dt_bias_f, dt_bias_b, norm_a, gla_w_f, gla_b_f, gla_w_b, gla_b_b, norm_b, w_out, ln2, w_router, w1, w3, w2, ln_f):
    w_main, w_small, small_params = _prep_in_weights(w_in[0], a_log_f[0], a_log_b[0],
                                                     dt_bias_f[0], dt_bias_b[0])
    conv8 = _conv_layout(conv_w[0])
    wg, bg = _gla_gate_weights(gla_w_f[0], gla_b_f[0], gla_w_b[0], gla_b_b[0])
    wl = jnp.asarray(_gla_level_weights(), BF16)
    wo = w_out[0].astype(BF16)
    wr_t = w_router[0].astype(F32).T
    wrh = wr_t.astype(BF16)
    wrl = (wr_t - wrh.astype(F32)).astype(BF16)
    wr_pad = jnp.pad(w_router[0].astype(F32), ((0, 0), (0, GATE_LANES - N_EXPERTS)))
    wrh2 = wr_pad.astype(BF16)
    wrl2 = (wr_pad - wrh2.astype(F32)).astype(BF16)
    w1b, w3b, w2b = w1[0].astype(BF16), w3[0].astype(BF16), w2[0].astype(BF16)
    row = lambda v: v.astype(F32).reshape(1, -1)
    routed = []
    for x in (x_prompt, x_sample):
        bsz, seq, _ = x.shape
        n = bsz * seq
        cap = EC_CAPACITY * n // N_EXPERTS
        x2d = x.reshape(n, D_MODEL)
        main, small = _in_proj(x2d, row(ln1[0]), w_main, w_small, small_params, tm=1024, tn=1792)
        main3 = main.reshape(bsz, seq, MAIN_WIDTH)
        small3 = small.reshape(bsz, seq, SMALL_WIDTH)
        oaf, oab = _gdn(_qkv_conv(main3, conv8, rows=SCAN_ROWS), _scal_layout(small3))
        obf, obb = _gla(main3, small3, wg, bg, wl)
        flat = lambda o: o.reshape(n, -1)
        h, hn, probs_t, p3 = _out_proj(flat(oaf), flat(oab), flat(obf), flat(obb), main, x2d, wo,
                                       row(norm_a[0]), row(norm_b[0]), row(ln2[0]),
                                       wrh, wrl, wrh2, wrl2, tm=256)
        slot, ts = _topk(probs_t, cap)
        dsp_steps, n_dsp = _route_steps(ts, cap, DSP_TOK)
        cmb_steps, n_cmb = _combine_steps(ts, cap)
        xs = _dispatch(hn, p3, slot, dsp_steps, n_dsp, cap)
        routed.append((x.shape, cap, xs, slot, cmb_steps, n_cmb, h))
    (_, cap_p, xs_p, *_), (_, cap_s, xs_s, *_) = routed
    ys = [_expert_mlp(xs_p, xs_s, w1b, w3b, w2b, e, cap_p, cap_s) for e in range(N_EXPERTS)]
    outs = []
    for k, (shape, cap, xs, slot, cmb_steps, n_cmb, h) in enumerate(routed):
        out = _combine3([y[k] for y in ys], slot, cmb_steps, n_cmb, h, row(ln_f))
        outs.append(out.reshape(shape))
    return tuple(outs)
```

```python
import functools

import jax
import jax.numpy as jnp
import numpy as np
from jax import lax
from jax.experimental import pallas as pl
from jax.experimental.pallas import tpu as pltpu

F32 = jnp.float32
BF16 = jnp.bfloat16

D_MODEL = 2048
A_HEADS, A_DK, A_DV = 8, 128, 128
B_HEADS, B_DK, B_DV = 4, 128, 256
GLA_LOWRANK = 16
GLA_GATE_NORM = 16.0
CONV_K = 5
CHUNK = 64
N_EXPERTS = 16
EC_CAPACITY = 2
EXPERT_FF = D_MODEL // 2
EPS = 1e-6

A_QK = A_HEADS * A_DK
A_V = A_HEADS * A_DV
B_QK = B_HEADS * B_DK
B_V = B_HEADS * B_DV
MAIN_WIDTH = 2 * A_QK + 2 * A_V + 2 * B_QK + 2 * B_V
SMALL_WIDTH = 128
LR_OFF = 4 * A_HEADS
V7X_VMEM_LIMIT = 56 * 1024 * 1024
CONV_HALO = 16
SCAN_ROWS = 512
CPS = SCAN_ROWS // CHUNK
GLA_LEVELS = 6
GATE_LANES = 128
GDN_HEADS_PER_PASS = 8


def _cparams(sem):
    return pltpu.CompilerParams(dimension_semantics=sem, vmem_limit_bytes=V7X_VMEM_LIMIT)


def _dot(a, b):
    return jnp.dot(a.astype(BF16), b.astype(BF16), preferred_element_type=F32)


def _dot_nt(a, b):
    return lax.dot_general(a.astype(BF16), b.astype(BF16), (((1,), (1,)), ((), ())),
                           preferred_element_type=F32)


def _dot_tn(a, b):
    return lax.dot_general(a.astype(BF16), b.astype(BF16), (((0,), (0,)), ((), ())),
                           preferred_element_type=F32)


def _silu(x):
    return x * (1.0 / (1.0 + jnp.exp(-x)))


def _in_proj_kernel(x_ref, ln_ref, wm_ref, ws_ref, sp_ref, main_ref, small_ref, hn_ref):
    j = pl.program_id(1)

    @pl.when(j == 0)
    def _():
        xf = x_ref[...]
        y = xf * lax.rsqrt(jnp.mean(xf * xf, axis=-1, keepdims=True) + EPS) * ln_ref[...]
        hn = y.astype(BF16)
        hn_ref[...] = hn
        s = jnp.dot(hn, ws_ref[...], preferred_element_type=F32)
        lane = lax.broadcasted_iota(jnp.int32, s.shape, 1)
        neg_a = -jnp.exp(sp_ref[0:1, :])
        z = s + sp_ref[1:2, :]
        softplus = jnp.maximum(z, 0.0) + jnp.log(1.0 + jnp.exp(-jnp.abs(z)))
        sig = 1.0 / (1.0 + jnp.exp(-s))
        small_ref[...] = jnp.where(lane < 2 * A_HEADS, sig,
                                   jnp.where(lane < 4 * A_HEADS, neg_a * softplus, s))

    main_ref[...] = jnp.dot(hn_ref[...], wm_ref[...], preferred_element_type=F32).astype(BF16)


def _in_proj(x2d, ln1, w_main, w_small, small_params, tm, tn):
    t = x2d.shape[0]
    return pl.pallas_call(
        _in_proj_kernel,
        grid=(t // tm, MAIN_WIDTH // tn),
        in_specs=[
            pl.BlockSpec((tm, D_MODEL), lambda i, j: (i, 0)),
            pl.BlockSpec((1, D_MODEL), lambda i, j: (0, 0)),
            pl.BlockSpec((D_MODEL, tn), lambda i, j: (0, j)),
            pl.BlockSpec((D_MODEL, SMALL_WIDTH), lambda i, j: (0, 0)),
            pl.BlockSpec((8, SMALL_WIDTH), lambda i, j: (0, 0)),
        ],
        out_specs=[
            pl.BlockSpec((tm, tn), lambda i, j: (i, j)),
            pl.BlockSpec((tm, SMALL_WIDTH), lambda i, j: (i, 0)),
        ],
        out_shape=[
            jax.ShapeDtypeStruct((t, MAIN_WIDTH), BF16),
            jax.ShapeDtypeStruct((t, SMALL_WIDTH), F32),
        ],
        scratch_shapes=[pltpu.VMEM((tm, D_MODEL), BF16)],
        compiler_params=_cparams(("arbitrary", "arbitrary")),
        name="in_proj",
    )(x2d, ln1, w_main, w_small, small_params)


def _chunk_iotas():
    ii = lax.broadcasted_iota(jnp.int32, (CHUNK, CHUNK), 0)
    jj = lax.broadcasted_iota(jnp.int32, (CHUNK, CHUNK), 1)
    return ii, jj


def _row_to_col(row, eye):
    return jnp.sum(jnp.where(eye, row, 0.0), axis=1, keepdims=True)


def _col_to_row(col, eye):
    return jnp.sum(jnp.where(eye, col, 0.0), axis=0, keepdims=True)


def _l2norm(x):
    return x * lax.rsqrt(jnp.sum(x * x, axis=-1, keepdims=True) + EPS)


def _delta_chunks(qs, ks, vs, beta_rows, g_rows, states, revs):
    ii, jj = _chunk_iotas()
    eye = ii == jj
    nc = range(len(qs))
    incl = [(ii <= jj) if r else (ii >= jj) for r in revs]
    strict = [(ii < jj) if r else (ii > jj) for r in revs]
    gc_col = [jnp.sum(jnp.where(incl[c], g_rows[c], 0.0), axis=1, keepdims=True) for c in nc]
    gc_row = [_col_to_row(gc_col[c], eye) for c in nc]
    beta_col = [_row_to_col(beta_rows[c], eye) for c in nc]
    g_tot = [jnp.sum(g_rows[c], axis=1, keepdims=True) for c in nc]
    decay = [jnp.where(incl[c], jnp.exp(jnp.where(incl[c], gc_col[c] - gc_row[c], 0.0)), 0.0)
             for c in nc]
    kb = [ks[c] * beta_col[c] for c in nc]
    kq = [_dot_nt(jnp.concatenate([kb[c], qs[c]], axis=0), ks[c]) for c in nc]
    kk = [kq[c][:CHUNK] for c in nc]
    qk = [kq[c][CHUNK:] for c in nc]
    p = [jnp.where(strict[c], -kk[c] * decay[c], 0.0) for c in nc]
    toff = p
    p = [_dot(p[c], p[c]) for c in nc]
    for _ in range(4):
        r = [_dot(jnp.concatenate([p[c], toff[c]], axis=0), p[c]) for c in nc]
        toff = [toff[c] + p[c] + r[c][CHUNK:] for c in nc]
        p = [r[c][:CHUNK] for c in nc]
    tp = [_dot(toff[c], p[c]) for c in nc]
    toff = [toff[c] + p[c] + tp[c] for c in nc]
    e_gc = [jnp.exp(gc_col[c]) for c in nc]
    rhs = [jnp.concatenate([vs[c] * beta_col[c], kb[c] * e_gc[c]], axis=1) for c in nc]
    sol = [rhs[c] + _dot(toff[c], rhs[c]) for c in nc]
    attn = [jnp.where(incl[c], qk[c] * decay[c], 0.0) for c in nc]
    wq = [_dot(jnp.concatenate([sol[c][:, A_DV:], qs[c] * e_gc[c]], axis=0), states[c]) for c in nc]
    ws = [wq[c][:CHUNK] for c in nc]
    qs_state = [wq[c][CHUNK:] for c in nc]
    v_new = [sol[c][:, :A_DV] - ws[c] for c in nc]
    av = [_dot(attn[c], v_new[c]) for c in nc]
    kv = [_dot_tn(ks[c] * jnp.exp(g_tot[c] - gc_col[c]), v_new[c]) for c in nc]
    outs = [qs_state[c] + av[c] for c in nc]
    new_states = [states[c] * jnp.exp(g_tot[c]) + kv[c] for c in nc]
    return outs, new_states


def _qkv_conv_kernel(blk, prv, nxt, cw_ref, out_ref, pad_ref):
    n = pl.program_id(1)
    nb = pl.num_programs(1)
    rows = blk.shape[1]
    pad_ref[0:CONV_HALO, :] = jnp.where(n == 0, 0.0, prv[0].astype(F32))
    pad_ref[CONV_HALO:CONV_HALO + rows, :] = blk[0].astype(F32)
    pad_ref[CONV_HALO + rows:, :] = jnp.where(n == nb - 1, 0.0, nxt[0].astype(F32))
    lo = CONV_HALO - CONV_K // 2
    for g in range((2 * A_QK + A_V) // A_DK):
        cols = slice(g * A_DK, (g + 1) * A_DK)
        w = cw_ref[:, cols]
        win = pad_ref[:, cols]
        total = win.shape[0]
        acc = None
        for j in range(CONV_K):
            shift = (CONV_HALO - lo - j) % total
            tap = win if shift == 0 else pltpu.roll(win, shift, 0)
            term = w[j:j + 1, :] * tap[CONV_HALO:CONV_HALO + rows, :]
            acc = term if acc is None else acc + term
        y = _silu(acc)
        if g < A_HEADS:
            y = _l2norm(y) * (A_DK ** -0.5)
        elif g < 2 * A_HEADS:
            y = _l2norm(y)
        out_ref[0, :, cols] = y.astype(out_ref.dtype)


def _qkv_conv(main3, conv8, rows):
    bsz, seq, _ = main3.shape
    width = 2 * A_QK + A_V
    hb = rows // CONV_HALO
    nhalo = seq // CONV_HALO
    return pl.pallas_call(
        _qkv_conv_kernel,
        grid=(bsz, seq // rows),
        in_specs=[
            pl.BlockSpec((1, rows, width), lambda b, n: (b, n, 0)),
            pl.BlockSpec((1, CONV_HALO, width), lambda b, n: (b, jnp.maximum(n * hb - 1, 0), 0)),
            pl.BlockSpec((1, CONV_HALO, width),
                         lambda b, n: (b, jnp.minimum((n + 1) * hb, nhalo - 1), 0)),
            pl.BlockSpec((8, width), lambda b, n: (0, 0)),
        ],
        out_specs=pl.BlockSpec((1, rows, width), lambda b, n: (b, n, 0)),
        out_shape=jax.ShapeDtypeStruct((bsz, seq, width), BF16),
        scratch_shapes=[pltpu.VMEM((rows + 2 * CONV_HALO, width), F32)],
        compiler_params=_cparams(("arbitrary", "arbitrary")),
        name="qkv_conv",
    )(main3, main3, main3, conv8)


def _gdn_kernel(blk_f, blk_b, sc_f, sc_b, of_ref, ob_ref, st_ref):
    n = pl.program_id(1)

    @pl.when(n == 0)
    def _():
        st_ref[...] = jnp.zeros_like(st_ref)

    def load(d, row0, a, h):
        blk = blk_f if d == 0 else blk_b
        col = a * A_QK + h * A_DK
        return blk[0, pl.ds(row0, CHUNK), col:col + A_DK].astype(F32)

    def body(s, carry):
        for h0 in range(0, A_HEADS, GDN_HEADS_PER_PASS):
            chains = []
            for d, sc_ref in enumerate((sc_f, sc_b)):
                sc = s if d == 0 else CPS - 1 - s
                row0 = pl.multiple_of(sc * CHUNK, CHUNK)
                for h in range(h0, h0 + GDN_HEADS_PER_PASS):
                    chains.append((d, h, sc, row0, sc_ref))
            qs = [load(d, row0, 0, h) for d, h, sc, row0, _ in chains]
            ks = [load(d, row0, 1, h) for d, h, sc, row0, _ in chains]
            vs = [load(d, row0, 2, h) for d, h, sc, row0, _ in chains]
            betas = [r[0, 0, d, h, pl.ds(sc, 1), :] for d, h, sc, row0, r in chains]
            gs = [r[0, 0, 2 + d, h, pl.ds(sc, 1), :] for d, h, sc, row0, r in chains]
            states = [st_ref[d, h] for d, h, sc, row0, _ in chains]
            outs, new_states = _delta_chunks(qs, ks, vs, betas, gs, states,
                                             [d == 1 for d, *_ in chains])
            for (d, h, sc, row0, _), o, st in zip(chains, outs, new_states):
                st_ref[d, h] = st
                out_ref = of_ref if d == 0 else ob_ref
                out_ref[0, pl.ds(row0, CHUNK), h * A_DV:(h + 1) * A_DV] = o.astype(out_ref.dtype)
        return carry

    lax.fori_loop(0, CPS, body, 0)


def _scal_layout(small):
    bsz, seq = small.shape[:2]
    s = small[..., :4 * A_HEADS].reshape(bsz, seq // SCAN_ROWS, CPS, CHUNK, 4, A_HEADS)
    return jnp.transpose(s, (0, 1, 4, 5, 2, 3))


def _conv_layout(conv_w):
    return jnp.pad(conv_w.astype(F32), ((0, 8 - CONV_K), (0, 0)))


def _gdn(qkv, scal):
    bsz, seq, width = qkv.shape
    nb = seq // SCAN_ROWS

    def specs(blk_of):
        return (pl.BlockSpec((1, SCAN_ROWS, width), lambda b, n: (b, blk_of(n), 0)),
                pl.BlockSpec((1, 1, 4, A_HEADS, CPS, CHUNK), lambda b, n: (b, blk_of(n), 0, 0, 0, 0)))

    qf, sf = specs(lambda n: n)
    qb, sb = specs(lambda n: nb - 1 - n)
    out_sds = jax.ShapeDtypeStruct((bsz, seq, A_V), BF16)
    return pl.pallas_call(
        _gdn_kernel,
        grid=(bsz, nb),
        in_specs=[qf, qb, sf, sb],
        out_specs=[
            pl.BlockSpec((1, SCAN_ROWS, A_V), lambda b, n: (b, n, 0)),
            pl.BlockSpec((1, SCAN_ROWS, A_V), lambda b, n: (b, nb - 1 - n, 0)),
        ],
        out_shape=[out_sds, out_sds],
        scratch_shapes=[pltpu.VMEM((2, A_HEADS, A_DK, A_DV), F32)],
        compiler_params=_cparams(("arbitrary", "arbitrary")),
        name="gdn",
    )(qkv, qkv, scal, scal)


GLA_BATCH = 2
GLA_MM_LEVELS = (4, 5)


def _gla_level_weights():
    c = CHUNK
    w = np.zeros((2, 3 * c, c), np.float32)
    for d in range(2):
        for blk, l in enumerate(GLA_MM_LEVELS):
            s = c >> (l + 1)
            for i in range(c):
                mid = (i // (2 * s)) * 2 * s + s
                if d == 0:
                    ts = range(mid, i + 1) if i >= mid else range(i + 1, mid)
                else:
                    ts = range(i, mid) if i < mid else range(mid, i)
                for t in ts:
                    w[d, blk * c + i, t] = 1.0
        for i in range(c):
            for t in range(c):
                before = t <= i if d == 0 else t >= i
                w[d, 2 * c + i, t] = 1.0 if before else 0.0
    return w


def _gla_kernel(qk_f, v_f, sm_f, qk_b, v_b, sm_b, wg_ref, bg_ref, wl_ref, of_ref, ob_ref, st_ref):
    n = pl.program_id(1)

    @pl.when(n == 0)
    def _():
        st_ref[...] = jnp.zeros_like(st_ref)

    ii, jj = _chunk_iotas()
    eye = ii == jj
    tok = lax.broadcasted_iota(jnp.int32, (CHUNK, 1), 0)

    streams = ((qk_f, v_f, sm_f, of_ref), (qk_b, v_b, sm_b, ob_ref))
    lanes = [(bi, d) for bi in range(qk_f.shape[0]) for d in range(2)]
    dirs = range(len(lanes))
    chains = [(p, h) for p in dirs for h in range(B_HEADS)]
    masks = [ii // (CHUNK >> l) == jj // (CHUNK >> l) for l in range(GLA_LEVELS)]

    def body(s, carry):
        rows = [pl.ds(pl.multiple_of((s if d == 0 else CPS - 1 - s) * CHUNK, CHUNK), CHUNK)
                for _, d in lanes]

        def blk(p, which, cols):
            bi, d = lanes[p]
            return streams[d][which][bi, rows[p], cols]

        z = [_dot(blk(p, 2, slice(None)), wg_ref[lanes[p][1]]) + bg_ref[lanes[p][1]]
             for p in dirs]
        la = [(jnp.minimum(z[d], 0.0) - jnp.log(1.0 + jnp.exp(-jnp.abs(z[d]))))
              * (1.0 / GLA_GATE_NORM) for d in dirs]
        hi = [la[d].astype(BF16) for d in dirs]
        r1 = [la[d] - hi[d].astype(F32) for d in dirs]
        mid = [r1[d].astype(BF16) for d in dirs]
        low = [(r1[d] - mid[d].astype(F32)).astype(BF16) for d in dirs]
        ex = [jnp.dot(wl_ref[lanes[d][1]], hi[d], preferred_element_type=F32)
              + jnp.dot(wl_ref[lanes[d][1]], mid[d], preferred_element_type=F32)
              + jnp.dot(wl_ref[lanes[d][1]], low[d], preferred_element_type=F32)
              for d in dirs]
        bcum = [ex[d][2 * CHUNK:] for d in dirs]
        b_tot = [bcum[p][CHUNK - 1:CHUNK] if lanes[p][1] == 0 else bcum[p][0:1] for p in dirs]
        q_all = [blk(p, 0, slice(0, B_QK)).astype(F32) * (B_DK ** -0.5) for p in dirs]
        k_all = [blk(p, 0, slice(B_QK, 2 * B_QK)).astype(F32) for p in dirs]
        qs, ks = [], []
        for l in range(GLA_LEVELS):
            half = CHUNK >> (l + 1)
            right = (tok // half) % 2 == 1
            q_side = [right if d == 0 else jnp.logical_not(right) for _, d in lanes]
            if l in GLA_MM_LEVELS:
                wrow = GLA_MM_LEVELS.index(l) * CHUNK
                expo = [ex[d][wrow:wrow + CHUNK] for d in dirs]
            else:
                expo = []
                for d in dirs:
                    ref = jnp.concatenate(
                        [jnp.broadcast_to(bcum[d][g + half - 1 + lanes[d][1]:g + half + lanes[d][1]],
                                          (2 * half, B_QK))
                         for g in range(0, CHUNK, 2 * half)], axis=0)
                    expo.append(jnp.where(q_side[d], bcum[d] - ref, ref - bcum[d]))
            e_l = [jnp.exp(expo[d]) for d in dirs]
            qs.append([jnp.where(q_side[d], q_all[d] * e_l[d], 0.0).astype(BF16) for d in dirs])
            ks.append([jnp.where(q_side[d], 0.0, k_all[d] * e_l[d]).astype(BF16) for d in dirs])
        q_dec = [(q_all[d] * jnp.exp(bcum[d])).astype(BF16) for d in dirs]
        k_dec = [(k_all[d] * jnp.exp(b_tot[d] - bcum[d])).astype(BF16) for d in dirs]
        e_tot = [jnp.exp(b_tot[d]) for d in dirs]
        ck = [slice(h * B_DK, (h + 1) * B_DK) for h in range(B_HEADS)]
        cv = [slice(h * B_DV, (h + 1) * B_DV) for h in range(B_HEADS)]
        attn = [jnp.where(eye, _dot_nt(q_all[d][:, ck[h]], k_all[d][:, ck[h]]), 0.0)
                for d, h in chains]
        for l in range(GLA_LEVELS):
            part = [_dot_nt(qs[l][d][:, ck[h]], ks[l][d][:, ck[h]]) for d, h in chains]
            attn = [attn[c] + jnp.where(masks[l], part[c], 0.0) for c in range(len(chains))]
        v = [blk(p, 1, cv[h]) for p, h in chains]
        st = [st_ref[lanes[p][0], lanes[p][1], h] for p, h in chains]
        o_state = [_dot_nt(q_dec[d][:, ck[h]], st[c]) for c, (d, h) in enumerate(chains)]
        o_local = [_dot(attn[c], v[c]) for c in range(len(chains))]
        kv = [_dot_tn(v[c], k_dec[d][:, ck[h]]) for c, (d, h) in enumerate(chains)]
        for c, (p, h) in enumerate(chains):
            bi, d = lanes[p]
            st_ref[bi, d, h] = st[c] * e_tot[p][:, ck[h]] + kv[c]
            streams[d][3][bi, rows[p], cv[h]] = (o_state[c] + o_local[c]).astype(of_ref.dtype)
        return carry

    lax.fori_loop(0, CPS, body, 0)


def _gla(main3, small3, wg, bg, wl):
    bsz, seq, _ = main3.shape
    nb = seq // SCAN_ROWS
    qk_blk = (2 * A_QK + 2 * A_V) // (2 * B_QK)
    v_blk = (2 * A_QK + 2 * A_V + 2 * B_QK) // B_V

    gb = GLA_BATCH
    assert bsz % gb == 0 and seq % SCAN_ROWS == 0, (bsz, seq)

    def stream(blk_of):
        return [
            pl.BlockSpec((gb, SCAN_ROWS, 2 * B_QK), lambda b, n: (b, blk_of(n), qk_blk)),
            pl.BlockSpec((gb, SCAN_ROWS, B_V), lambda b, n: (b, blk_of(n), v_blk)),
            pl.BlockSpec((gb, SCAN_ROWS, SMALL_WIDTH), lambda b, n: (b, blk_of(n), 0)),
        ]

    out_sds = jax.ShapeDtypeStruct((bsz, seq, B_V), BF16)
    return pl.pallas_call(
        _gla_kernel,
        grid=(bsz // gb, nb),
        in_specs=stream(lambda n: n) + stream(lambda n: nb - 1 - n) + [
            pl.BlockSpec((2, SMALL_WIDTH, B_QK), lambda b, n: (0, 0, 0)),
            pl.BlockSpec((2, 1, B_QK), lambda b, n: (0, 0, 0)),
            pl.BlockSpec((2, 3 * CHUNK, CHUNK), lambda b, n: (0, 0, 0)),
        ],
        out_specs=[
            pl.BlockSpec((gb, SCAN_ROWS, B_V), lambda b, n: (b, n, 0)),
            pl.BlockSpec((gb, SCAN_ROWS, B_V), lambda b, n: (b, nb - 1 - n, 0)),
        ],
        out_shape=[out_sds, out_sds],
        scratch_shapes=[pltpu.VMEM((gb, 2, B_HEADS, B_DV, B_DK), F32)],
        compiler_params=_cparams(("arbitrary", "arbitrary")),
        name="gla",
    )(main3, main3, small3, main3, main3, small3, wg, bg, wl)


def _gla_gate_weights(gla_w_f, gla_b_f, gla_w_b, gla_b_b):
    wg = jnp.zeros((2, SMALL_WIDTH, B_QK), F32)
    wg = wg.at[0, LR_OFF:LR_OFF + GLA_LOWRANK].set(gla_w_f.astype(F32))
    wg = wg.at[1, LR_OFF + GLA_LOWRANK:LR_OFF + 2 * GLA_LOWRANK].set(gla_w_b.astype(F32))
    bg = jnp.stack([gla_b_f, gla_b_b]).astype(F32).reshape(2, 1, B_QK)
    return wg.astype(BF16), bg


def _out_proj_kernel(oaf, oab, obf, obb, ga, gb, x_ref, wo_ref, na_ref, nb_ref, ln2_ref,
                     wrh_ref, wrl_ref, wrh2_ref, wrl2_ref, h_ref, hn_ref, pt_ref, p3_ref,
                     mix_ref, hres_ref):
    s = pl.program_id(0)

    @pl.when(s == 0)
    def _():
        mix_ref[...] = jnp.zeros_like(mix_ref)
        hres_ref[...] = jnp.zeros_like(hres_ref)

    cur, prv = 0, 1
    mix_ref[prv] = mix_ref[cur]
    hres_ref[prv] = hres_ref[cur]

    def head_norm(of_ref, ob_ref, g_ref, w_ref, width, h, base):
        c = slice(h * width, (h + 1) * width)
        o = of_ref[:, c].astype(F32) + ob_ref[:, c].astype(F32)
        y = o * lax.rsqrt(jnp.mean(o * o, axis=-1, keepdims=True) + EPS) * w_ref[...]
        mix_ref[cur, :, base + h * width:base + (h + 1) * width] = (
            y * _silu(g_ref[:, c].astype(F32))).astype(BF16)

    def mix(q):
        for h in range(q * A_HEADS // OUT_PIECES, (q + 1) * A_HEADS // OUT_PIECES):
            head_norm(oaf, oab, ga, na_ref, A_DV, h, 0)
        for h in range(q * B_HEADS // OUT_PIECES, (q + 1) * B_HEADS // OUT_PIECES):
            head_norm(obf, obb, gb, nb_ref, B_DV, h, A_V)

    def project(q):
        nsl = slice(q * D_MODEL // OUT_PIECES, (q + 1) * D_MODEL // OUT_PIECES)
        hres_ref[cur, :, nsl] = x_ref[:, nsl] + jnp.dot(mix_ref[prv], wo_ref[:, nsl],
                                                        preferred_element_type=F32)

    def route(c):
        rows = x_ref.shape[0] // 2
        rs = slice(c * rows, (c + 1) * rows)
        hres = hres_ref[prv, rs, :]
        h_ref[rs, :] = hres
        hn = hres * lax.rsqrt(jnp.mean(hres * hres, axis=-1, keepdims=True) + EPS) * ln2_ref[...]
        hi = hn.astype(BF16)
        hn_ref[rs, :] = hi
        lo = (hn - hi.astype(F32)).astype(BF16)
        lt = _dot_nt(wrh_ref[...], hi) + _dot_nt(wrl_ref[...], hi) + _dot_nt(wrh_ref[...], lo)
        e = jnp.exp(lt - jnp.max(lt, axis=0, keepdims=True))
        pt_ref[:, rs] = e / jnp.sum(e, axis=0, keepdims=True)
        l2 = _dot(hi, wrh2_ref[...]) + _dot(hi, wrl2_ref[...]) + _dot(lo, wrh2_ref[...])
        lane = lax.broadcasted_iota(jnp.int32, l2.shape, 1)
        l2 = jnp.where(lane < N_EXPERTS, l2, -jnp.inf)
        e2 = jnp.exp(l2 - jnp.max(l2, axis=1, keepdims=True))
        p2 = e2 / jnp.sum(e2, axis=1, keepdims=True)
        g_hi = p2.astype(BF16)
        r1 = p2 - g_hi.astype(F32)
        g_mid = r1.astype(BF16)
        g_lo = (r1 - g_mid.astype(F32)).astype(BF16)
        p3 = (g_hi.astype(F32) + pltpu.roll(g_mid.astype(F32), N_EXPERTS, 1)
              + pltpu.roll(g_lo.astype(F32), 2 * N_EXPERTS, 1))
        p3_ref[rs, :] = p3.astype(BF16)

    route(0)
    for q in range(OUT_PIECES):
        mix(q)
        project(q)
        if q == OUT_PIECES // 2:
            route(1)


OUT_PIECES = 4


def _out_proj(oaf, oab, obf, obb, main, x2d, wo, na, nb, ln2, wrh, wrl, wrh2, wrl2, tm):
    t = x2d.shape[0]
    nt = t // tm
    ga_blk = (2 * A_QK + A_V) // A_V
    gb_blk = (2 * A_QK + 2 * A_V + 2 * B_QK + B_V) // B_V
    ahead = lambda s: jnp.minimum(s, nt - 1)
    mid = lambda s: jnp.clip(s - 1, 0, nt - 1)
    behind = lambda s: jnp.maximum(s - 2, 0)
    row_in = lambda w: pl.BlockSpec((tm, w), lambda s: (ahead(s), 0))
    row_out = lambda w: pl.BlockSpec((tm, w), lambda s: (behind(s), 0))
    full = lambda a, b: pl.BlockSpec((a, b), lambda s: (0, 0))
    return pl.pallas_call(
        _out_proj_kernel,
        grid=(nt + 2,),
        in_specs=[
            row_in(A_V), row_in(A_V), row_in(B_V), row_in(B_V),
            pl.BlockSpec((tm, A_V), lambda s: (ahead(s), ga_blk)),
            pl.BlockSpec((tm, B_V), lambda s: (ahead(s), gb_blk)),
            pl.BlockSpec((tm, D_MODEL), lambda s: (mid(s), 0)),
            full(D_MODEL, D_MODEL), full(1, A_DV), full(1, B_DV), full(1, D_MODEL),
            full(N_EXPERTS, D_MODEL), full(N_EXPERTS, D_MODEL),
            full(D_MODEL, GATE_LANES), full(D_MODEL, GATE_LANES),
        ],
        out_specs=[row_out(D_MODEL), row_out(D_MODEL),
                   pl.BlockSpec((N_EXPERTS, tm), lambda s: (0, behind(s))), row_out(GATE_LANES)],
        out_shape=[
            jax.ShapeDtypeStruct((t, D_MODEL), F32),
            jax.ShapeDtypeStruct((t, D_MODEL), BF16),
            jax.ShapeDtypeStruct((N_EXPERTS, t), F32),
            jax.ShapeDtypeStruct((t, GATE_LANES), BF16),
        ],
        scratch_shapes=[pltpu.VMEM((2, tm, D_MODEL), BF16), pltpu.VMEM((2, tm, D_MODEL), F32)],
        compiler_params=_cparams(("arbitrary",)),
        name="out_proj",
    )(oaf, oab, obf, obb, main, main, x2d, wo, na, nb, ln2, wrh, wrl, wrh2, wrl2)


TOPK_LANES = 256


def _topk_kernel(p_ref, slot_ref, ts_ref, *, cap):
    n = p_ref.shape[1]
    n_steps = n // TOPK_LANES
    bits = pltpu.bitcast(p_ref[...], jnp.int32)

    def bisect(i, thr):
        cand = thr | jnp.left_shift(jnp.int32(1), 30 - i)
        cnt = jnp.sum(jnp.where(bits >= cand, 1.0, 0.0), axis=1, keepdims=True)
        return jnp.where(cnt >= cap, cand, thr)

    thr = lax.fori_loop(0, 31, bisect, jnp.zeros((N_EXPERTS, 1), jnp.int32))
    need = cap - jnp.sum(jnp.where(bits > thr, 1.0, 0.0), axis=1, keepdims=True)
    r = lax.broadcasted_iota(jnp.int32, (TOPK_LANES, TOPK_LANES), 0)
    c = lax.broadcasted_iota(jnp.int32, (TOPK_LANES, TOPK_LANES), 1)
    tri = jnp.where(r <= c, 1.0, 0.0).astype(BF16)
    step_lane = lax.broadcasted_iota(jnp.int32, (N_EXPERTS, n_steps), 1)

    ts_ref[...] = jnp.zeros_like(ts_ref)

    def step(j, carry):
        ties_before, sel_before = carry
        lanes = pl.ds(pl.multiple_of(j * TOPK_LANES, TOPK_LANES), TOPK_LANES)
        pb = pltpu.bitcast(p_ref[:, lanes], jnp.int32)
        tie = jnp.where(pb == thr, 1.0, 0.0)
        tie_incl = jnp.dot(tie.astype(BF16), tri, preferred_element_type=F32)
        sel = jnp.logical_or(pb > thr, jnp.logical_and(pb == thr, ties_before + tie_incl - tie < need))
        self_f = jnp.where(sel, 1.0, 0.0)
        sel_incl = jnp.dot(self_f.astype(BF16), tri, preferred_element_type=F32)
        pos = sel_before + sel_incl - self_f
        slot_ref[:, lanes] = jnp.where(sel, pos, -1.0).astype(jnp.int32)
        ts_ref[...] = jnp.where(step_lane == j, sel_before.astype(jnp.int32), ts_ref[...])
        return (ties_before + tie_incl[:, TOPK_LANES - 1:],
                sel_before + sel_incl[:, TOPK_LANES - 1:])

    zero = jnp.zeros((N_EXPERTS, 1), F32)
    lax.fori_loop(0, n_steps, step, (zero, zero))


def _topk(probs_t, cap):
    n = probs_t.shape[1]
    return pl.pallas_call(
        functools.partial(_topk_kernel, cap=cap),
        out_shape=[
            jax.ShapeDtypeStruct((N_EXPERTS, n), jnp.int32),
            jax.ShapeDtypeStruct((N_EXPERTS, n // TOPK_LANES), jnp.int32),
        ],
        compiler_params=pltpu.CompilerParams(vmem_limit_bytes=V7X_VMEM_LIMIT),
        name="topk",
    )(probs_t)


RT_TOK = TOPK_LANES
DSP_TOK = 2 * TOPK_LANES
RT_ROWS = 128
RT_SPECS = 2 * N_EXPERTS
RT_GROUP = 4
XS_WIDTH = D_MODEL + GATE_LANES
SF_VALID, SF_FIRST, SF_LAST = 1, 2, 4
BF_ACTIVE, BF_FIRST = 1, 2


def _route_steps(ts, cap, tok):
    ts = ts[:, ::tok // TOPK_LANES]
    n_sub = ts.shape[1]
    tse = jnp.concatenate([ts, jnp.full((N_EXPERTS, 1), cap, jnp.int32)], axis=1)
    a, b = tse[:, :-1], tse[:, 1:]
    kf = jnp.minimum(a, cap - 1) // RT_ROWS
    kl = jnp.where(b > a, (b - 1) // RT_ROWS, kf)
    rounds = jnp.max((kl - kf + 2) // 2, axis=0)
    incl = jnp.cumsum(rounds)
    total = incl[-1]
    n_steps = ((tok // RT_ROWS + 2) // 2) * n_sub
    idx = jnp.arange(n_steps, dtype=jnp.int32)
    valid = idx < total
    idc = jnp.minimum(idx, total - 1)
    j = jnp.sum(incl[None, :] <= idc[:, None], axis=1).astype(jnp.int32)
    r = idc - (incl[j] - rounds[j])
    rep = lambda v: jnp.repeat(jnp.take(v, j, axis=1), 2, axis=0)
    par = jnp.tile(jnp.arange(2, dtype=jnp.int32), N_EXPERTS)[:, None]
    k0 = rep(kf) + 2 * r[None, :]
    cand = k0 + (par - k0) % 2
    a_s, b_s, kl_s = rep(a), rep(b), rep(kl)
    active = valid[None, :] & (cand <= kl_s) & (b_s > a_s)
    first = active & (a_s <= cand * RT_ROWS) & (cand * RT_ROWS < b_s)
    held = lax.cummax(jnp.where(active, cand, -1), axis=1)
    held = jnp.where(held < 0, par, held)
    order = lambda v: v.reshape(-1).astype(jnp.int32)
    bflag = BF_ACTIVE * active + BF_FIRST * first
    sflag = jnp.where(valid, SF_VALID + SF_FIRST * (r == 0) + SF_LAST * (r == rounds[j] - 1), 0)
    return (j, order(held), order(bflag), sflag.astype(jnp.int32)), total.astype(jnp.int32)


def _route_groups():
    return [[(RT_GROUP // 2 * g + q // 2, q % 2) for q in range(RT_GROUP)]
            for g in range(RT_SPECS // RT_GROUP)]


def _spec_at(table, spec, i):
    return spec * (table.shape[0] // RT_SPECS) + i


def _one_hots(i, blk, flg, slot_ref, specs):
    iota = lax.broadcasted_iota(jnp.int32, (RT_ROWS, slot_ref.shape[1]), 0)
    ohs = []
    for e, par in specs:
        k = _spec_at(blk, e * 2 + par, i)
        s0 = jnp.where((flg[k] & BF_ACTIVE) != 0, blk[k] * RT_ROWS, -(1 << 30))
        ohs.append(jnp.where((iota + s0) == slot_ref[e:e + 1, :], 1.0, 0.0).astype(BF16))
    return jnp.concatenate(ohs, axis=0)


def _dispatch_kernel(tile, blk, flg, sflg, x_ref, p3_ref, slot_ref, *outs):
    i = pl.program_id(0)

    @pl.when(i == 0)
    def _():
        for out in outs:
            out[...] = jnp.zeros_like(out)

    @pl.when((sflg[i] & SF_VALID) != 0)
    def _():
        xa = jnp.concatenate([x_ref[...], p3_ref[...]], axis=1)
        for specs in _route_groups():
            res = jnp.dot(_one_hots(i, blk, flg, slot_ref, specs), xa, preferred_element_type=F32)
            for q, (e, par) in enumerate(specs):
                out = outs[e * 2 + par]
                first = (flg[_spec_at(flg, e * 2 + par, i)] & BF_FIRST) != 0
                prev = out[...]
                prev = jnp.where(first, jnp.zeros_like(prev), prev)
                out[...] = prev + res[q * RT_ROWS:(q + 1) * RT_ROWS].astype(out.dtype)


def _dispatch(hn, p3, slot, steps, n_steps, cap):
    def out_spec(e, par):
        return pl.BlockSpec((RT_ROWS, XS_WIDTH),
                            lambda i, t, b, f, s: (b[_spec_at(b, e * 2 + par, i)] // 2, 0))

    grid_spec = pltpu.PrefetchScalarGridSpec(
        num_scalar_prefetch=4,
        grid=(n_steps,),
        in_specs=[
            pl.BlockSpec((DSP_TOK, D_MODEL), lambda i, t, b, f, s: (t[i], 0)),
            pl.BlockSpec((DSP_TOK, GATE_LANES), lambda i, t, b, f, s: (t[i], 0)),
            pl.BlockSpec((N_EXPERTS, DSP_TOK), lambda i, t, b, f, s: (0, t[i])),
        ],
        out_specs=[out_spec(e, par) for e in range(N_EXPERTS) for par in range(2)],
    )
    sds = jax.ShapeDtypeStruct((cap // 2, XS_WIDTH), BF16)
    return pl.pallas_call(
        _dispatch_kernel,
        grid_spec=grid_spec,
        out_shape=[sds] * RT_SPECS,
        compiler_params=_cparams(("arbitrary",)),
        name="moe_dispatch",
    )(*steps, hn, p3, slot)


MLP_ROWS = 4 * RT_ROWS


def _expert_mlp_kernel(xe_p, xo_p, xe_s, xo_s, w1_ref, w3_ref, w2_ref, yp_ref, ys_ref, *, e, steps_p):
    def run(xe_ref, xo_ref, y_ref):
        r = RT_ROWS
        xa = jnp.concatenate([xe_ref[0:r], xo_ref[0:r], xe_ref[r:2 * r], xo_ref[r:2 * r]], axis=0)
        xs = xa[:, :D_MODEL]
        g3 = xa[:, D_MODEL:].astype(F32)
        lane = lax.broadcasted_iota(jnp.int32, g3.shape, 1)
        mine = jnp.logical_and((lane & (N_EXPERTS - 1)) == e, lane < 3 * N_EXPERTS)
        gate = jnp.sum(jnp.where(mine, g3, 0.0), axis=1, keepdims=True)
        h1 = jnp.dot(xs, w1_ref[0], preferred_element_type=F32)
        h3 = jnp.dot(xs, w3_ref[0], preferred_element_type=F32)
        hid = (_silu(h1) * h3).astype(BF16)
        y_ref[...] = (jnp.dot(hid, w2_ref[0], preferred_element_type=F32) * gate).astype(y_ref.dtype)

    m = pl.program_id(0)

    @pl.when(m < steps_p)
    def _():
        run(xe_p, xo_p, yp_ref)

    @pl.when(m >= steps_p)
    def _():
        run(xe_s, xo_s, ys_ref)


def _expert_mlp(xs_p, xs_s, w1, w3, w2, e, cap_p, cap_s):
    steps_p, steps_s = cap_p // MLP_ROWS, cap_s // MLP_ROWS
    first = lambda m: (jnp.minimum(m, steps_p - 1), 0)
    second = lambda m: (jnp.maximum(m - steps_p, 0), 0)
    half = lambda idx: pl.BlockSpec((MLP_ROWS // 2, XS_WIDTH), idx)
    return pl.pallas_call(
        functools.partial(_expert_mlp_kernel, e=e, steps_p=steps_p),
        grid=(steps_p + steps_s,),
        in_specs=[
            half(first), half(first), half(second), half(second),
            pl.BlockSpec((1, D_MODEL, EXPERT_FF), lambda m: (e, 0, 0)),
            pl.BlockSpec((1, D_MODEL, EXPERT_FF), lambda m: (e, 0, 0)),
            pl.BlockSpec((1, EXPERT_FF, D_MODEL), lambda m: (e, 0, 0)),
        ],
        out_specs=[pl.BlockSpec((MLP_ROWS, D_MODEL), first),
                   pl.BlockSpec((MLP_ROWS, D_MODEL), second)],
        out_shape=[jax.ShapeDtypeStruct((cap_p, D_MODEL), BF16),
                   jax.ShapeDtypeStruct((cap_s, D_MODEL), BF16)],
        compiler_params=_cparams(("arbitrary",)),
        name="expert_mlp",
    )(xs_p[2 * e], xs_p[2 * e + 1], xs_s[2 * e], xs_s[2 * e + 1], w1, w3, w2)


CMB_WIN = 64
CMB_ALIGN = 16


def _combine_steps(ts, cap):
    n_sub = ts.shape[1]
    tse = jnp.concatenate([ts, jnp.full((N_EXPERTS, 1), cap, jnp.int32)], axis=1)
    a, b = tse[:, :-1], tse[:, 1:]
    a16 = (a // CMB_ALIGN) * CMB_ALIGN
    rounds = jnp.maximum(jnp.max((b - a16 + CMB_WIN - 1) // CMB_WIN, axis=0), 1)
    incl = jnp.cumsum(rounds)
    total = incl[-1]
    n_steps = ((RT_TOK + CMB_ALIGN) // CMB_WIN + 1) * n_sub
    idx = jnp.arange(n_steps, dtype=jnp.int32)
    valid = idx < total
    idc = jnp.minimum(idx, total - 1)
    j = jnp.sum(incl[None, :] <= idc[:, None], axis=1).astype(jnp.int32)
    r = idc - (incl[j] - rounds[j])
    base = jnp.take(a16, j, axis=1) + CMB_WIN * r[None, :]
    wstart = jnp.clip(base, 0, cap - CMB_WIN) // CMB_ALIGN
    order = lambda v: v.reshape(-1).astype(jnp.int32)
    sflag = jnp.where(valid, SF_VALID + SF_FIRST * (r == 0) + SF_LAST * (r == rounds[j] - 1), 0)
    return (j, order(wstart), order(base), sflag.astype(jnp.int32)), total.astype(jnp.int32)


def _combine3_kernel(tile, ws, base, sflg, slot_ref, h_ref, lnf_ref, *rest):
    ys = rest[:N_EXPERTS]
    out_ref, acc_ref = rest[N_EXPERTS], rest[N_EXPERTS + 1]
    i = pl.program_id(0)
    sf = sflg[i]

    @pl.when((sf & SF_FIRST) != 0)
    def _():
        acc_ref[...] = jnp.zeros_like(acc_ref)

    @pl.when((sf & SF_VALID) != 0)
    def _():
        iota = lax.broadcasted_iota(jnp.int32, (CMB_WIN, RT_TOK), 0)
        ohs = []
        for e in range(N_EXPERTS):
            k = e * (base.shape[0] // N_EXPERTS) + i
            rel = slot_ref[e:e + 1, :] - base[k]
            rel = jnp.where(rel >= 0, rel, -(1 << 30))
            shift = base[k] - ws[k] * CMB_ALIGN
            ohs.append(jnp.where((iota - shift) == rel, 1.0, 0.0).astype(BF16))
        ycat = jnp.concatenate([y[...] for y in ys], axis=0)
        acc_ref[...] += _dot_tn(jnp.concatenate(ohs, axis=0), ycat)

    @pl.when((sf & SF_LAST) != 0)
    def _():
        hh = h_ref[...] + acc_ref[...]
        out_ref[...] = hh * lax.rsqrt(jnp.mean(hh * hh, axis=-1, keepdims=True) + EPS) * lnf_ref[...]


def _combine3(ys, slot, steps, n_steps, h, ln_f):
    n = h.shape[0]

    def y_spec(e):
        return pl.BlockSpec((pl.Element(CMB_WIN), pl.Element(D_MODEL)),
                            lambda i, t, w, b, s: (w[e * (w.shape[0] // N_EXPERTS) + i] * CMB_ALIGN, 0))

    grid_spec = pltpu.PrefetchScalarGridSpec(
        num_scalar_prefetch=4,
        grid=(n_steps,),
        in_specs=[
            pl.BlockSpec((N_EXPERTS, RT_TOK), lambda i, t, w, b, s: (0, t[i])),
            pl.BlockSpec((RT_TOK, D_MODEL), lambda i, t, w, b, s: (t[i], 0)),
            pl.BlockSpec((1, D_MODEL), lambda i, t, w, b, s: (0, 0)),
        ] + [y_spec(e) for e in range(N_EXPERTS)],
        out_specs=pl.BlockSpec((RT_TOK, D_MODEL), lambda i, t, w, b, s: (t[i], 0)),
        scratch_shapes=[pltpu.VMEM((RT_TOK, D_MODEL), F32)],
    )
    return pl.pallas_call(
        _combine3_kernel,
        grid_spec=grid_spec,
        out_shape=jax.ShapeDtypeStruct((n, D_MODEL), F32),
        compiler_params=_cparams(("arbitrary",)),
        name="moe_combine",
    )(*steps, slot, h, ln_f, *ys)


def _prep_in_weights(w_in, a_log_f, a_log_b, dt_bias_f, dt_bias_b):
    offs = np.cumsum([0, A_QK, A_QK, A_V, A_V, A_HEADS, A_HEADS, A_HEADS, A_HEADS,
                      B_QK, B_QK, B_V, B_V, GLA_LOWRANK, GLA_LOWRANK])
    seg = [w_in[:, offs[i]:offs[i + 1]] for i in range(14)]
    w_main = jnp.concatenate(seg[0:4] + seg[8:12], axis=1).astype(BF16)
    w_small = jnp.concatenate(seg[4:8] + seg[12:14], axis=1)
    w_small = jnp.pad(w_small, ((0, 0), (0, SMALL_WIDTH - w_small.shape[1]))).astype(BF16)
    pad = SMALL_WIDTH - 4 * A_HEADS
    z = jnp.zeros((2 * A_HEADS,), F32)
    a_row = jnp.concatenate([z, a_log_f.astype(F32), a_log_b.astype(F32), jnp.zeros((pad,), F32)])
    dt_row = jnp.concatenate([z, dt_bias_f.astype(F32), dt_bias_b.astype(F32), jnp.zeros((pad,), F32)])
    small_params = jnp.zeros((8, SMALL_WIDTH), F32).at[0].set(a_row).at[1].set(dt_row)
    return w_main, w_small, small_params


def kernel(x_prompt, x_sample, ln1, w_in, conv_w, a_log_f, a_log_b, dt_bias_f, dt_bias_b, norm_a, gla_w_f, gla_b_f, gla_w_b, gla_b_b, norm_b, w_out, ln2, w_router, w1, w3, w2, ln_f):
    w_main, w_small, small_params = _prep_in_weights(w_in[0], a_log_f[0], a_log_b[0],
                                                     dt_bias_f[0], dt_bias_b[0])
    conv8 = _conv_layout(conv_w[0])
    wg, bg = _gla_gate_weights(gla_w_f[0], gla_b_f[0], gla_w_b[0], gla_b_b[0])
    wl = jnp.asarray(_gla_level_weights(), BF16)
    wo = w_out[0].astype(BF16)
    wr_t = w_router[0].astype(F32).T
    wrh = wr_t.astype(BF16)
    wrl = (wr_t - wrh.astype(F32)).astype(BF16)
    wr_pad = jnp.pad(w_router[0].astype(F32), ((0, 0), (0, GATE_LANES - N_EXPERTS)))
    wrh2 = wr_pad.astype(BF16)
    wrl2 = (wr_pad - wrh2.astype(F32)).astype(BF16)
    w1b, w3b, w2b = w1[0].astype(BF16), w3[0].astype(BF16), w2[0].astype(BF16)
    row = lambda v: v.astype(F32).reshape(1, -1)
    routed = []
    for x in (x_prompt, x_sample):
        bsz, seq, _ = x.shape
        n = bsz * seq
        cap = EC_CAPACITY * n // N_EXPERTS
        x2d = x.reshape(n, D_MODEL)
        main, small = _in_proj(x2d, row(ln1[0]), w_main, w_small, small_params, tm=1024, tn=1792)
        main3 = main.reshape(bsz, seq, MAIN_WIDTH)
        small3 = small.reshape(bsz, seq, SMALL_WIDTH)
        oaf, oab = _gdn(_qkv_conv(main3, conv8, rows=SCAN_ROWS), _scal_layout(small3))
        obf, obb = _gla(main3, small3, wg, bg, wl)
        flat = lambda o: o.reshape(n, -1)
        h, hn, probs_t, p3 = _out_proj(flat(oaf), flat(oab), flat(obf), flat(obb), main, x2d, wo,
                                       row(norm_a[0]), row(norm_b[0]), row(ln2[0]),
                                       wrh, wrl, wrh2, wrl2, tm=256)
        slot, ts = _topk(probs_t, cap)
        dsp_steps, n_dsp = _route_steps(ts, cap, DSP_TOK)
        cmb_steps, n_cmb = _combine_steps(ts, cap)
        xs = _dispatch(hn, p3, slot, dsp_steps, n_dsp, cap)
        routed.append((x.shape, cap, xs, slot, cmb_steps, n_cmb, h))
    (_, cap_p, xs_p, *_), (_, cap_s, xs_s, *_) = routed
    ys = [_expert_mlp(xs_p, xs_s, w1b, w3b, w2b, e, cap_p, cap_s) for e in range(N_EXPERTS)]
    outs = []
    for k, (shape, cap, xs, slot, cmb_steps, n_cmb, h) in enumerate(routed):
        out = _combine3([y[k] for y in ys], slot, cmb_steps, n_cmb, h, row(ln_f))
        outs.append(out.reshape(shape))
    return tuple(outs)
```

```python
import functools

import jax
import jax.numpy as jnp
import numpy as np
from jax import lax
from jax.experimental import pallas as pl
from jax.experimental.pallas import tpu as pltpu

F32 = jnp.float32
BF16 = jnp.bfloat16

D_MODEL = 2048
A_HEADS, A_DK, A_DV = 8, 128, 128
B_HEADS, B_DK, B_DV = 4, 128, 256
GLA_LOWRANK = 16
GLA_GATE_NORM = 16.0
CONV_K = 5
CHUNK = 64
N_EXPERTS = 16
EC_CAPACITY = 2
EXPERT_FF = D_MODEL // 2
EPS = 1e-6

A_QK = A_HEADS * A_DK
A_V = A_HEADS * A_DV
B_QK = B_HEADS * B_DK
B_V = B_HEADS * B_DV
MAIN_WIDTH = 2 * A_QK + 2 * A_V + 2 * B_QK + 2 * B_V
SMALL_WIDTH = 128
LR_OFF = 4 * A_HEADS
V7X_VMEM_LIMIT = 56 * 1024 * 1024
CONV_HALO = 16
CONV_ROWS = 256
SCAN_ROWS = 512
CPS = SCAN_ROWS // CHUNK
GLA_LEVELS = 6
GATE_LANES = 128
GDN_HEADS_PER_PASS = 8


def _cparams(sem):
    return pltpu.CompilerParams(dimension_semantics=sem, vmem_limit_bytes=V7X_VMEM_LIMIT)


def _dot(a, b):
    return jnp.dot(a.astype(BF16), b.astype(BF16), preferred_element_type=F32)


def _dot_nt(a, b):
    return lax.dot_general(a.astype(BF16), b.astype(BF16), (((1,), (1,)), ((), ())),
                           preferred_element_type=F32)


def _dot_tn(a, b):
    return lax.dot_general(a.astype(BF16), b.astype(BF16), (((0,), (0,)), ((), ())),
                           preferred_element_type=F32)


def _silu(x):
    return x * (1.0 / (1.0 + jnp.exp(-x)))


def _in_proj_kernel(x_ref, ln_ref, wm_ref, ws_ref, sp_ref, main_ref, small_ref, hn_ref):
    j = pl.program_id(1)

    @pl.when(j == 0)
    def _():
        xf = x_ref[...]
        y = xf * lax.rsqrt(jnp.mean(xf * xf, axis=-1, keepdims=True) + EPS) * ln_ref[...]
        hn = y.astype(BF16)
        hn_ref[...] = hn
        s = jnp.dot(hn, ws_ref[...], preferred_element_type=F32)
        lane = lax.broadcasted_iota(jnp.int32, s.shape, 1)
        neg_a = -jnp.exp(sp_ref[0:1, :])
        z = s + sp_ref[1:2, :]
        softplus = jnp.maximum(z, 0.0) + jnp.log(1.0 + jnp.exp(-jnp.abs(z)))
        sig = 1.0 / (1.0 + jnp.exp(-s))
        small_ref[...] = jnp.where(lane < 2 * A_HEADS, sig,
                                   jnp.where(lane < 4 * A_HEADS, neg_a * softplus, s))

    main_ref[...] = jnp.dot(hn_ref[...], wm_ref[...], preferred_element_type=F32).astype(BF16)


def _in_proj(x2d, ln1, w_main, w_small, small_params, tm, tn):
    t = x2d.shape[0]
    return pl.pallas_call(
        _in_proj_kernel,
        grid=(t // tm, MAIN_WIDTH // tn),
        in_specs=[
            pl.BlockSpec((tm, D_MODEL), lambda i, j: (i, 0)),
            pl.BlockSpec((1, D_MODEL), lambda i, j: (0, 0)),
            pl.BlockSpec((D_MODEL, tn), lambda i, j: (0, j)),
            pl.BlockSpec((D_MODEL, SMALL_WIDTH), lambda i, j: (0, 0)),
            pl.BlockSpec((8, SMALL_WIDTH), lambda i, j: (0, 0)),
        ],
        out_specs=[
            pl.BlockSpec((tm, tn), lambda i, j: (i, j)),
            pl.BlockSpec((tm, SMALL_WIDTH), lambda i, j: (i, 0)),
        ],
        out_shape=[
            jax.ShapeDtypeStruct((t, MAIN_WIDTH), BF16),
            jax.ShapeDtypeStruct((t, SMALL_WIDTH), F32),
        ],
        scratch_shapes=[pltpu.VMEM((tm, D_MODEL), BF16)],
        compiler_params=_cparams(("arbitrary", "arbitrary")),
        name="in_proj",
    )(x2d, ln1, w_main, w_small, small_params)


def _chunk_iotas():
    ii = lax.broadcasted_iota(jnp.int32, (CHUNK, CHUNK), 0)
    jj = lax.broadcasted_iota(jnp.int32, (CHUNK, CHUNK), 1)
    return ii, jj


def _row_to_col(row, eye):
    return jnp.sum(jnp.where(eye, row, 0.0), axis=1, keepdims=True)


def _col_to_row(col, eye):
    return jnp.sum(jnp.where(eye, col, 0.0), axis=0, keepdims=True)


def _l2norm(x):
    return x * lax.rsqrt(jnp.sum(x * x, axis=-1, keepdims=True) + EPS)


def _delta_chunks(qs, ks, vs, beta_rows, g_rows, states, revs):
    ii, jj = _chunk_iotas()
    eye = ii == jj
    nc = range(len(qs))
    incl = [(ii <= jj) if r else (ii >= jj) for r in revs]
    strict = [(ii < jj) if r else (ii > jj) for r in revs]
    gc_col = [jnp.sum(jnp.where(incl[c], g_rows[c], 0.0), axis=1, keepdims=True) for c in nc]
    gc_row = [_col_to_row(gc_col[c], eye) for c in nc]
    beta_col = [_row_to_col(beta_rows[c], eye) for c in nc]
    g_tot = [jnp.sum(g_rows[c], axis=1, keepdims=True) for c in nc]
    decay = [jnp.where(incl[c], jnp.exp(jnp.where(incl[c], gc_col[c] - gc_row[c], 0.0)), 0.0)
             for c in nc]
    kb = [ks[c] * beta_col[c] for c in nc]
    kq = [_dot_nt(jnp.concatenate([kb[c], qs[c]], axis=0), ks[c]) for c in nc]
    kk = [kq[c][:CHUNK] for c in nc]
    qk = [kq[c][CHUNK:] for c in nc]
    p = [jnp.where(strict[c], -kk[c] * decay[c], 0.0) for c in nc]
    toff = p
    p = [_dot(p[c], p[c]) for c in nc]
    for _ in range(4):
        r = [_dot(jnp.concatenate([p[c], toff[c]], axis=0), p[c]) for c in nc]
        toff = [toff[c] + p[c] + r[c][CHUNK:] for c in nc]
        p = [r[c][:CHUNK] for c in nc]
    tp = [_dot(toff[c], p[c]) for c in nc]
    toff = [toff[c] + p[c] + tp[c] for c in nc]
    e_gc = [jnp.exp(gc_col[c]) for c in nc]
    rhs = [jnp.concatenate([vs[c] * beta_col[c], kb[c] * e_gc[c]], axis=1) for c in nc]
    sol = [rhs[c] + _dot(toff[c], rhs[c]) for c in nc]
    attn = [jnp.where(incl[c], qk[c] * decay[c], 0.0) for c in nc]
    wq = [_dot(jnp.concatenate([sol[c][:, A_DV:], qs[c] * e_gc[c]], axis=0), states[c]) for c in nc]
    ws = [wq[c][:CHUNK] for c in nc]
    qs_state = [wq[c][CHUNK:] for c in nc]
    v_new = [sol[c][:, :A_DV] - ws[c] for c in nc]
    av = [_dot(attn[c], v_new[c]) for c in nc]
    kv = [_dot_tn(ks[c] * jnp.exp(g_tot[c] - gc_col[c]), v_new[c]) for c in nc]
    outs = [qs_state[c] + av[c] for c in nc]
    new_states = [states[c] * jnp.exp(g_tot[c]) + kv[c] for c in nc]
    return outs, new_states


def _qkv_conv_kernel(blk, prv, nxt, cw_ref, out_ref, pad_ref):
    n = pl.program_id(1)
    nb = pl.num_programs(1)
    rows = blk.shape[1]
    pad_ref[0:CONV_HALO, :] = jnp.where(n == 0, 0.0, prv[0].astype(F32))
    pad_ref[CONV_HALO:CONV_HALO + rows, :] = blk[0].astype(F32)
    pad_ref[CONV_HALO + rows:, :] = jnp.where(n == nb - 1, 0.0, nxt[0].astype(F32))
    lo = CONV_HALO - CONV_K // 2
    for g in range((2 * A_QK + A_V) // A_DK):
        cols = slice(g * A_DK, (g + 1) * A_DK)
        w = cw_ref[:, cols]
        win = pad_ref[:, cols]
        total = win.shape[0]
        acc = None
        for j in range(CONV_K):
            shift = (CONV_HALO - lo - j) % total
            tap = win if shift == 0 else pltpu.roll(win, shift, 0)
            term = w[j:j + 1, :] * tap[CONV_HALO:CONV_HALO + rows, :]
            acc = term if acc is None else acc + term
        y = _silu(acc)
        if g < A_HEADS:
            y = _l2norm(y) * (A_DK ** -0.5)
        elif g < 2 * A_HEADS:
            y = _l2norm(y)
        out_ref[0, :, cols] = y.astype(out_ref.dtype)


def _qkv_conv(main3, conv8, rows):
    bsz, seq, _ = main3.shape
    width = 2 * A_QK + A_V
    hb = rows // CONV_HALO
    nhalo = seq // CONV_HALO
    return pl.pallas_call(
        _qkv_conv_kernel,
        grid=(bsz, seq // rows),
        in_specs=[
            pl.BlockSpec((1, rows, width), lambda b, n: (b, n, 0)),
            pl.BlockSpec((1, CONV_HALO, width), lambda b, n: (b, jnp.maximum(n * hb - 1, 0), 0)),
            pl.BlockSpec((1, CONV_HALO, width),
                         lambda b, n: (b, jnp.minimum((n + 1) * hb, nhalo - 1), 0)),
            pl.BlockSpec((8, width), lambda b, n: (0, 0)),
        ],
        out_specs=pl.BlockSpec((1, rows, width), lambda b, n: (b, n, 0)),
        out_shape=jax.ShapeDtypeStruct((bsz, seq, width), BF16),
        scratch_shapes=[pltpu.VMEM((rows + 2 * CONV_HALO, width), F32)],
        compiler_params=_cparams(("arbitrary", "arbitrary")),
        name="qkv_conv",
    )(main3, main3, main3, conv8)


def _gdn_kernel(blk_f, blk_b, sc_f, sc_b, of_ref, ob_ref, st_ref):
    n = pl.program_id(1)

    @pl.when(n == 0)
    def _():
        st_ref[...] = jnp.zeros_like(st_ref)

    def load(d, row0, a, h):
        blk = blk_f if d == 0 else blk_b
        col = a * A_QK + h * A_DK
        return blk[0, pl.ds(row0, CHUNK), col:col + A_DK].astype(F32)

    def body(s, carry):
        for h0 in range(0, A_HEADS, GDN_HEADS_PER_PASS):
            chains = []
            for d, sc_ref in enumerate((sc_f, sc_b)):
                sc = s if d == 0 else CPS - 1 - s
                row0 = pl.multiple_of(sc * CHUNK, CHUNK)
                for h in range(h0, h0 + GDN_HEADS_PER_PASS):
                    chains.append((d, h, sc, row0, sc_ref))
            qs = [load(d, row0, 0, h) for d, h, sc, row0, _ in chains]
            ks = [load(d, row0, 1, h) for d, h, sc, row0, _ in chains]
            vs = [load(d, row0, 2, h) for d, h, sc, row0, _ in chains]
            betas = [r[0, 0, d, h, pl.ds(sc, 1), :] for d, h, sc, row0, r in chains]
            gs = [r[0, 0, 2 + d, h, pl.ds(sc, 1), :] for d, h, sc, row0, r in chains]
            states = [st_ref[d, h] for d, h, sc, row0, _ in chains]
            outs, new_states = _delta_chunks(qs, ks, vs, betas, gs, states,
                                             [d == 1 for d, *_ in chains])
            for (d, h, sc, row0, _), o, st in zip(chains, outs, new_states):
                st_ref[d, h] = st
                out_ref = of_ref if d == 0 else ob_ref
                out_ref[0, pl.ds(row0, CHUNK), h * A_DV:(h + 1) * A_DV] = o.astype(out_ref.dtype)
        return carry

    lax.fori_loop(0, CPS, body, 0)


def _scal_layout(small):
    bsz, seq = small.shape[:2]
    s = small[..., :4 * A_HEADS].reshape(bsz, seq // SCAN_ROWS, CPS, CHUNK, 4, A_HEADS)
    return jnp.transpose(s, (0, 1, 4, 5, 2, 3))


def _conv_layout(conv_w):
    return jnp.pad(conv_w.astype(F32), ((0, 8 - CONV_K), (0, 0)))


def _gdn(qkv, scal):
    bsz, seq, width = qkv.shape
    nb = seq // SCAN_ROWS

    def specs(blk_of):
        return (pl.BlockSpec((1, SCAN_ROWS, width), lambda b, n: (b, blk_of(n), 0)),
                pl.BlockSpec((1, 1, 4, A_HEADS, CPS, CHUNK), lambda b, n: (b, blk_of(n), 0, 0, 0, 0)))

    qf, sf = specs(lambda n: n)
    qb, sb = specs(lambda n: nb - 1 - n)
    out_sds = jax.ShapeDtypeStruct((bsz, seq, A_V), BF16)
    return pl.pallas_call(
        _gdn_kernel,
        grid=(bsz, nb),
        in_specs=[qf, qb, sf, sb],
        out_specs=[
            pl.BlockSpec((1, SCAN_ROWS, A_V), lambda b, n: (b, n, 0)),
            pl.BlockSpec((1, SCAN_ROWS, A_V), lambda b, n: (b, nb - 1 - n, 0)),
        ],
        out_shape=[out_sds, out_sds],
        scratch_shapes=[pltpu.VMEM((2, A_HEADS, A_DK, A_DV), F32)],
        compiler_params=_cparams(("arbitrary", "arbitrary")),
        name="gdn",
    )(qkv, qkv, scal, scal)


GLA_BATCH = 2
GLA_MM_LEVELS = (4,)
GLA_W_ROWS = (len(GLA_MM_LEVELS) + 1) * CHUNK


def _gla_level_weights():
    c = CHUNK
    w = np.zeros((2, GLA_W_ROWS, c), np.float32)
    for d in range(2):
        for blk, l in enumerate(GLA_MM_LEVELS):
            s = c >> (l + 1)
            for i in range(c):
                mid = (i // (2 * s)) * 2 * s + s
                if d == 0:
                    ts = range(mid, i + 1) if i >= mid else range(i + 1, mid)
                else:
                    ts = range(i, mid) if i < mid else range(mid, i)
                for t in ts:
                    w[d, blk * c + i, t] = 1.0
        for i in range(c):
            for t in range(c):
                before = t <= i if d == 0 else t >= i
                w[d, GLA_W_ROWS - c + i, t] = 1.0 if before else 0.0
    return w


def _gla_kernel(qk_f, v_f, sm_f, qk_b, v_b, sm_b, wg_ref, bg_ref, wl_ref, of_ref, ob_ref, st_ref):
    n = pl.program_id(1)

    @pl.when(n == 0)
    def _():
        st_ref[...] = jnp.zeros_like(st_ref)

    ii, jj = _chunk_iotas()
    eye = ii == jj
    tok = lax.broadcasted_iota(jnp.int32, (CHUNK, 1), 0)

    streams = ((qk_f, v_f, sm_f, of_ref), (qk_b, v_b, sm_b, ob_ref))
    lanes = [(bi, d) for bi in range(qk_f.shape[0]) for d in range(2)]
    dirs = range(len(lanes))
    chains = [(p, h) for p in dirs for h in range(B_HEADS)]
    masks = [ii // (CHUNK >> l) == jj // (CHUNK >> l) for l in range(GLA_LEVELS)]

    def body(s, carry):
        rows = [pl.ds(pl.multiple_of((s if d == 0 else CPS - 1 - s) * CHUNK, CHUNK), CHUNK)
                for _, d in lanes]

        def blk(p, which, cols):
            bi, d = lanes[p]
            return streams[d][which][bi, rows[p], cols]

        z = [_dot(blk(p, 2, slice(None)), wg_ref[lanes[p][1]]) + bg_ref[lanes[p][1]]
             for p in dirs]
        la = [(jnp.minimum(z[d], 0.0) - jnp.log(1.0 + jnp.exp(-jnp.abs(z[d]))))
              * (1.0 / GLA_GATE_NORM) for d in dirs]
        hi = [la[d].astype(BF16) for d in dirs]
        r1 = [la[d] - hi[d].astype(F32) for d in dirs]
        mid = [r1[d].astype(BF16) for d in dirs]
        low = [(r1[d] - mid[d].astype(F32)).astype(BF16) for d in dirs]
        ex = [jnp.dot(wl_ref[lanes[d][1]], hi[d], preferred_element_type=F32)
              + jnp.dot(wl_ref[lanes[d][1]], mid[d], preferred_element_type=F32)
              + jnp.dot(wl_ref[lanes[d][1]], low[d], preferred_element_type=F32)
              for d in dirs]
        bcum = [ex[d][GLA_W_ROWS - CHUNK:] for d in dirs]
        b_tot = [bcum[p][CHUNK - 1:CHUNK] if lanes[p][1] == 0 else bcum[p][0:1] for p in dirs]
        q_all = [blk(p, 0, slice(0, B_QK)).astype(F32) * (B_DK ** -0.5) for p in dirs]
        k_all = [blk(p, 0, slice(B_QK, 2 * B_QK)).astype(F32) for p in dirs]
        qs, ks = [], []
        for l in range(GLA_LEVELS):
            half = CHUNK >> (l + 1)
            right = (tok // half) % 2 == 1
            q_side = [right if d == 0 else jnp.logical_not(right) for _, d in lanes]
            if l in GLA_MM_LEVELS:
                wrow = GLA_MM_LEVELS.index(l) * CHUNK
                expo = [ex[d][wrow:wrow + CHUNK] for d in dirs]
            elif 2 * half < 8:
                expo = []
                pos = tok & (2 * half - 1)
                for d in dirs:
                    rev = lanes[d][1]
                    ref = None
                    for rr in range(2 * half):
                        off = half - 1 + rev - rr
                        cand = bcum[d] if off == 0 else pltpu.roll(bcum[d], (-off) % CHUNK, 0)
                        ref = cand if ref is None else jnp.where(pos == rr, cand, ref)
                    expo.append(jnp.where(q_side[d], bcum[d] - ref, ref - bcum[d]))
            else:
                expo = []
                for d in dirs:
                    ref = jnp.concatenate(
                        [jnp.broadcast_to(bcum[d][g + half - 1 + lanes[d][1]:g + half + lanes[d][1]],
                                          (2 * half, B_QK))
                         for g in range(0, CHUNK, 2 * half)], axis=0)
                    expo.append(jnp.where(q_side[d], bcum[d] - ref, ref - bcum[d]))
            e_l = [jnp.exp(expo[d]) for d in dirs]
            qs.append([jnp.where(q_side[d], q_all[d] * e_l[d], 0.0).astype(BF16) for d in dirs])
            ks.append([jnp.where(q_side[d], 0.0, k_all[d] * e_l[d]).astype(BF16) for d in dirs])
        q_dec = [(q_all[d] * jnp.exp(bcum[d])).astype(BF16) for d in dirs]
        k_dec = [(k_all[d] * jnp.exp(b_tot[d] - bcum[d])).astype(BF16) for d in dirs]
        e_tot = [jnp.exp(b_tot[d]) for d in dirs]
        ck = [slice(h * B_DK, (h + 1) * B_DK) for h in range(B_HEADS)]
        cv = [slice(h * B_DV, (h + 1) * B_DV) for h in range(B_HEADS)]
        attn = [jnp.where(eye, _dot_nt(q_all[d][:, ck[h]], k_all[d][:, ck[h]]), 0.0)
                for d, h in chains]
        for l in range(GLA_LEVELS):
            part = [_dot_nt(qs[l][d][:, ck[h]], ks[l][d][:, ck[h]]) for d, h in chains]
            attn = [attn[c] + jnp.where(masks[l], part[c], 0.0) for c in range(len(chains))]
        v = [blk(p, 1, cv[h]) for p, h in chains]
        st = [st_ref[lanes[p][0], lanes[p][1], h] for p, h in chains]
        o_state = [_dot_nt(q_dec[d][:, ck[h]], st[c]) for c, (d, h) in enumerate(chains)]
        o_local = [_dot(attn[c], v[c]) for c in range(len(chains))]
        kv = [_dot_tn(v[c], k_dec[d][:, ck[h]]) for c, (d, h) in enumerate(chains)]
        for c, (p, h) in enumerate(chains):
            bi, d = lanes[p]
            st_ref[bi, d, h] = st[c] * e_tot[p][:, ck[h]] + kv[c]
            streams[d][3][bi, rows[p], cv[h]] = (o_state[c] + o_local[c]).astype(of_ref.dtype)
        return carry

    lax.fori_loop(0, CPS, body, 0)


def _gla(main3, small3, wg, bg, wl):
    bsz, seq, _ = main3.shape
    nb = seq // SCAN_ROWS
    qk_blk = (2 * A_QK + 2 * A_V) // (2 * B_QK)
    v_blk = (2 * A_QK + 2 * A_V + 2 * B_QK) // B_V

    gb = GLA_BATCH
    assert bsz % gb == 0 and seq % SCAN_ROWS == 0, (bsz, seq)

    def stream(blk_of):
        return [
            pl.BlockSpec((gb, SCAN_ROWS, 2 * B_QK), lambda b, n: (b, blk_of(n), qk_blk)),
            pl.BlockSpec((gb, SCAN_ROWS, B_V), lambda b, n: (b, blk_of(n), v_blk)),
            pl.BlockSpec((gb, SCAN_ROWS, SMALL_WIDTH), lambda b, n: (b, blk_of(n), 0)),
        ]

    out_sds = jax.ShapeDtypeStruct((bsz, seq, B_V), BF16)
    return pl.pallas_call(
        _gla_kernel,
        grid=(bsz // gb, nb),
        in_specs=stream(lambda n: n) + stream(lambda n: nb - 1 - n) + [
            pl.BlockSpec((2, SMALL_WIDTH, B_QK), lambda b, n: (0, 0, 0)),
            pl.BlockSpec((2, 1, B_QK), lambda b, n: (0, 0, 0)),
            pl.BlockSpec((2, GLA_W_ROWS, CHUNK), lambda b, n: (0, 0, 0)),
        ],
        out_specs=[
            pl.BlockSpec((gb, SCAN_ROWS, B_V), lambda b, n: (b, n, 0)),
            pl.BlockSpec((gb, SCAN_ROWS, B_V), lambda b, n: (b, nb - 1 - n, 0)),
        ],
        out_shape=[out_sds, out_sds],
        scratch_shapes=[pltpu.VMEM((gb, 2, B_HEADS, B_DV, B_DK), F32)],
        compiler_params=_cparams(("arbitrary", "arbitrary")),
        name="gla",
    )(main3, main3, small3, main3, main3, small3, wg, bg, wl)


def _gla_gate_weights(gla_w_f, gla_b_f, gla_w_b, gla_b_b):
    wg = jnp.zeros((2, SMALL_WIDTH, B_QK), F32)
    wg = wg.at[0, LR_OFF:LR_OFF + GLA_LOWRANK].set(gla_w_f.astype(F32))
    wg = wg.at[1, LR_OFF + GLA_LOWRANK:LR_OFF + 2 * GLA_LOWRANK].set(gla_w_b.astype(F32))
    bg = jnp.stack([gla_b_f, gla_b_b]).astype(F32).reshape(2, 1, B_QK)
    return wg.astype(BF16), bg


def _out_proj_kernel(oaf, oab, obf, obb, ga, gb, x_ref, wo_ref, na_ref, nb_ref, ln2_ref,
                     wrh_ref, wrl_ref, wrh2_ref, wrl2_ref, h_ref, hn_ref, pt_ref, p3_ref,
                     mix_ref, hres_ref):
    s = pl.program_id(0)

    @pl.when(s == 0)
    def _():
        mix_ref[...] = jnp.zeros_like(mix_ref)
        hres_ref[...] = jnp.zeros_like(hres_ref)

    cur, prv = 0, 1
    mix_ref[prv] = mix_ref[cur]
    hres_ref[prv] = hres_ref[cur]

    def head_norm(of_ref, ob_ref, g_ref, w_ref, width, h, base):
        c = slice(h * width, (h + 1) * width)
        o = of_ref[:, c].astype(F32) + ob_ref[:, c].astype(F32)
        y = o * lax.rsqrt(jnp.mean(o * o, axis=-1, keepdims=True) + EPS) * w_ref[...]
        mix_ref[cur, :, base + h * width:base + (h + 1) * width] = (
            y * _silu(g_ref[:, c].astype(F32))).astype(BF16)

    def mix(q):
        for h in range(q * A_HEADS // OUT_PIECES, (q + 1) * A_HEADS // OUT_PIECES):
            head_norm(oaf, oab, ga, na_ref, A_DV, h, 0)
        for h in range(q * B_HEADS // OUT_PIECES, (q + 1) * B_HEADS // OUT_PIECES):
            head_norm(obf, obb, gb, nb_ref, B_DV, h, A_V)

    def project(q):
        nsl = slice(q * D_MODEL // OUT_PIECES, (q + 1) * D_MODEL // OUT_PIECES)
        hres_ref[cur, :, nsl] = x_ref[:, nsl] + jnp.dot(mix_ref[prv], wo_ref[:, nsl],
                                                        preferred_element_type=F32)

    def route(c):
        rows = x_ref.shape[0] // 2
        rs = slice(c * rows, (c + 1) * rows)
        hres = hres_ref[prv, rs, :]
        h_ref[rs, :] = hres
        hn = hres * lax.rsqrt(jnp.mean(hres * hres, axis=-1, keepdims=True) + EPS) * ln2_ref[...]
        hi = hn.astype(BF16)
        hn_ref[rs, :] = hi
        lo = (hn - hi.astype(F32)).astype(BF16)
        lt = _dot_nt(wrh_ref[...], hi) + _dot_nt(wrl_ref[...], hi) + _dot_nt(wrh_ref[...], lo)
        e = jnp.exp(lt - jnp.max(lt, axis=0, keepdims=True))
        pt_ref[:, rs] = e / jnp.sum(e, axis=0, keepdims=True)
        l2 = _dot(hi, wrh2_ref[...]) + _dot(hi, wrl2_ref[...]) + _dot(lo, wrh2_ref[...])
        lane = lax.broadcasted_iota(jnp.int32, l2.shape, 1)
        l2 = jnp.where(lane < N_EXPERTS, l2, -jnp.inf)
        e2 = jnp.exp(l2 - jnp.max(l2, axis=1, keepdims=True))
        p2 = e2 / jnp.sum(e2, axis=1, keepdims=True)
        g_hi = p2.astype(BF16)
        r1 = p2 - g_hi.astype(F32)
        g_mid = r1.astype(BF16)
        g_lo = (r1 - g_mid.astype(F32)).astype(BF16)
        p3 = (g_hi.astype(F32) + pltpu.roll(g_mid.astype(F32), N_EXPERTS, 1)
              + pltpu.roll(g_lo.astype(F32), 2 * N_EXPERTS, 1))
        p3_ref[rs, :] = p3.astype(BF16)

    route(0)
    for q in range(OUT_PIECES):
        mix(q)
        project(q)
        if q == OUT_PIECES // 2:
            route(1)


OUT_PIECES = 4


def _out_proj(oaf, oab, obf, obb, main, x2d, wo, na, nb, ln2, wrh, wrl, wrh2, wrl2, tm):
    t = x2d.shape[0]
    nt = t // tm
    ga_blk = (2 * A_QK + A_V) // A_V
    gb_blk = (2 * A_QK + 2 * A_V + 2 * B_QK + B_V) // B_V
    ahead = lambda s: jnp.minimum(s, nt - 1)
    mid = lambda s: jnp.clip(s - 1, 0, nt - 1)
    behind = lambda s: jnp.maximum(s - 2, 0)
    row_in = lambda w: pl.BlockSpec((tm, w), lambda s: (ahead(s), 0))
    row_out = lambda w: pl.BlockSpec((tm, w), lambda s: (behind(s), 0))
    full = lambda a, b: pl.BlockSpec((a, b), lambda s: (0, 0))
    return pl.pallas_call(
        _out_proj_kernel,
        grid=(nt + 2,),
        in_specs=[
            row_in(A_V), row_in(A_V), row_in(B_V), row_in(B_V),
            pl.BlockSpec((tm, A_V), lambda s: (ahead(s), ga_blk)),
            pl.BlockSpec((tm, B_V), lambda s: (ahead(s), gb_blk)),
            pl.BlockSpec((tm, D_MODEL), lambda s: (mid(s), 0)),
            full(D_MODEL, D_MODEL), full(1, A_DV), full(1, B_DV), full(1, D_MODEL),
            full(N_EXPERTS, D_MODEL), full(N_EXPERTS, D_MODEL),
            full(D_MODEL, GATE_LANES), full(D_MODEL, GATE_LANES),
        ],
        out_specs=[row_out(D_MODEL), row_out(D_MODEL),
                   pl.BlockSpec((N_EXPERTS, tm), lambda s: (0, behind(s))), row_out(GATE_LANES)],
        out_shape=[
            jax.ShapeDtypeStruct((t, D_MODEL), F32),
            jax.ShapeDtypeStruct((t, D_MODEL), BF16),
            jax.ShapeDtypeStruct((N_EXPERTS, t), F32),
            jax.ShapeDtypeStruct((t, GATE_LANES), BF16),
        ],
        scratch_shapes=[pltpu.VMEM((2, tm, D_MODEL), BF16), pltpu.VMEM((2, tm, D_MODEL), F32)],
        compiler_params=_cparams(("arbitrary",)),
        name="out_proj",
    )(oaf, oab, obf, obb, main, main, x2d, wo, na, nb, ln2, wrh, wrl, wrh2, wrl2)


TOPK_LANES = 256


def _topk_kernel(p_ref, slot_ref, ts_ref, *, cap):
    n = p_ref.shape[1]
    n_steps = n // TOPK_LANES
    bits = pltpu.bitcast(p_ref[...], jnp.int32)

    def bisect(i, thr):
        cand = thr | jnp.left_shift(jnp.int32(1), 30 - i)
        cnt = jnp.sum(jnp.where(bits >= cand, 1.0, 0.0), axis=1, keepdims=True)
        return jnp.where(cnt >= cap, cand, thr)

    thr = lax.fori_loop(0, 31, bisect, jnp.zeros((N_EXPERTS, 1), jnp.int32))
    need = cap - jnp.sum(jnp.where(bits > thr, 1.0, 0.0), axis=1, keepdims=True)
    r = lax.broadcasted_iota(jnp.int32, (TOPK_LANES, TOPK_LANES), 0)
    c = lax.broadcasted_iota(jnp.int32, (TOPK_LANES, TOPK_LANES), 1)
    tri = jnp.where(r <= c, 1.0, 0.0).astype(BF16)
    step_lane = lax.broadcasted_iota(jnp.int32, (N_EXPERTS, n_steps), 1)

    ts_ref[...] = jnp.zeros_like(ts_ref)

    def step(j, carry):
        ties_before, sel_before = carry
        lanes = pl.ds(pl.multiple_of(j * TOPK_LANES, TOPK_LANES), TOPK_LANES)
        pb = pltpu.bitcast(p_ref[:, lanes], jnp.int32)
        tie = jnp.where(pb == thr, 1.0, 0.0)
        tie_incl = jnp.dot(tie.astype(BF16), tri, preferred_element_type=F32)
        sel = jnp.logical_or(pb > thr, jnp.logical_and(pb == thr, ties_before + tie_incl - tie < need))
        self_f = jnp.where(sel, 1.0, 0.0)
        sel_incl = jnp.dot(self_f.astype(BF16), tri, preferred_element_type=F32)
        pos = sel_before + sel_incl - self_f
        slot_ref[:, lanes] = jnp.where(sel, pos, -1.0).astype(jnp.int32)
        ts_ref[...] = jnp.where(step_lane == j, sel_before.astype(jnp.int32), ts_ref[...])
        return (ties_before + tie_incl[:, TOPK_LANES - 1:],
                sel_before + sel_incl[:, TOPK_LANES - 1:])

    zero = jnp.zeros((N_EXPERTS, 1), F32)
    lax.fori_loop(0, n_steps, step, (zero, zero))


def _topk(probs_t, cap):
    n = probs_t.shape[1]
    return pl.pallas_call(
        functools.partial(_topk_kernel, cap=cap),
        out_shape=[
            jax.ShapeDtypeStruct((N_EXPERTS, n), jnp.int32),
            jax.ShapeDtypeStruct((N_EXPERTS, n // TOPK_LANES), jnp.int32),
        ],
        compiler_params=pltpu.CompilerParams(vmem_limit_bytes=V7X_VMEM_LIMIT),
        name="topk",
    )(probs_t)


RT_TOK = TOPK_LANES
DSP_TOK = 2 * TOPK_LANES
RT_ROWS = 128
RT_SPECS = 2 * N_EXPERTS
RT_GROUP = 4
XS_WIDTH = D_MODEL + GATE_LANES
SF_VALID, SF_FIRST, SF_LAST = 1, 2, 4
BF_ACTIVE, BF_FIRST = 1, 2


def _route_steps(ts, cap, tok):
    ts = ts[:, ::tok // TOPK_LANES]
    n_sub = ts.shape[1]
    tse = jnp.concatenate([ts, jnp.full((N_EXPERTS, 1), cap, jnp.int32)], axis=1)
    a, b = tse[:, :-1], tse[:, 1:]
    kf = jnp.minimum(a, cap - 1) // RT_ROWS
    kl = jnp.where(b > a, (b - 1) // RT_ROWS, kf)
    rounds = jnp.max((kl - kf + 2) // 2, axis=0)
    incl = jnp.cumsum(rounds)
    total = incl[-1]
    n_steps = ((tok // RT_ROWS + 2) // 2) * n_sub
    idx = jnp.arange(n_steps, dtype=jnp.int32)
    valid = idx < total
    idc = jnp.minimum(idx, total - 1)
    j = jnp.sum(incl[None, :] <= idc[:, None], axis=1).astype(jnp.int32)
    r = idc - (incl[j] - rounds[j])
    rep = lambda v: jnp.repeat(jnp.take(v, j, axis=1), 2, axis=0)
    par = jnp.tile(jnp.arange(2, dtype=jnp.int32), N_EXPERTS)[:, None]
    k0 = rep(kf) + 2 * r[None, :]
    cand = k0 + (par - k0) % 2
    a_s, b_s, kl_s = rep(a), rep(b), rep(kl)
    active = valid[None, :] & (cand <= kl_s) & (b_s > a_s)
    first = active & (a_s <= cand * RT_ROWS) & (cand * RT_ROWS < b_s)
    held = lax.cummax(jnp.where(active, cand, -1), axis=1)
    held = jnp.where(held < 0, par, held)
    order = lambda v: v.reshape(-1).astype(jnp.int32)
    bflag = BF_ACTIVE * active + BF_FIRST * first
    sflag = jnp.where(valid, SF_VALID + SF_FIRST * (r == 0) + SF_LAST * (r == rounds[j] - 1), 0)
    return (j, order(held), order(bflag), sflag.astype(jnp.int32)), total.astype(jnp.int32)


def _route_groups():
    return [[(RT_GROUP // 2 * g + q // 2, q % 2) for q in range(RT_GROUP)]
            for g in range(RT_SPECS // RT_GROUP)]


def _spec_at(table, spec, i):
    return spec * (table.shape[0] // RT_SPECS) + i


def _one_hots(i, blk, flg, slot_ref, specs):
    iota = lax.broadcasted_iota(jnp.int32, (RT_ROWS, slot_ref.shape[1]), 0)
    ohs = []
    for e, par in specs:
        k = _spec_at(blk, e * 2 + par, i)
        s0 = jnp.where((flg[k] & BF_ACTIVE) != 0, blk[k] * RT_ROWS, -(1 << 30))
        ohs.append(jnp.where((iota + s0) == slot_ref[e:e + 1, :], 1.0, 0.0).astype(BF16))
    return jnp.concatenate(ohs, axis=0)


def _dispatch_kernel(tile, blk, flg, sflg, x_ref, p3_ref, slot_ref, *outs):
    i = pl.program_id(0)

    @pl.when(i == 0)
    def _():
        for out in outs:
            out[...] = jnp.zeros_like(out)

    @pl.when((sflg[i] & SF_VALID) != 0)
    def _():
        xa = jnp.concatenate([x_ref[...], p3_ref[...]], axis=1)
        for specs in _route_groups():
            res = jnp.dot(_one_hots(i, blk, flg, slot_ref, specs), xa, preferred_element_type=F32)
            for q, (e, par) in enumerate(specs):
                out = outs[e * 2 + par]
                first = (flg[_spec_at(flg, e * 2 + par, i)] & BF_FIRST) != 0
                prev = out[...]
                prev = jnp.where(first, jnp.zeros_like(prev), prev)
                out[...] = prev + res[q * RT_ROWS:(q + 1) * RT_ROWS].astype(out.dtype)


def _dispatch(hn, p3, slot, steps, n_steps, cap):
    def out_spec(e, par):
        return pl.BlockSpec((RT_ROWS, XS_WIDTH),
                            lambda i, t, b, f, s: (b[_spec_at(b, e * 2 + par, i)] // 2, 0))

    grid_spec = pltpu.PrefetchScalarGridSpec(
        num_scalar_prefetch=4,
        grid=(n_steps,),
        in_specs=[
            pl.BlockSpec((DSP_TOK, D_MODEL), lambda i, t, b, f, s: (t[i], 0)),
            pl.BlockSpec((DSP_TOK, GATE_LANES), lambda i, t, b, f, s: (t[i], 0)),
            pl.BlockSpec((N_EXPERTS, DSP_TOK), lambda i, t, b, f, s: (0, t[i])),
        ],
        out_specs=[out_spec(e, par) for e in range(N_EXPERTS) for par in range(2)],
    )
    sds = jax.ShapeDtypeStruct((cap // 2, XS_WIDTH), BF16)
    return pl.pallas_call(
        _dispatch_kernel,
        grid_spec=grid_spec,
        out_shape=[sds] * RT_SPECS,
        compiler_params=_cparams(("arbitrary",)),
        name="moe_dispatch",
    )(*steps, hn, p3, slot)


MLP_ROWS = 4 * RT_ROWS


def _expert_mlp_kernel(xe_p, xo_p, xe_s, xo_s, w1_ref, w3_ref, w2_ref, yp_ref, ys_ref, *, e, steps_p):
    def run(xe_ref, xo_ref, y_ref):
        r = RT_ROWS
        xa = jnp.concatenate([xe_ref[0:r], xo_ref[0:r], xe_ref[r:2 * r], xo_ref[r:2 * r]], axis=0)
        xs = xa[:, :D_MODEL]
        g3 = xa[:, D_MODEL:].astype(F32)
        lane = lax.broadcasted_iota(jnp.int32, g3.shape, 1)
        mine = jnp.logical_and((lane & (N_EXPERTS - 1)) == e, lane < 3 * N_EXPERTS)
        gate = jnp.sum(jnp.where(mine, g3, 0.0), axis=1, keepdims=True)
        h1 = jnp.dot(xs, w1_ref[0], preferred_element_type=F32)
        h3 = jnp.dot(xs, w3_ref[0], preferred_element_type=F32)
        hid = (_silu(h1) * h3).astype(BF16)
        y_ref[...] = (jnp.dot(hid, w2_ref[0], preferred_element_type=F32) * gate).astype(y_ref.dtype)

    m = pl.program_id(0)

    @pl.when(m < steps_p)
    def _():
        run(xe_p, xo_p, yp_ref)

    @pl.when(m >= steps_p)
    def _():
        run(xe_s, xo_s, ys_ref)


def _expert_mlp(xs_p, xs_s, w1, w3, w2, e, cap_p, cap_s):
    steps_p, steps_s = cap_p // MLP_ROWS, cap_s // MLP_ROWS
    first = lambda m: (jnp.minimum(m, steps_p - 1), 0)
    second = lambda m: (jnp.maximum(m - steps_p, 0), 0)
    half = lambda idx: pl.BlockSpec((MLP_ROWS // 2, XS_WIDTH), idx)
    return pl.pallas_call(
        functools.partial(_expert_mlp_kernel, e=e, steps_p=steps_p),
        grid=(steps_p + steps_s,),
        in_specs=[
            half(first), half(first), half(second), half(second),
            pl.BlockSpec((1, D_MODEL, EXPERT_FF), lambda m: (e, 0, 0)),
            pl.BlockSpec((1, D_MODEL, EXPERT_FF), lambda m: (e, 0, 0)),
            pl.BlockSpec((1, EXPERT_FF, D_MODEL), lambda m: (e, 0, 0)),
        ],
        out_specs=[pl.BlockSpec((MLP_ROWS, D_MODEL), first),
                   pl.BlockSpec((MLP_ROWS, D_MODEL), second)],
        out_shape=[jax.ShapeDtypeStruct((cap_p, D_MODEL), BF16),
                   jax.ShapeDtypeStruct((cap_s, D_MODEL), BF16)],
        compiler_params=_cparams(("arbitrary",)),
        name="expert_mlp",
    )(xs_p[2 * e], xs_p[2 * e + 1], xs_s[2 * e], xs_s[2 * e + 1], w1, w3, w2)


CMB_WIN = 64
CMB_ALIGN = 16


def _combine_steps(ts, cap):
    n_sub = ts.shape[1]
    tse = jnp.concatenate([ts, jnp.full((N_EXPERTS, 1), cap, jnp.int32)], axis=1)
    a, b = tse[:, :-1], tse[:, 1:]
    a16 = (a // CMB_ALIGN) * CMB_ALIGN
    rounds = jnp.maximum(jnp.max((b - a16 + CMB_WIN - 1) // CMB_WIN, axis=0), 1)
    incl = jnp.cumsum(rounds)
    total = incl[-1]
    n_steps = ((RT_TOK + CMB_ALIGN) // CMB_WIN + 1) * n_sub
    idx = jnp.arange(n_steps, dtype=jnp.int32)
    valid = idx < total
    idc = jnp.minimum(idx, total - 1)
    j = jnp.sum(incl[None, :] <= idc[:, None], axis=1).astype(jnp.int32)
    r = idc - (incl[j] - rounds[j])
    base = jnp.take(a16, j, axis=1) + CMB_WIN * r[None, :]
    wstart = jnp.clip(base, 0, cap - CMB_WIN) // CMB_ALIGN
    order = lambda v: v.reshape(-1).astype(jnp.int32)
    sflag = jnp.where(valid, SF_VALID + SF_FIRST * (r == 0) + SF_LAST * (r == rounds[j] - 1), 0)
    return (j, order(wstart), order(base), sflag.astype(jnp.int32)), total.astype(jnp.int32)


def _combine3_kernel(tile, ws, base, sflg, slot_ref, h_ref, lnf_ref, *rest):
    ys = rest[:N_EXPERTS]
    out_ref, acc_ref = rest[N_EXPERTS], rest[N_EXPERTS + 1]
    i = pl.program_id(0)
    sf = sflg[i]

    @pl.when((sf & SF_FIRST) != 0)
    def _():
        acc_ref[...] = jnp.zeros_like(acc_ref)

    @pl.when((sf & SF_VALID) != 0)
    def _():
        iota = lax.broadcasted_iota(jnp.int32, (CMB_WIN, RT_TOK), 0)
        ohs = []
        for e in range(N_EXPERTS):
            k = e * (base.shape[0] // N_EXPERTS) + i
            rel = slot_ref[e:e + 1, :] - base[k]
            rel = jnp.where(rel >= 0, rel, -(1 << 30))
            shift = base[k] - ws[k] * CMB_ALIGN
            ohs.append(jnp.where((iota - shift) == rel, 1.0, 0.0).astype(BF16))
        ycat = jnp.concatenate([y[...] for y in ys], axis=0)
        acc_ref[...] += _dot_tn(jnp.concatenate(ohs, axis=0), ycat)

    @pl.when((sf & SF_LAST) != 0)
    def _():
        hh = h_ref[...] + acc_ref[...]
        out_ref[...] = hh * lax.rsqrt(jnp.mean(hh * hh, axis=-1, keepdims=True) + EPS) * lnf_ref[...]


def _combine3(ys, slot, steps, n_steps, h, ln_f):
    n = h.shape[0]

    def y_spec(e):
        return pl.BlockSpec((pl.Element(CMB_WIN), pl.Element(D_MODEL)),
                            lambda i, t, w, b, s: (w[e * (w.shape[0] // N_EXPERTS) + i] * CMB_ALIGN, 0))

    grid_spec = pltpu.PrefetchScalarGridSpec(
        num_scalar_prefetch=4,
        grid=(n_steps,),
        in_specs=[
            pl.BlockSpec((N_EXPERTS, RT_TOK), lambda i, t, w, b, s: (0, t[i])),
            pl.BlockSpec((RT_TOK, D_MODEL), lambda i, t, w, b, s: (t[i], 0)),
            pl.BlockSpec((1, D_MODEL), lambda i, t, w, b, s: (0, 0)),
        ] + [y_spec(e) for e in range(N_EXPERTS)],
        out_specs=pl.BlockSpec((RT_TOK, D_MODEL), lambda i, t, w, b, s: (t[i], 0)),
        scratch_shapes=[pltpu.VMEM((RT_TOK, D_MODEL), F32)],
    )
    return pl.pallas_call(
        _combine3_kernel,
        grid_spec=grid_spec,
        out_shape=jax.ShapeDtypeStruct((n, D_MODEL), F32),
        compiler_params=_cparams(("arbitrary",)),
        name="moe_combine",
    )(*steps, slot, h, ln_f, *ys)


def _prep_in_weights(w_in, a_log_f, a_log_b, dt_bias_f, dt_bias_b):
    offs = np.cumsum([0, A_QK, A_QK, A_V, A_V, A_HEADS, A_HEADS, A_HEADS, A_HEADS,
                      B_QK, B_QK, B_V, B_V, GLA_LOWRANK, GLA_LOWRANK])
    seg = [w_in[:, offs[i]:offs[i + 1]] for i in range(14)]
    w_main = jnp.concatenate(seg[0:4] + seg[8:12], axis=1).astype(BF16)
    w_small = jnp.concatenate(seg[4:8] + seg[12:14], axis=1)
    w_small = jnp.pad(w_small, ((0, 0), (0, SMALL_WIDTH - w_small.shape[1]))).astype(BF16)
    pad = SMALL_WIDTH - 4 * A_HEADS
    z = jnp.zeros((2 * A_HEADS,), F32)
    a_row = jnp.concatenate([z, a_log_f.astype(F32), a_log_b.astype(F32), jnp.zeros((pad,), F32)])
    dt_row = jnp.concatenate([z, dt_bias_f.astype(F32), dt_bias_b.astype(F32), jnp.zeros((pad,), F32)])
    small_params = jnp.zeros((8, SMALL_WIDTH), F32).at[0].set(a_row).at[1].set(dt_row)
    return w_main, w_small, small_params


def kernel(x_prompt, x_sample, ln1, w_in, conv_w, a_log_f, a_log_b, dt_bias_f, dt_bias_b, norm_a, gla_w_f, gla_b_f, gla_w_b, gla_b_b, norm_b, w_out, ln2, w_router, w1, w3, w2, ln_f):
    w_main, w_small, small_params = _prep_in_weights(w_in[0], a_log_f[0], a_log_b[0],
                                                     dt_bias_f[0], dt_bias_b[0])
    conv8 = _conv_layout(conv_w[0])
    wg, bg = _gla_gate_weights(gla_w_f[0], gla_b_f[0], gla_w_b[0], gla_b_b[0])
    wl = jnp.asarray(_gla_level_weights(), BF16)
    wo = w_out[0].astype(BF16)
    wr_t = w_router[0].astype(F32).T
    wrh = wr_t.astype(BF16)
    wrl = (wr_t - wrh.astype(F32)).astype(BF16)
    wr_pad = jnp.pad(w_router[0].astype(F32), ((0, 0), (0, GATE_LANES - N_EXPERTS)))
    wrh2 = wr_pad.astype(BF16)
    wrl2 = (wr_pad - wrh2.astype(F32)).astype(BF16)
    w1b, w3b, w2b = w1[0].astype(BF16), w3[0].astype(BF16), w2[0].astype(BF16)
    row = lambda v: v.astype(F32).reshape(1, -1)
    routed = []
    for x in (x_prompt, x_sample):
        bsz, seq, _ = x.shape
        n = bsz * seq
        cap = EC_CAPACITY * n // N_EXPERTS
        x2d = x.reshape(n, D_MODEL)
        main, small = _in_proj(x2d, row(ln1[0]), w_main, w_small, small_params, tm=1024, tn=1792)
        main3 = main.reshape(bsz, seq, MAIN_WIDTH)
        small3 = small.reshape(bsz, seq, SMALL_WIDTH)
        oaf, oab = _gdn(_qkv_conv(main3, conv8, rows=CONV_ROWS), _scal_layout(small3))
        obf, obb = _gla(main3, small3, wg, bg, wl)
        flat = lambda o: o.reshape(n, -1)
        h, hn, probs_t, p3 = _out_proj(flat(oaf), flat(oab), flat(obf), flat(obb), main, x2d, wo,
                                       row(norm_a[0]), row(norm_b[0]), row(ln2[0]),
                                       wrh, wrl, wrh2, wrl2, tm=256)
        slot, ts = _topk(probs_t, cap)
        dsp_steps, n_dsp = _route_steps(ts, cap, DSP_TOK)
        cmb_steps, n_cmb = _combine_steps(ts, cap)
        xs = _dispatch(hn, p3, slot, dsp_steps, n_dsp, cap)
        routed.append((x.shape, cap, xs, slot, cmb_steps, n_cmb, h))
    (_, cap_p, xs_p, *_), (_, cap_s, xs_s, *_) = routed
    ys = [_expert_mlp(xs_p, xs_s, w1b, w3b, w2b, e, cap_p, cap_s) for e in range(N_EXPERTS)]
    outs = []
    for k, (shape, cap, xs, slot, cmb_steps, n_cmb, h) in enumerate(routed):
        out = _combine3([y[k] for y in ys], slot, cmb_steps, n_cmb, h, row(ln_f))
        outs.append(out.reshape(shape))
    return tuple(outs)
```
